```python
import jax, jax.numpy as jnp
from jax import lax
import numpy as np

D_MODEL = 2048
BATCH = 1
SEQ = 8192
DEPTH = 2

CHUNK = 64
Q_BLOCK = 128
HEAD_DIM = 128
N_HEADS_A = 8
N_HEADS_B = 8
ROT_DIM = HEAD_DIM // 4
ROPE_THETA = 500000.0
N_IDX_HEADS = 16
IDX_DIM = 64
IDX_ROT_DIM = IDX_DIM // 4
TOPK_MAX = 256
POOL_GROUPS = 4
POOL_WINDOWS = (2, 4, 8, 16)
POOL_WIDTH = D_MODEL // POOL_GROUPS
D_FF = 5632
N_EXPERTS = 8
TOP_K_EXPERTS = 2
D_FF_EXPERT = 7168
N_EVEN = (DEPTH + 1) // 2
N_ODD = DEPTH // 2
RMS_EPS = 1e-6

IN_SPLITS = (
    N_HEADS_A * HEAD_DIM,
    N_HEADS_A * HEAD_DIM,
    N_HEADS_A * HEAD_DIM,
    N_IDX_HEADS * IDX_DIM,
    IDX_DIM,
    N_IDX_HEADS,
    N_HEADS_B * HEAD_DIM,
    N_HEADS_B * HEAD_DIM,
    N_HEADS_B * HEAD_DIM,
    N_HEADS_B,
    N_HEADS_B * HEAD_DIM,
)
IN_DIM = sum(IN_SPLITS)

kernel_name = "hybrid_dsa_fox_pool_moe_encoder"


def rms_norm(x, g):
    xf = x.astype(jnp.float32)
    y = xf * lax.rsqrt(jnp.mean(xf * xf, axis=-1, keepdims=True) + RMS_EPS)
    return (y * g.astype(jnp.float32)).astype(x.dtype)


def partial_rotary(x, rot_dim):
    S = x.shape[1]
    half = rot_dim // 2
    inv_freq = 1.0 / (ROPE_THETA ** (jnp.arange(half, dtype=jnp.float32) * 2.0 / rot_dim))
    ang = jnp.arange(S, dtype=jnp.float32)[:, None] * inv_freq[None, :]
    cos = jnp.cos(ang)[None, :, None, :]
    sin = jnp.sin(ang)[None, :, None, :]
    xf = x.astype(jnp.float32)
    x1, x2 = xf[..., :half], xf[..., half:rot_dim]
    out = jnp.concatenate([x1 * cos - x2 * sin, x2 * cos + x1 * sin, xf[..., rot_dim:]], axis=-1)
    return out.astype(x.dtype)


def to_blocks(a):
    nb = a.shape[1] // Q_BLOCK
    return jnp.moveaxis(a.reshape((a.shape[0], nb, Q_BLOCK) + a.shape[2:]), 1, 0)


def from_blocks(a):
    a = jnp.moveaxis(a, 0, 1)
    return a.reshape((a.shape[0], a.shape[1] * a.shape[2]) + a.shape[3:])


def dsa_sparse_attention(q, k, v, q_idx, k_idx, w_idx):
    B, S = q.shape[0], q.shape[1]
    top_k = min(TOPK_MAX, S // 4)
    key_chunk = jnp.arange(S) // CHUNK
    scale = HEAD_DIM ** -0.5
    b_ix = jnp.arange(B)[:, None, None]

    def block(args):
        qb, qib, wib, blk = args
        q_chunk = (blk * Q_BLOCK + jnp.arange(Q_BLOCK)) // CHUNK
        adm = key_chunk[None, :] <= q_chunk[:, None]
        rel = jax.nn.relu(jnp.einsum('bthd,bsd->bths', qib, k_idx).astype(jnp.float32))
        score = jnp.einsum('bths,bth->bts', rel, wib.astype(jnp.float32))
        score = jnp.where(adm[None], score, -jnp.inf)
        _, sel = lax.top_k(score, top_k)
        kg = k[b_ix, sel]
        vg = v[b_ix, sel]
        logits = jnp.einsum('bthd,btkhd->bhtk', qb, kg).astype(jnp.float32) * scale
        valid = (sel // CHUNK) <= q_chunk[None, :, None]
        logits = jnp.where(valid[:, None], logits, -jnp.inf)
        p = jax.nn.softmax(logits, axis=-1).astype(v.dtype)
        return jnp.einsum('bhtk,btkhd->bthd', p, vg)

    nb = S // Q_BLOCK
    out = lax.map(block, (to_blocks(q), to_blocks(q_idx), to_blocks(w_idx), jnp.arange(nb)))
    return from_blocks(out)


def forgetting_attention(q, k, v, log_f):
    S = q.shape[1]
    scale = HEAD_DIM ** -0.5
    c = jnp.cumsum(log_f, axis=1)
    c_keys = jnp.moveaxis(c, 1, 2)
    key_pos = jnp.arange(S)

    def block(args):
        qb, cb, blk = args
        q_pos = blk * Q_BLOCK + jnp.arange(Q_BLOCK)
        logits = jnp.einsum('bthd,bshd->bhts', qb, k).astype(jnp.float32) * scale
        logits = logits + jnp.moveaxis(cb, 1, 2)[..., None] - c_keys[:, :, None, :]
        causal = key_pos[None, :] <= q_pos[:, None]
        logits = jnp.where(causal, logits, -jnp.inf)
        p = jax.nn.softmax(logits, axis=-1).astype(v.dtype)
        return jnp.einsum('bhts,bshd->bthd', p, v)

    nb = S // Q_BLOCK
    out = lax.map(block, (to_blocks(q), to_blocks(c), jnp.arange(nb)))
    return from_blocks(out)


def even_mixer(h, w_in, b_forget, qn_a, kn_a, qn_b, kn_b, w_out):
    B, S, _ = h.shape
    proj = h @ w_in
    offs = np.cumsum(IN_SPLITS)[:-1].tolist()
    qa, ka, va, qi, ki, wi, qb, kb, vb, fb, gb = jnp.split(proj, offs, axis=-1)

    def heads(t, n, d):
        return t.reshape(B, S, n, d)

    qa = partial_rotary(rms_norm(heads(qa, N_HEADS_A, HEAD_DIM), qn_a), ROT_DIM)
    ka = partial_rotary(rms_norm(heads(ka, N_HEADS_A, HEAD_DIM), kn_a), ROT_DIM)
    va = heads(va, N_HEADS_A, HEAD_DIM)
    qi = partial_rotary(heads(qi, N_IDX_HEADS, IDX_DIM), IDX_ROT_DIM)
    ki = partial_rotary(ki.reshape(B, S, 1, IDX_DIM), IDX_ROT_DIM)[:, :, 0]
    wi = wi * (N_IDX_HEADS * IDX_DIM) ** -0.5
    o_a = dsa_sparse_attention(qa, ka, va, qi, ki, wi)

    qb = rms_norm(heads(qb, N_HEADS_B, HEAD_DIM), qn_b)
    kb = rms_norm(heads(kb, N_HEADS_B, HEAD_DIM), kn_b)
    vb = heads(vb, N_HEADS_B, HEAD_DIM)
    log_f = jax.nn.log_sigmoid(fb.astype(jnp.float32) + b_forget.astype(jnp.float32))
    o_b = forgetting_attention(qb, kb, vb, log_f) * jax.nn.sigmoid(heads(gb, N_HEADS_B, HEAD_DIM))

    o = jnp.concatenate([o_a.reshape(B, S, -1), o_b.reshape(B, S, -1)], axis=-1)
    return o @ w_out


def multiscale_pool_mixer(h, w_pool, pool_scale):
    B, S, D = h.shape
    hf = h.astype(jnp.float32).reshape(B, S, POOL_GROUPS, POOL_WIDTH)
    cs = jnp.concatenate([jnp.zeros_like(hf[:, :1]), jnp.cumsum(hf, axis=1)], axis=1)
    win = jnp.array(POOL_WINDOWS, dtype=jnp.int32)
    t = jnp.arange(S)
    start = jnp.maximum(t[:, None] + 1 - win[None, :], 0)
    window_sum = cs[:, 1:] - cs[:, start, jnp.arange(POOL_GROUPS)[None, :]]
    count = jnp.minimum(t[:, None] + 1, win[None, :]).astype(jnp.float32)
    pooled = window_sum / count[None, :, :, None] - hf
    y = jnp.einsum('bsgc,gcd->bsgd', pooled.astype(h.dtype), w_pool).reshape(B, S, D)
    return y * pool_scale


def swiglu(h, wg, wu, wd):
    return (jax.nn.silu(h @ wg) * (h @ wu)) @ wd


def moe_swiglu(h, router, wg, wu, wd):
    logits = (h @ router).astype(jnp.float32)
    top_vals, top_idx = lax.top_k(logits, TOP_K_EXPERTS)
    gates = jax.nn.softmax(top_vals, axis=-1)
    combine = jnp.sum(jax.nn.one_hot(top_idx, N_EXPERTS, dtype=jnp.float32) * gates[..., None],
                      axis=-2).astype(h.dtype)
    y = jnp.zeros_like(h)
    for e in range(N_EXPERTS):
        y = y + combine[..., e:e + 1] * swiglu(h, wg[e], wu[e], wd[e])
    return y


def setup_inputs(seed: int = 0) -> dict:
    key = jax.random.key(seed)
    ks = jax.random.split(key, 24)
    f32 = jnp.float32

    def nrm(k, shape, scale):
        return jax.random.normal(k, shape, f32) * scale

    def gain(k, shape, s=0.02):
        return 1.0 + s * jax.random.normal(k, shape, f32)

    E, O = N_EVEN, N_ODD
    return {
        "x": nrm(ks[0], (BATCH, SEQ, D_MODEL), 1.0),
        "attn_norm_e": gain(ks[1], (E, D_MODEL)),
        "w_in_e": nrm(ks[2], (E, D_MODEL, IN_DIM), D_MODEL ** -0.5),
        "b_forget_e": jax.random.uniform(ks[3], (E, N_HEADS_B), f32, 1.0, 4.0),
        "q_norm_a_e": gain(ks[4], (E, HEAD_DIM)),
        "k_norm_a_e": gain(ks[5], (E, HEAD_DIM)),
        "q_norm_b_e": gain(ks[6], (E, HEAD_DIM)),
        "k_norm_b_e": gain(ks[7], (E, HEAD_DIM)),
        "w_out_e": nrm(ks[8], (E, D_MODEL, D_MODEL), D_MODEL ** -0.5),
        "ffn_norm_e": gain(ks[9], (E, D_MODEL)),
        "w_gate_e": nrm(ks[10], (E, D_MODEL, D_FF), D_MODEL ** -0.5),
        "w_up_e": nrm(ks[11], (E, D_MODEL, D_FF), D_MODEL ** -0.5),
        "w_down_e": nrm(ks[12], (E, D_FF, D_MODEL), D_FF ** -0.5),
        "pool_norm_o": gain(ks[13], (O, D_MODEL)),
        "w_pool_o": nrm(ks[14], (O, POOL_GROUPS, POOL_WIDTH, POOL_WIDTH), POOL_WIDTH ** -0.5),
        "pool_scale_o": gain(ks[15], (O, D_MODEL), 0.1),
        "moe_norm_o": gain(ks[16], (O, D_MODEL)),
        "router_o": nrm(ks[17], (O, D_MODEL, N_EXPERTS), D_MODEL ** -0.5),
        "w_gate_o": nrm(ks[18], (O, N_EXPERTS, D_MODEL, D_FF_EXPERT), D_MODEL ** -0.5),
        "w_up_o": nrm(ks[19], (O, N_EXPERTS, D_MODEL, D_FF_EXPERT), D_MODEL ** -0.5),
        "w_down_o": nrm(ks[20], (O, N_EXPERTS, D_FF_EXPERT, D_MODEL), D_FF_EXPERT ** -0.5),
    }


def reference(x, attn_norm_e, w_in_e, b_forget_e, q_norm_a_e, k_norm_a_e, q_norm_b_e,
              k_norm_b_e, w_out_e, ffn_norm_e, w_gate_e, w_up_e, w_down_e, pool_norm_o,
              w_pool_o, pool_scale_o, moe_norm_o, router_o, w_gate_o, w_up_o, w_down_o):
    for layer in range(DEPTH):
        i = layer // 2
        if layer % 2 == 0:
            x = x + even_mixer(rms_norm(x, attn_norm_e[i]), w_in_e[i], b_forget_e[i],
                               q_norm_a_e[i], k_norm_a_e[i], q_norm_b_e[i], k_norm_b_e[i],
                               w_out_e[i])
            x = x + swiglu(rms_norm(x, ffn_norm_e[i]), w_gate_e[i], w_up_e[i], w_down_e[i])
        else:
            x = x + multiscale_pool_mixer(rms_norm(x, pool_norm_o[i]), w_pool_o[i], pool_scale_o[i])
            x = x + moe_swiglu(rms_norm(x, moe_norm_o[i]), router_o[i], w_gate_o[i], w_up_o[i],
                               w_down_o[i])
    return x
```

```python
import functools

import jax
import jax.numpy as jnp
import numpy as np
from jax import lax
from jax.experimental import pallas as pl
from jax.experimental.pallas import tpu as pltpu

HEAD_DIM = 128
N_HEADS_A = 8
N_HEADS_B = 8
N_IDX_HEADS = 16
IDX_DIM = 64
ROT_DIM = HEAD_DIM // 4
IDX_ROT_DIM = IDX_DIM // 4
ROPE_THETA = 500000.0
CHUNK = 64
TOPK_MAX = 256
POOL_WINDOWS = (2, 4, 8, 16)
RMS_EPS = 1e-6

LANES = 128
VMEM_LIMIT_BYTES = 56 * 1024 * 1024

KI_OFF = 0
WI_OFF = IDX_DIM
FB_OFF = 80

NEG_BIG = -1e30
INT_MIN = -(2 ** 31)
INT_MAX = 2 ** 31 - 1
KEY_NONE = INT_MIN + 1

F32 = jnp.float32
BF16 = jnp.bfloat16


def _cparams(sem):
    return pltpu.CompilerParams(dimension_semantics=sem, vmem_limit_bytes=VMEM_LIMIT_BYTES)


def _tile(n, pref):
    t = min(n, pref)
    while n % t:
        t //= 2
    return t


def _rms(x, g):
    ms = jnp.mean(x * x, axis=-1, keepdims=True)
    return x * lax.rsqrt(ms + RMS_EPS) * g


def _nt_dot(a, b):
    return lax.dot_general(a, b, (((1,), (1,)), ((), ())), preferred_element_type=F32)


def _rotate(y, cos_t, sin_t, half, period):
    lane = lax.broadcasted_iota(jnp.int32, y.shape, 1)
    up = pltpu.roll(y, LANES - half, 1)
    dn = pltpu.roll(y, half, 1)
    rot = jnp.where((lane & (period - 1)) < half, up, dn)
    return y * cos_t + rot * sin_t


def _in_proj_kernel(x_ref, g_ref, wm_ref, ws_ref, bf_ref, gains_ref, cosa_ref, sina_ref,
                    cosi_ref, sini_ref, main_ref, small_ref, h_scr, acc_scr, carry_scr,
                    *, n_seg, idx_scale, n_fb):
    i = pl.program_id(0)
    j = pl.program_id(1)
    tm = h_scr.shape[0]
    n_slab = acc_scr.shape[1] // LANES

    @pl.when(j == 0)
    def _():
        h_scr[...] = _rms(x_ref[...], g_ref[...]).astype(BF16)

    @pl.when(j < n_seg)
    def _():
        acc_scr[...] = jnp.dot(h_scr[...], wm_ref[...], preferred_element_type=F32)

    def head_norm(slab, gain):
        ms = jnp.mean(slab * slab, axis=-1, keepdims=True)
        return slab * lax.rsqrt(ms + RMS_EPS) * gain

    @pl.when(j < 2)
    def _():
        gain = gains_ref[pl.ds(j, 1), :]
        for s in range(n_slab):
            sl = slice(s * LANES, (s + 1) * LANES)
            y = head_norm(acc_scr[:, sl], gain)
            y = _rotate(y, cosa_ref[...], sina_ref[...], ROT_DIM // 2, LANES)
            main_ref[0, :, sl] = y.astype(BF16)

    @pl.when((j == 4) | (j == 5))
    def _():
        gain = gains_ref[pl.ds(j - 2, 1), :]
        for s in range(n_slab):
            sl = slice(s * LANES, (s + 1) * LANES)
            main_ref[0, :, sl] = head_norm(acc_scr[:, sl], gain).astype(BF16)

    @pl.when((j == 2) | (j == 6))
    def _():
        main_ref[0] = acc_scr[...].astype(BF16)

    @pl.when(j == 3)
    def _():
        for s in range(n_slab):
            sl = slice(s * LANES, (s + 1) * LANES)
            y = _rotate(acc_scr[:, sl], cosi_ref[...], sini_ref[...], IDX_ROT_DIM // 2, IDX_DIM)
            main_ref[0, :, sl] = y.astype(BF16)

    @pl.when(j == 7)
    def _():
        main_ref[0] = jax.nn.sigmoid(acc_scr[...]).astype(BF16)

    @pl.when(j == n_seg)
    def _():
        acc = jnp.dot(h_scr[...], ws_ref[...], preferred_element_type=F32)
        lane = lax.broadcasted_iota(jnp.int32, acc.shape, 1)
        is_ki = lane < WI_OFF
        cos_k = jnp.where(is_ki, cosi_ref[...], 1.0)
        sin_k = jnp.where(is_ki, sini_ref[...], 0.0)
        ki = _rotate(acc, cos_k, sin_k, IDX_ROT_DIM // 2, IDX_DIM)
        wi = acc * idx_scale
        z = acc + bf_ref[...]
        ls = jnp.minimum(z, 0.0) - jnp.log1p(jnp.exp(-jnp.abs(z)))
        is_fb = (lane >= FB_OFF) & (lane < FB_OFF + n_fb)
        ls = jnp.where(is_fb, ls, 0.0)
        row = lax.broadcasted_iota(jnp.int32, (tm, tm), 0)
        col = lax.broadcasted_iota(jnp.int32, (tm, tm), 1)
        tri = (col <= row).astype(BF16)
        p1 = ls.astype(BF16)
        r1 = ls - p1.astype(F32)
        p2 = r1.astype(BF16)
        p3 = (r1 - p2.astype(F32)).astype(BF16)
        c = (jnp.dot(tri, p1, preferred_element_type=F32)
             + jnp.dot(tri, p2, preferred_element_type=F32)
             + jnp.dot(tri, p3, preferred_element_type=F32))

        @pl.when(i == 0)
        def _():
            carry_scr[...] = jnp.zeros_like(carry_scr)

        c = c + carry_scr[...]
        carry_scr[...] = c[tm - 1:tm, :]
        small_ref[...] = jnp.where(is_ki, ki, jnp.where(lane < FB_OFF, wi, c))


def _in_proj(x, g, w_main, w_small, bf_pad, gains, tabs, *, n_seg, seg, tm):
    s_len, d = x.shape
    cosa, sina, cosi, sini = tabs
    grid = (s_len // tm, n_seg + 1)
    row_spec = lambda w: pl.BlockSpec((tm, w), lambda i, j: (i, 0))
    kern = functools.partial(_in_proj_kernel, n_seg=n_seg,
                             idx_scale=float((N_IDX_HEADS * IDX_DIM) ** -0.5), n_fb=N_HEADS_B)
    return pl.pallas_call(
        kern,
        grid=grid,
        in_specs=[
            pl.BlockSpec((tm, d), lambda i, j: (i, 0), pipeline_mode=pl.Buffered(1)),
            pl.BlockSpec((1, d), lambda i, j: (0, 0)),
            pl.BlockSpec((d, seg), lambda i, j: (0, jnp.minimum(j, n_seg - 1))),
            pl.BlockSpec((d, LANES), lambda i, j: (0, 0)),
            pl.BlockSpec((1, LANES), lambda i, j: (0, 0)),
            pl.BlockSpec((4, LANES), lambda i, j: (0, 0)),
            row_spec(LANES), row_spec(LANES), row_spec(LANES), row_spec(LANES),
        ],
        out_specs=[
            pl.BlockSpec((1, tm, seg), lambda i, j: (jnp.minimum(j, n_seg - 1), i, 0)),
            pl.BlockSpec((tm, LANES), lambda i, j: (i, 0)),
        ],
        out_shape=[
            jax.ShapeDtypeStruct((n_seg, s_len, seg), BF16),
            jax.ShapeDtypeStruct((s_len, LANES), F32),
        ],
        scratch_shapes=[
            pltpu.VMEM((tm, d), BF16),
            pltpu.VMEM((tm, seg), F32),
            pltpu.VMEM((1, LANES), F32),
        ],
        compiler_params=_cparams(("arbitrary", "arbitrary")),
        name="in_proj",
    )(x, g, w_main, w_small, bf_pad, gains, cosa, sina, cosi, sini)


def _dsa_kernel(qa_ref, qi_ref, wi_ref, ki_ref, ka_ref, va_ref, o_ref, key_scr,
                *, tq, tk, top_k, n_heads, n_idx):
    i = pl.program_id(0)
    n_kt = ((i + 1) * tq + tk - 1) // tk
    t_glob = i * tq + lax.broadcasted_iota(jnp.int32, (tq, 1), 0)
    adm_len = (t_glob // CHUNK + 1) * CHUNK

    wi = wi_ref[...]

    def score_tile(kt, carry):
        k0 = pl.multiple_of(kt * tk, tk)
        kt_i = ki_ref[pl.ds(k0, tk), :]
        acc = jnp.zeros((tq, tk), F32)
        for h in range(n_idx):
            qh = qi_ref[0, :, h * IDX_DIM:(h + 1) * IDX_DIM]
            rel = jnp.maximum(_nt_dot(qh, kt_i), 0.0)
            acc = acc + rel * wi[:, h:h + 1]
        bits = lax.bitcast_convert_type(acc, jnp.int32)
        key = jnp.where(bits < 0, bits ^ INT_MAX, bits)
        kpos = k0 + lax.broadcasted_iota(jnp.int32, (1, tk), 1)
        key_scr[kt] = jnp.where(kpos < adm_len, key, KEY_NONE)
        return carry

    lax.fori_loop(0, n_kt, score_tile, 0)

    lane_pos = lax.broadcasted_iota(jnp.int32, (1, LANES), 1)

    def count(pred):
        def body(kt, cnt):
            keys = key_scr[kt]
            for c in range(tk // LANES):
                hit = pred(keys[:, c * LANES:(c + 1) * LANES], kt * tk + c * LANES + lane_pos)
                cnt = cnt + jnp.where(hit, 1, 0)
            return cnt

        cnt = lax.fori_loop(0, n_kt, body, jnp.zeros((tq, LANES), jnp.int32))
        return jnp.sum(cnt, axis=1, keepdims=True)

    def bisect(_, state):
        lo, hi, c_lo, c_hi = state
        mid = (lo >> 1) + (hi >> 1) + (lo & hi & 1)
        mid_b = jnp.broadcast_to(mid, (tq, LANES))
        c_mid = count(lambda k, _: k >= mid_b)
        take = c_mid >= top_k
        return (jnp.where(take, mid, lo), jnp.where(take, hi, mid),
                jnp.where(take, c_mid, c_lo), jnp.where(take, c_hi, c_mid))

    col = lambda v: jnp.full((tq, 1), v, jnp.int32)
    thr, _, c_ge, c_gt = lax.fori_loop(0, 32, bisect,
                                       (col(INT_MIN), col(INT_MAX), n_kt * tk + col(0), col(0)))
    short = thr <= KEY_NONE
    thr = jnp.maximum(thr, KEY_NONE)
    need = top_k - c_gt
    thr_b = jnp.broadcast_to(thr, (tq, LANES))

    def tie_cut():
        def step(_, lohi):
            lo, hi = lohi
            mid = (lo + hi) >> 1
            mid_b = jnp.broadcast_to(mid, (tq, LANES))
            ok = count(lambda k, pos: (k == thr_b) & (pos <= mid_b)) >= need
            return jnp.where(ok, lo, mid), jnp.where(ok, mid, hi)

        n_steps = int(np.ceil(np.log2(key_scr.shape[0] * tk))) + 1
        return lax.fori_loop(0, n_steps, step, (col(-1), n_kt * tk - 1 + col(0)))[1]

    has_tie = jnp.max(jnp.where((c_ge > top_k) & ~short, 1.0, 0.0)) > 0.5
    cut = lax.cond(has_tie, tie_cut, lambda: col(INT_MAX))
    cut = jnp.where(short, -1, cut)

    def bias_tile(kt, carry):
        keys = key_scr[kt]
        kpos = kt * tk + lax.broadcasted_iota(jnp.int32, (1, tk), 1)
        sel = (keys > thr) | ((keys == thr) & (kpos <= cut))
        key_scr[kt] = lax.bitcast_convert_type(jnp.where(sel, 0.0, NEG_BIG).astype(F32), jnp.int32)
        return carry

    lax.fori_loop(0, n_kt, bias_tile, 0)

    for h in range(n_heads):
        hs = slice(h * HEAD_DIM, (h + 1) * HEAD_DIM)
        qh = qa_ref[0, :, hs]

        def attend(kt, mla):
            m, l, acc = mla
            k0 = pl.multiple_of(kt * tk, tk)
            kh = ka_ref[0, pl.ds(k0, tk), hs]
            vh = va_ref[0, pl.ds(k0, tk), hs]
            z = _nt_dot(qh, kh) + lax.bitcast_convert_type(key_scr[kt], F32)
            m_new = jnp.maximum(m, jnp.max(z, axis=1, keepdims=True))
            alpha = jnp.exp(m - m_new)
            p = jnp.exp(z - m_new)
            l = alpha * l + jnp.sum(p, axis=1, keepdims=True)
            acc = alpha * acc + jnp.dot(p.astype(BF16), vh, preferred_element_type=F32)
            return m_new, l, acc

        m0 = jnp.full((tq, 1), NEG_BIG, F32)
        l0 = jnp.zeros((tq, 1), F32)
        a0 = jnp.zeros((tq, HEAD_DIM), F32)
        m, l, acc = lax.fori_loop(0, n_kt, attend, (m0, l0, a0))
        o_ref[:, hs] = (acc / l).astype(BF16)


def _dsa(main, ki, wi, *, tq, tk, top_k):
    n_seg, s_len, seg = main.shape
    n_kt_max = s_len // tk
    kern = functools.partial(_dsa_kernel, tq=tq, tk=tk, top_k=top_k, n_heads=N_HEADS_A,
                             n_idx=N_IDX_HEADS)
    resident = functools.partial(pl.BlockSpec, pipeline_mode=pl.Buffered(1))
    return pl.pallas_call(
        kern,
        grid=(s_len // tq,),
        in_specs=[
            pl.BlockSpec((1, tq, seg), lambda i: (0, i, 0)),
            pl.BlockSpec((1, tq, seg), lambda i: (3, i, 0)),
            pl.BlockSpec((tq, N_IDX_HEADS), lambda i: (i, 0)),
            resident((s_len, IDX_DIM), lambda i: (0, 0)),
            resident((1, s_len, seg), lambda i: (1, 0, 0)),
            resident((1, s_len, seg), lambda i: (2, 0, 0)),
        ],
        out_specs=pl.BlockSpec((tq, seg), lambda i: (i, 0)),
        out_shape=jax.ShapeDtypeStruct((s_len, seg), BF16),
        scratch_shapes=[pltpu.VMEM((n_kt_max, tq, tk), jnp.int32)],
        compiler_params=_cparams(("arbitrary",)),
        name="dsa",
    )(main, main, wi, ki, main, main)


def _fox_kernel(qt_ref, kt_ref, q_ref, k_ref, v_ref, gate_ref, ctq_ref, ctk_ref, o_ref,
                m_scr, l_scr, acc_scr, *, t, n_heads):
    p = pl.program_id(0)
    qi = qt_ref[p]
    ki = kt_ref[p]

    @pl.when(ki == 0)
    def _():
        m_scr[...] = jnp.full_like(m_scr, NEG_BIG)
        l_scr[...] = jnp.zeros_like(l_scr)
        acc_scr[...] = jnp.zeros_like(acc_scr)

    row = lax.broadcasted_iota(jnp.int32, (t, t), 0)
    col = lax.broadcasted_iota(jnp.int32, (t, t), 1)
    visible = (col <= row) | (ki < qi)

    for h in range(n_heads):
        hs = slice(h * HEAD_DIM, (h + 1) * HEAD_DIM)
        bias = ctq_ref[h:h + 1, 0:1] - ctk_ref[h:h + 1, :]
        z = _nt_dot(q_ref[0, :, hs], k_ref[0, :, hs]) + bias
        z = jnp.where(visible, z, NEG_BIG)
        m_old = m_scr[h]
        m_new = jnp.maximum(m_old, jnp.max(z, axis=1, keepdims=True))
        alpha = jnp.exp(m_old - m_new)
        pr = jnp.exp(z - m_new)
        l_scr[h] = alpha * l_scr[h] + jnp.sum(pr, axis=1, keepdims=True)
        acc_scr[:, hs] = alpha * acc_scr[:, hs] + jnp.dot(pr.astype(BF16), v_ref[0, :, hs],
                                                           preferred_element_type=F32)
        m_scr[h] = m_new

    @pl.when(ki == qi)
    def _():
        for h in range(n_heads):
            hs = slice(h * HEAD_DIM, (h + 1) * HEAD_DIM)
            o = acc_scr[:, hs] / l_scr[h]
            o_ref[:, hs] = (o * gate_ref[0, :, hs].astype(F32)).astype(BF16)


def _fox(main, c_t, *, t):
    n_seg, s_len, seg = main.shape
    nq = s_len // t
    pairs = [(a, b) for a in range(nq) for b in range(a + 1)]
    q_tab = jnp.asarray([a for a, _ in pairs], jnp.int32)
    k_tab = jnp.asarray([b for _, b in pairs], jnp.int32)
    n_hb = c_t.shape[0]
    kern = functools.partial(_fox_kernel, t=t, n_heads=N_HEADS_B)
    grid_spec = pltpu.PrefetchScalarGridSpec(
        num_scalar_prefetch=2,
        grid=(len(pairs),),
        in_specs=[
            pl.BlockSpec((1, t, seg), lambda p, qt, kt: (4, qt[p], 0)),
            pl.BlockSpec((1, t, seg), lambda p, qt, kt: (5, kt[p], 0)),
            pl.BlockSpec((1, t, seg), lambda p, qt, kt: (6, kt[p], 0)),
            pl.BlockSpec((1, t, seg), lambda p, qt, kt: (7, qt[p], 0)),
            pl.BlockSpec((n_hb, t), lambda p, qt, kt: (0, qt[p])),
            pl.BlockSpec((n_hb, t), lambda p, qt, kt: (0, kt[p])),
        ],
        out_specs=pl.BlockSpec((t, seg), lambda p, qt, kt: (qt[p], 0)),
        scratch_shapes=[
            pltpu.VMEM((N_HEADS_B, t, 1), F32),
            pltpu.VMEM((N_HEADS_B, t, 1), F32),
            pltpu.VMEM((t, seg), F32),
        ],
    )
    return pl.pallas_call(
        kern,
        grid_spec=grid_spec,
        out_shape=jax.ShapeDtypeStruct((s_len, seg), BF16),
        compiler_params=_cparams(("arbitrary",)),
        name="fox",
    )(q_tab, k_tab, main, main, main, main, c_t, c_t)


def _out_proj_kernel(x_ref, oa_ref, ob_ref, wa_ref, wb_ref, y_ref):
    y = x_ref[...] + jnp.dot(oa_ref[...], wa_ref[...], preferred_element_type=F32)
    y_ref[...] = y + jnp.dot(ob_ref[...], wb_ref[...], preferred_element_type=F32)


def _out_proj(x, o_a, o_b, w_a, w_b, *, tm):
    s_len, d = x.shape
    seg = o_a.shape[1]
    resident = functools.partial(pl.BlockSpec, pipeline_mode=pl.Buffered(1))
    return pl.pallas_call(
        _out_proj_kernel,
        grid=(s_len // tm,),
        in_specs=[
            pl.BlockSpec((tm, d), lambda i: (i, 0)),
            pl.BlockSpec((tm, seg), lambda i: (i, 0)),
            pl.BlockSpec((tm, seg), lambda i: (i, 0)),
            resident((seg, d), lambda i: (0, 0)),
            resident((seg, d), lambda i: (0, 0)),
        ],
        out_specs=pl.BlockSpec((tm, d), lambda i: (i, 0)),
        out_shape=jax.ShapeDtypeStruct((s_len, d), F32),
        compiler_params=_cparams(("arbitrary",)),
        name="out_proj",
    )(x, o_a, o_b, w_a, w_b)


def _swiglu_step(h, wg, wu, wd):
    g = jnp.dot(h, wg.astype(BF16), preferred_element_type=F32)
    u = jnp.dot(h, wu.astype(BF16), preferred_element_type=F32)
    a = (g * jax.nn.sigmoid(g) * u).astype(BF16)
    return jnp.dot(a, wd.astype(BF16), preferred_element_type=F32)


def _ffn_kernel(x_ref, g_ref, wg_ref, wu_ref, wd_ref, y_ref, h_scr):
    @pl.when(pl.program_id(1) == 0)
    def _():
        x = x_ref[...]
        h_scr[...] = _rms(x, g_ref[...]).astype(BF16)
        y_ref[...] = x

    y_ref[...] += _swiglu_step(h_scr[...], wg_ref[...], wu_ref[...], wd_ref[...])


def _ffn(x, g, wg, wu, wd, *, tm, tf):
    s_len, d = x.shape
    f = wg.shape[1]
    return pl.pallas_call(
        _ffn_kernel,
        grid=(s_len // tm, f // tf),
        in_specs=[
            pl.BlockSpec((tm, d), lambda i, j: (i, 0), pipeline_mode=pl.Buffered(1)),
            pl.BlockSpec((1, d), lambda i, j: (0, 0)),
            pl.BlockSpec((d, tf), lambda i, j: (0, j)),
            pl.BlockSpec((d, tf), lambda i, j: (0, j)),
            pl.BlockSpec((tf, d), lambda i, j: (j, 0)),
        ],
        out_specs=pl.BlockSpec((tm, d), lambda i, j: (i, 0)),
        out_shape=jax.ShapeDtypeStruct((s_len, d), F32),
        scratch_shapes=[pltpu.VMEM((tm, d), BF16)],
        compiler_params=_cparams(("arbitrary", "arbitrary")),
        name="ffn",
    )(x, g, wg, wu, wd)


def _pool_kernel(x_ref, g_ref, w_ref, sc_ref, y_ref, ext_scr, *, halo):
    i = pl.program_id(0)
    tm, d = x_ref.shape
    n_grp = len(POOL_WINDOWS)
    width = d // n_grp

    @pl.when(i == 0)
    def _():
        ext_scr[0:halo, :] = jnp.zeros((halo, d), F32)

    x = x_ref[...]
    ext_scr[halo:halo + tm, :] = _rms(x, g_ref[...])
    t_glob = i * tm + lax.broadcasted_iota(jnp.int32, (tm, 1), 0)
    for gi, win in enumerate(POOL_WINDOWS):
        cs = slice(gi * width, (gi + 1) * width)
        hn = ext_scr[halo:halo + tm, cs]
        tot = hn
        for dlt in range(1, win):
            tot = tot + ext_scr[halo - dlt:halo - dlt + tm, cs]
        count = jnp.minimum(t_glob + 1, win).astype(F32)
        pooled = (tot / count - hn).astype(BF16)
        y = jnp.dot(pooled, w_ref[gi], preferred_element_type=F32)
        y_ref[:, cs] = x[:, cs] + y * sc_ref[:, cs]
    ext_scr[0:halo, :] = ext_scr[tm:tm + halo, :]


def _pool(x, g, w_pool, scale, *, tm):
    s_len, d = x.shape
    halo = 16
    assert max(POOL_WINDOWS) <= halo
    n_grp, width, _ = w_pool.shape
    return pl.pallas_call(
        functools.partial(_pool_kernel, halo=halo),
        grid=(s_len // tm,),
        in_specs=[
            pl.BlockSpec((tm, d), lambda i: (i, 0)),
            pl.BlockSpec((1, d), lambda i: (0, 0)),
            pl.BlockSpec((n_grp, width, width), lambda i: (0, 0, 0)),
            pl.BlockSpec((1, d), lambda i: (0, 0)),
        ],
        out_specs=pl.BlockSpec((tm, d), lambda i: (i, 0)),
        out_shape=jax.ShapeDtypeStruct((s_len, d), F32),
        scratch_shapes=[pltpu.VMEM((tm + halo, d), F32)],
        compiler_params=_cparams(("arbitrary",)),
        name="pool",
    )(x, g, w_pool, scale)


R_E1, R_E2, R_G1, R_G2, R_K1, R_K2 = range(6)


def _router_kernel(x_ref, g_ref, r_ref, route_ref, cnt_ref, carry_scr, *, n_exp):
    i = pl.program_id(0)
    tm = x_ref.shape[0]

    @pl.when(i == 0)
    def _():
        carry_scr[...] = jnp.zeros_like(carry_scr)

    h = _rms(x_ref[...], g_ref[...])
    logits = jnp.dot(h, r_ref[...], preferred_element_type=F32, precision=lax.Precision.HIGHEST)
    lane = lax.broadcasted_iota(jnp.int32, logits.shape, 1)
    logits = jnp.where(lane < n_exp, logits, -jnp.inf)
    lane_f = lane.astype(F32)
    v1 = jnp.max(logits, axis=1, keepdims=True)
    e1 = jnp.min(jnp.where(logits == v1, lane_f, float(LANES)), axis=1, keepdims=True)
    rest = jnp.where(lane_f == e1, -jnp.inf, logits)
    v2 = jnp.max(rest, axis=1, keepdims=True)
    e2 = jnp.min(jnp.where(rest == v2, lane_f, float(LANES)), axis=1, keepdims=True)
    ex = jnp.exp(v2 - v1)
    g1 = 1.0 / (1.0 + ex)
    g2 = ex / (1.0 + ex)
    oh1 = (lane_f == e1).astype(F32)
    oh2 = (lane_f == e2).astype(F32)
    both = oh1 + oh2
    row = lax.broadcasted_iota(jnp.int32, (tm, tm), 0)
    col = lax.broadcasted_iota(jnp.int32, (tm, tm), 1)
    tri = (col < row).astype(BF16)
    before = jnp.dot(tri, both.astype(BF16), preferred_element_type=F32) + carry_scr[...]
    k1 = jnp.sum(before * oh1, axis=1, keepdims=True)
    k2 = jnp.sum(before * oh2, axis=1, keepdims=True)
    carry_scr[...] = carry_scr[...] + jnp.sum(both, axis=0, keepdims=True)
    cnt_ref[...] = carry_scr[...]
    vals = (e1, e2, g1, g2, k1, k2)
    out = jnp.zeros(logits.shape, F32)
    for li, v in enumerate(vals):
        out = jnp.where(lane == li, v, out)
    route_ref[...] = out


def _router(x, g, r_pad, *, tm, n_exp):
    s_len, d = x.shape
    return pl.pallas_call(
        functools.partial(_router_kernel, n_exp=n_exp),
        grid=(s_len // tm,),
        in_specs=[
            pl.BlockSpec((tm, d), lambda i: (i, 0)),
            pl.BlockSpec((1, d), lambda i: (0, 0)),
            pl.BlockSpec((d, LANES), lambda i: (0, 0)),
        ],
        out_specs=[
            pl.BlockSpec((tm, LANES), lambda i: (i, 0)),
            pl.BlockSpec((1, LANES), lambda i: (0, 0)),
        ],
        out_shape=[
            jax.ShapeDtypeStruct((s_len, LANES), F32),
            jax.ShapeDtypeStruct((1, LANES), F32),
        ],
        scratch_shapes=[pltpu.VMEM((1, LANES), F32)],
        compiler_params=_cparams(("arbitrary",)),
        name="router",
    )(x, g, r_pad)


def _pack_bf16_pairs(h):
    half = h.shape[1] // 2
    lo = lax.bitcast_convert_type(h[:, :half].astype(BF16).astype(F32), jnp.uint32)
    hi = lax.bitcast_convert_type(h[:, half:].astype(BF16).astype(F32), jnp.uint32)
    return (lo >> 16) | (hi & jnp.uint32(0xFFFF0000))


def _unpack_bf16_pairs(u):
    lo = lax.bitcast_convert_type(u << 16, F32).astype(BF16)
    hi = lax.bitcast_convert_type(u & jnp.uint32(0xFFFF0000), F32).astype(BF16)
    return lo, hi


def _scatter_kernel(p1_ref, p2_ref, x_ref, g_ref, xs_in_ref, xs_ref, row_scr, sem):
    del xs_in_ref
    i = pl.program_id(0)
    tm = x_ref.shape[0]
    row_scr[...] = _pack_bf16_pairs(_rms(x_ref[...], g_ref[...]))

    def copies(t):
        src = row_scr.at[pl.ds(t, 1), :]
        return (pltpu.make_async_copy(src, xs_ref.at[pl.ds(p1_ref[i * tm + t], 1), :], sem.at[0]),
                pltpu.make_async_copy(src, xs_ref.at[pl.ds(p2_ref[i * tm + t], 1), :], sem.at[1]))

    def start(t, c):
        a, b = copies(t)
        a.start()
        b.start()
        return c

    def wait(t, c):
        a, b = copies(t)
        a.wait()
        b.wait()
        return c

    lax.fori_loop(0, tm, start, 0)
    lax.fori_loop(0, tm, wait, 0)


def _scatter(pos1, pos2, x, g, xs_zero, *, tm):
    s_len, d = x.shape
    grid_spec = pltpu.PrefetchScalarGridSpec(
        num_scalar_prefetch=2,
        grid=(s_len // tm,),
        in_specs=[
            pl.BlockSpec((tm, d), lambda i, p1, p2: (i, 0)),
            pl.BlockSpec((1, d), lambda i, p1, p2: (0, 0)),
            pl.BlockSpec(memory_space=pl.ANY),
        ],
        out_specs=pl.BlockSpec(memory_space=pl.ANY),
        scratch_shapes=[
            pltpu.VMEM((tm, d // 2), jnp.uint32),
            pltpu.SemaphoreType.DMA((2,)),
        ],
    )
    return pl.pallas_call(
        _scatter_kernel,
        grid_spec=grid_spec,
        out_shape=jax.ShapeDtypeStruct(xs_zero.shape, jnp.uint32),
        input_output_aliases={4: 0},
        compiler_params=_cparams(("arbitrary",)),
        name="moe_scatter",
    )(pos1, pos2, x, g, xs_zero)


def _experts_kernel(e_ref, v_ref, xs_ref, wg_ref, wu_ref, wd_ref, o_ref, xb_scr):
    j = pl.program_id(0)
    f = pl.program_id(1)
    half = xs_ref.shape[1]

    @pl.when(f == 0)
    def _():
        lo, hi = _unpack_bf16_pairs(xs_ref[...])
        xb_scr[:, :half] = lo
        xb_scr[:, half:] = hi
        o_ref[...] = jnp.zeros_like(o_ref)

    @pl.when(v_ref[j] > 0)
    def _():
        o_ref[...] += _swiglu_step(xb_scr[...], wg_ref[0], wu_ref[0], wd_ref[0])


def _experts(item_e, item_valid, xs, wg, wu, wd, *, r, tf):
    rows, half = xs.shape
    n_exp, d, f = wg.shape
    n_items = rows // r
    n_f = f // tf

    def f_idx(fi, j, v):
        return jnp.where(v[j] > 0, fi, n_f - 1)

    grid_spec = pltpu.PrefetchScalarGridSpec(
        num_scalar_prefetch=2,
        grid=(n_items, n_f),
        in_specs=[
            pl.BlockSpec((r, half), lambda j, fi, e, v: (j, 0)),
            pl.BlockSpec((1, d, tf), lambda j, fi, e, v: (e[j], 0, f_idx(fi, j, v))),
            pl.BlockSpec((1, d, tf), lambda j, fi, e, v: (e[j], 0, f_idx(fi, j, v))),
            pl.BlockSpec((1, tf, d), lambda j, fi, e, v: (e[j], f_idx(fi, j, v), 0)),
        ],
        out_specs=pl.BlockSpec((r, d), lambda j, fi, e, v: (j, 0)),
        scratch_shapes=[pltpu.VMEM((r, d), BF16)],
    )
    return pl.pallas_call(
        _experts_kernel,
        grid_spec=grid_spec,
        out_shape=jax.ShapeDtypeStruct((rows, d), F32),
        compiler_params=_cparams(("arbitrary", "arbitrary")),
        name="moe_experts",
    )(item_e, item_valid, xs, wg, wu, wd)


def _combine_kernel(p1_ref, p2_ref, x_ref, route_ref, o_hbm, y_ref, a_scr, b_scr, sem):
    i = pl.program_id(0)
    tm = x_ref.shape[0]

    def copies(t):
        return (pltpu.make_async_copy(o_hbm.at[pl.ds(p1_ref[i * tm + t], 1), :],
                                      a_scr.at[pl.ds(t, 1), :], sem.at[0]),
                pltpu.make_async_copy(o_hbm.at[pl.ds(p2_ref[i * tm + t], 1), :],
                                      b_scr.at[pl.ds(t, 1), :], sem.at[1]))

    def start(t, c):
        a, b = copies(t)
        a.start()
        b.start()
        return c

    def wait(t, c):
        a, b = copies(t)
        a.wait()
        b.wait()
        return c

    lax.fori_loop(0, tm, start, 0)
    lax.fori_loop(0, tm, wait, 0)
    route = route_ref[...]
    g1 = route[:, R_G1:R_G1 + 1]
    g2 = route[:, R_G2:R_G2 + 1]
    y_ref[...] = x_ref[...] + g1 * a_scr[...] + g2 * b_scr[...]


def _combine(pos1, pos2, x, route, o_sorted, *, tm):
    s_len, d = x.shape
    grid_spec = pltpu.PrefetchScalarGridSpec(
        num_scalar_prefetch=2,
        grid=(s_len // tm,),
        in_specs=[
            pl.BlockSpec((tm, d), lambda i, p1, p2: (i, 0)),
            pl.BlockSpec((tm, LANES), lambda i, p1, p2: (i, 0)),
            pl.BlockSpec(memory_space=pl.ANY),
        ],
        out_specs=pl.BlockSpec((tm, d), lambda i, p1, p2: (i, 0)),
        scratch_shapes=[
            pltpu.VMEM((tm, d), F32),
            pltpu.VMEM((tm, d), F32),
            pltpu.SemaphoreType.DMA((2,)),
        ],
    )
    return pl.pallas_call(
        _combine_kernel,
        grid_spec=grid_spec,
        out_shape=jax.ShapeDtypeStruct((s_len, d), F32),
        compiler_params=_cparams(("arbitrary",)),
        name="moe_combine",
    )(pos1, pos2, x, route, o_sorted)


def _rotary_tables(s_len, rot_dim, period):
    half = rot_dim // 2
    inv_freq = 1.0 / (ROPE_THETA ** (jnp.arange(half, dtype=F32) * 2.0 / rot_dim))
    ang = jnp.arange(s_len, dtype=F32)[:, None] * inv_freq[None, :]
    cos, sin = jnp.cos(ang), jnp.sin(ang)
    pad = period - rot_dim
    cos_p = jnp.concatenate([cos, cos, jnp.ones((s_len, pad), F32)], axis=1)
    sin_p = jnp.concatenate([-sin, sin, jnp.zeros((s_len, pad), F32)], axis=1)
    reps = LANES // period
    return jnp.tile(cos_p, (1, reps)), jnp.tile(sin_p, (1, reps))


def _split_w_in(w_in):
    seg_a = N_HEADS_A * HEAD_DIM
    seg_i = N_IDX_HEADS * IDX_DIM
    seg_b = N_HEADS_B * HEAD_DIM
    sizes = (seg_a, seg_a, seg_a, seg_i, IDX_DIM, N_IDX_HEADS, seg_b, seg_b, seg_b, N_HEADS_B, seg_b)
    offs = np.cumsum(sizes)[:-1].tolist()
    return jnp.split(w_in, offs, axis=1)


def _moe_tables(counts, e1, e2, k1, k2, *, r, n_items):
    n_exp = counts.shape[0]
    blocks = (counts + r - 1) // r
    bend = jnp.cumsum(blocks)
    bstart = bend - blocks
    j = jnp.arange(n_items, dtype=jnp.int32)
    item_e = jnp.sum((j[:, None] >= bend[None, :]).astype(jnp.int32), axis=1)
    valid = item_e < n_exp
    last_e = jnp.minimum(item_e[jnp.maximum(bend[-1] - 1, 0)], n_exp - 1)
    item_e = jnp.where(valid, item_e, last_e).astype(jnp.int32)
    pos1 = bstart[e1] * r + k1
    pos2 = bstart[e2] * r + k2
    return item_e, valid.astype(jnp.int32), pos1.astype(jnp.int32), pos2.astype(jnp.int32)


def _moe_block_rows(n_assign, n_exp):
    r = -(-n_assign // (2 * n_exp))
    r = -(-(r + r // 16) // 64) * 64
    return max(r, 64)


def kernel(x, attn_norm_e, w_in_e, b_forget_e, q_norm_a_e, k_norm_a_e, q_norm_b_e, k_norm_b_e,
           w_out_e, ffn_norm_e, w_gate_e, w_up_e, w_down_e, pool_norm_o, w_pool_o, pool_scale_o,
           moe_norm_o, router_o, w_gate_o, w_up_o, w_down_o):
    b, s_len, d = x.shape
    assert b == 1
    x0 = x[0]
    seg = N_HEADS_A * HEAD_DIM
    assert seg == N_IDX_HEADS * IDX_DIM == N_HEADS_B * HEAD_DIM
    assert N_IDX_HEADS <= FB_OFF - WI_OFF and N_HEADS_B <= LANES - FB_OFF
    row = lambda v: v.reshape(1, -1)

    qa, ka, va, qi, ki, wi, qb, kb, vb, fb, gb = _split_w_in(w_in_e[0])
    w_main = jnp.concatenate([qa, ka, va, qi, qb, kb, vb, gb], axis=1).astype(BF16)
    w_small = jnp.concatenate(
        [ki, wi, jnp.zeros((d, FB_OFF - WI_OFF - N_IDX_HEADS), F32), fb,
         jnp.zeros((d, LANES - FB_OFF - N_HEADS_B), F32)], axis=1).astype(BF16)
    bf_pad = jnp.zeros((1, LANES), F32).at[0, FB_OFF:FB_OFF + N_HEADS_B].set(b_forget_e[0])
    scale = HEAD_DIM ** -0.5
    gains = jnp.stack([q_norm_a_e[0] * scale, k_norm_a_e[0], q_norm_b_e[0] * scale, k_norm_b_e[0]])
    tabs = _rotary_tables(s_len, ROT_DIM, HEAD_DIM) + _rotary_tables(s_len, IDX_ROT_DIM, IDX_DIM)
    main, small = _in_proj(x0, row(attn_norm_e[0]), w_main, w_small, bf_pad, gains, tabs,
                           n_seg=8, seg=seg, tm=_tile(s_len, 1024))
    ki_rot = small[:, KI_OFF:KI_OFF + IDX_DIM].astype(BF16)
    wi_s = small[:, WI_OFF:WI_OFF + N_IDX_HEADS]
    c_t = small[:, FB_OFF:FB_OFF + N_HEADS_B].T

    top_k = min(TOPK_MAX, s_len // 4)
    o_a = _dsa(main, ki_rot, wi_s, tq=_tile(s_len, 256), tk=_tile(s_len, 512), top_k=top_k)
    o_b = _fox(main, c_t, t=_tile(s_len, 512))
    w_out = w_out_e[0].astype(BF16)
    x1 = _out_proj(x0, o_a, o_b, w_out[:seg], w_out[seg:], tm=_tile(s_len, 512))

    x2 = _ffn(x1, row(ffn_norm_e[0]), w_gate_e[0], w_up_e[0], w_down_e[0],
              tm=_tile(s_len, 1024), tf=_tile(w_gate_e.shape[2], 256))

    x3 = _pool(x2, row(pool_norm_o[0]), w_pool_o[0].astype(BF16), row(pool_scale_o[0]),
               tm=_tile(s_len, 512))

    n_exp = router_o.shape[2]
    r_pad = jnp.zeros((d, LANES), F32).at[:, :n_exp].set(router_o[0])
    route, counts = _router(x3, row(moe_norm_o[0]), r_pad, tm=_tile(s_len, 512), n_exp=n_exp)
    r_rows = _moe_block_rows(2 * s_len, n_exp)
    n_items = (2 * s_len) // r_rows + n_exp
    as_int = lambda c: route[:, c].astype(jnp.int32)
    item_e, item_valid, pos1, pos2 = _moe_tables(
        counts[0, :n_exp].astype(jnp.int32), as_int(R_E1), as_int(R_E2), as_int(R_K1), as_int(R_K2),
        r=r_rows, n_items=n_items)
    xs_zero = jnp.zeros((n_items * r_rows, d // 2), jnp.uint32)
    xs = _scatter(pos1, pos2, x3, row(moe_norm_o[0]), xs_zero, tm=_tile(s_len, 256))
    o_sorted = _experts(item_e, item_valid, xs, w_gate_o[0], w_up_o[0], w_down_o[0],
                        r=r_rows, tf=_tile(w_gate_o.shape[3], 256))
    y = _combine(pos1, pos2, x3, route, o_sorted, tm=_tile(s_len, 256))
    return y[None]
```

```python
import functools

import jax
import jax.numpy as jnp
import numpy as np
from jax import lax
from jax.experimental import pallas as pl
from jax.experimental.pallas import tpu as pltpu

HEAD_DIM = 128
N_HEADS_A = 8
N_HEADS_B = 8
N_IDX_HEADS = 16
IDX_DIM = 64
ROT_DIM = HEAD_DIM // 4
IDX_ROT_DIM = IDX_DIM // 4
ROPE_THETA = 500000.0
CHUNK = 64
TOPK_MAX = 256
POOL_WINDOWS = (2, 4, 8, 16)
RMS_EPS = 1e-6

LANES = 128
VMEM_LIMIT_BYTES = 56 * 1024 * 1024

KI_OFF = 0
WI_OFF = IDX_DIM
FB_OFF = 80

NEG_BIG = -1e30
INT_MIN = -(2 ** 31)
INT_MAX = 2 ** 31 - 1
KEY_NONE = INT_MIN + 1

F32 = jnp.float32
BF16 = jnp.bfloat16


def _cparams(sem):
    return pltpu.CompilerParams(dimension_semantics=sem, vmem_limit_bytes=VMEM_LIMIT_BYTES)


def _tile(n, pref):
    t = min(n, pref)
    while n % t:
        t //= 2
    return t


def _rms(x, g):
    ms = jnp.mean(x * x, axis=-1, keepdims=True)
    return x * lax.rsqrt(ms + RMS_EPS) * g


def _nt_dot(a, b):
    return lax.dot_general(a, b, (((1,), (1,)), ((), ())), preferred_element_type=F32)


def _rotate(y, cos_t, sin_t, half, period):
    lane = lax.broadcasted_iota(jnp.int32, y.shape, 1)
    up = pltpu.roll(y, LANES - half, 1)
    dn = pltpu.roll(y, half, 1)
    rot = jnp.where((lane & (period - 1)) < half, up, dn)
    return y * cos_t + rot * sin_t


def _in_proj_kernel(x_ref, g_ref, wm_ref, ws_ref, bf_ref, gains_ref, cosa_ref, sina_ref,
                    cosi_ref, sini_ref, main_ref, small_ref, vt_ref, cneg_ref, h_scr, acc_scr,
                    carry_scr, *, n_seg, idx_scale, n_fb):
    i = pl.program_id(0)
    j = pl.program_id(1)
    tm = h_scr.shape[0]
    n_slab = acc_scr.shape[1] // LANES

    @pl.when(j == 0)
    def _():
        h_scr[...] = _rms(x_ref[...], g_ref[...]).astype(BF16)

    @pl.when(j < n_seg)
    def _():
        acc_scr[...] = jnp.dot(h_scr[...], wm_ref[...], preferred_element_type=F32)

    def head_norm(slab, gain):
        ms = jnp.mean(slab * slab, axis=-1, keepdims=True)
        return slab * lax.rsqrt(ms + RMS_EPS) * gain

    @pl.when(j < 2)
    def _():
        gain = gains_ref[pl.ds(j, 1), :]
        for s in range(n_slab):
            sl = slice(s * LANES, (s + 1) * LANES)
            y = head_norm(acc_scr[:, sl], gain)
            y = _rotate(y, cosa_ref[...], sina_ref[...], ROT_DIM // 2, LANES)
            main_ref[0, :, sl] = y.astype(BF16)

    @pl.when((j == 4) | (j == 5))
    def _():
        gain = gains_ref[pl.ds(j - 2, 1), :]
        for s in range(n_slab):
            sl = slice(s * LANES, (s + 1) * LANES)
            main_ref[0, :, sl] = head_norm(acc_scr[:, sl], gain).astype(BF16)

    @pl.when((j == 2) | (j == 6))
    def _():
        v = acc_scr[...]
        main_ref[0] = v.astype(BF16)
        vt_ref[0, 0] = v.T.astype(BF16)

    @pl.when(j == 3)
    def _():
        for s in range(n_slab):
            sl = slice(s * LANES, (s + 1) * LANES)
            y = _rotate(acc_scr[:, sl], cosi_ref[...], sini_ref[...], IDX_ROT_DIM // 2, IDX_DIM)
            main_ref[0, :, sl] = y.astype(BF16)

    @pl.when(j == 7)
    def _():
        main_ref[0] = jax.nn.sigmoid(acc_scr[...]).astype(BF16)

    @pl.when(j == n_seg)
    def _():
        acc = jnp.dot(h_scr[...], ws_ref[...], preferred_element_type=F32)
        lane = lax.broadcasted_iota(jnp.int32, acc.shape, 1)
        is_ki = lane < WI_OFF
        cos_k = jnp.where(is_ki, cosi_ref[...], 1.0)
        sin_k = jnp.where(is_ki, sini_ref[...], 0.0)
        ki = _rotate(acc, cos_k, sin_k, IDX_ROT_DIM // 2, IDX_DIM)
        wi = acc * idx_scale
        z = acc + bf_ref[...]
        ls = jnp.minimum(z, 0.0) - jnp.log1p(jnp.exp(-jnp.abs(z)))
        is_fb = (lane >= FB_OFF) & (lane < FB_OFF + n_fb)
        ls = jnp.where(is_fb, ls, 0.0)
        row = lax.broadcasted_iota(jnp.int32, (tm, tm), 0)
        col = lax.broadcasted_iota(jnp.int32, (tm, tm), 1)
        tri = (col <= row).astype(BF16)
        p1 = ls.astype(BF16)
        r1 = ls - p1.astype(F32)
        p2 = r1.astype(BF16)
        p3 = (r1 - p2.astype(F32)).astype(BF16)
        c = (jnp.dot(tri, p1, preferred_element_type=F32)
             + jnp.dot(tri, p2, preferred_element_type=F32)
             + jnp.dot(tri, p3, preferred_element_type=F32))

        @pl.when(i == 0)
        def _():
            carry_scr[...] = jnp.zeros_like(carry_scr)

        c = c + carry_scr[...]
        carry_scr[...] = c[tm - 1:tm, :]
        small_ref[...] = jnp.where(is_ki, ki, jnp.where(lane < FB_OFF, wi, c))
        for hb in range(n_fb):
            cneg_ref[hb] = jnp.broadcast_to(-c[:, FB_OFF + hb:FB_OFF + hb + 1], (tm, LANES))


def _in_proj(x, g, w_main, w_small, bf_pad, gains, tabs, *, n_seg, seg, tm):
    s_len, d = x.shape
    cosa, sina, cosi, sini = tabs
    grid = (s_len // tm, n_seg + 1)
    row_spec = lambda w: pl.BlockSpec((tm, w), lambda i, j: (i, 0))
    kern = functools.partial(_in_proj_kernel, n_seg=n_seg,
                             idx_scale=float((N_IDX_HEADS * IDX_DIM) ** -0.5), n_fb=N_HEADS_B)
    return pl.pallas_call(
        kern,
        grid=grid,
        in_specs=[
            pl.BlockSpec((tm, d), lambda i, j: (i, 0), pipeline_mode=pl.Buffered(1)),
            pl.BlockSpec((1, d), lambda i, j: (0, 0)),
            pl.BlockSpec((d, seg), lambda i, j: (0, jnp.minimum(j, n_seg - 1))),
            pl.BlockSpec((d, LANES), lambda i, j: (0, 0)),
            pl.BlockSpec((1, LANES), lambda i, j: (0, 0)),
            pl.BlockSpec((4, LANES), lambda i, j: (0, 0)),
            row_spec(LANES), row_spec(LANES), row_spec(LANES), row_spec(LANES),
        ],
        out_specs=[
            pl.BlockSpec((1, tm, seg), lambda i, j: (jnp.minimum(j, n_seg - 1), i, 0)),
            pl.BlockSpec((tm, LANES), lambda i, j: (i, 0)),
            pl.BlockSpec((1, 1, seg, tm), lambda i, j: (jnp.where(j <= 2, 0, 1), i, 0, 0)),
            pl.BlockSpec((N_HEADS_B, tm, LANES), lambda i, j: (0, i, 0)),
        ],
        out_shape=[
            jax.ShapeDtypeStruct((n_seg, s_len, seg), BF16),
            jax.ShapeDtypeStruct((s_len, LANES), F32),
            jax.ShapeDtypeStruct((2, s_len // tm, seg, tm), BF16),
            jax.ShapeDtypeStruct((N_HEADS_B, s_len, LANES), F32),
        ],
        scratch_shapes=[
            pltpu.VMEM((tm, d), BF16),
            pltpu.VMEM((tm, seg), F32),
            pltpu.VMEM((1, LANES), F32),
        ],
        compiler_params=_cparams(("arbitrary", "arbitrary")),
        name="in_proj",
    )(x, g, w_main, w_small, bf_pad, gains, cosa, sina, cosi, sini)


def _dsa_kernel(qa_ref, qi_ref, wi_ref, ki_ref, ka_ref, va_ref, o_ref, key_scr,
                *, tq, tk, top_k, n_heads, n_idx):
    i = pl.program_id(0)
    n_kt = ((i + 1) * tq + tk - 1) // tk
    t_glob = i * tq + lax.broadcasted_iota(jnp.int32, (tq, 1), 0)
    adm_len = (t_glob // CHUNK + 1) * CHUNK

    wi = wi_ref[...]

    def score_tile(kt, carry):
        k0 = pl.multiple_of(kt * tk, tk)
        kt_i = ki_ref[pl.ds(k0, tk), :]
        acc = jnp.zeros((tq, tk), F32)
        for h in range(n_idx):
            qh = qi_ref[0, :, h * IDX_DIM:(h + 1) * IDX_DIM]
            rel = jnp.maximum(_nt_dot(qh, kt_i), 0.0)
            acc = acc + rel * wi[:, h:h + 1]
        bits = lax.bitcast_convert_type(acc, jnp.int32)
        key = jnp.where(bits < 0, bits ^ INT_MAX, bits)
        kpos = k0 + lax.broadcasted_iota(jnp.int32, (1, tk), 1)
        key_scr[kt] = jnp.where(kpos < adm_len, key, KEY_NONE)
        return carry

    lax.fori_loop(0, n_kt, score_tile, 0)

    lane_pos = lax.broadcasted_iota(jnp.int32, (1, LANES), 1)

    def count(pred):
        def body(kt, cnt):
            keys = key_scr[kt]
            for c in range(tk // LANES):
                hit = pred(keys[:, c * LANES:(c + 1) * LANES], kt * tk + c * LANES + lane_pos)
                cnt = cnt + jnp.where(hit, 1, 0)
            return cnt

        cnt = lax.fori_loop(0, n_kt, body, jnp.zeros((tq, LANES), jnp.int32))
        return jnp.sum(cnt, axis=1, keepdims=True)

    def bisect(_, state):
        lo, hi, c_lo, c_hi = state
        mid = (lo >> 1) + (hi >> 1) + (lo & hi & 1)
        mid_b = jnp.broadcast_to(mid, (tq, LANES))
        c_mid = count(lambda k, _: k >= mid_b)
        take = c_mid >= top_k
        return (jnp.where(take, mid, lo), jnp.where(take, hi, mid),
                jnp.where(take, c_mid, c_lo), jnp.where(take, c_hi, c_mid))

    col = lambda v: jnp.full((tq, 1), v, jnp.int32)
    thr, _, c_ge, c_gt = lax.fori_loop(0, 32, bisect,
                                       (col(INT_MIN), col(INT_MAX), n_kt * tk + col(0), col(0)))
    short = thr <= KEY_NONE
    thr = jnp.maximum(thr, KEY_NONE)
    need = top_k - c_gt
    thr_b = jnp.broadcast_to(thr, (tq, LANES))

    def tie_cut():
        def step(_, lohi):
            lo, hi = lohi
            mid = (lo + hi) >> 1
            mid_b = jnp.broadcast_to(mid, (tq, LANES))
            ok = count(lambda k, pos: (k == thr_b) & (pos <= mid_b)) >= need
            return jnp.where(ok, lo, mid), jnp.where(ok, mid, hi)

        n_steps = int(np.ceil(np.log2(key_scr.shape[0] * tk))) + 1
        return lax.fori_loop(0, n_steps, step, (col(-1), n_kt * tk - 1 + col(0)))[1]

    has_tie = jnp.max(jnp.where((c_ge > top_k) & ~short, 1.0, 0.0)) > 0.5
    cut = lax.cond(has_tie, tie_cut, lambda: col(INT_MAX))
    cut = jnp.where(short, -1, cut)

    def bias_tile(kt, carry):
        keys = key_scr[kt]
        kpos = kt * tk + lax.broadcasted_iota(jnp.int32, (1, tk), 1)
        sel = (keys > thr) | ((keys == thr) & (kpos <= cut))
        key_scr[kt] = lax.bitcast_convert_type(jnp.where(sel, 0.0, NEG_BIG).astype(F32), jnp.int32)
        return carry

    lax.fori_loop(0, n_kt, bias_tile, 0)

    for h in range(n_heads):
        hs = slice(h * HEAD_DIM, (h + 1) * HEAD_DIM)
        qh = qa_ref[0, :, hs]

        def attend(kt, mla):
            m, l, acc = mla
            k0 = pl.multiple_of(kt * tk, tk)
            kh = ka_ref[0, pl.ds(k0, tk), hs]
            vh = va_ref[0, pl.ds(k0, tk), hs]
            z = _nt_dot(qh, kh) + lax.bitcast_convert_type(key_scr[kt], F32)
            m_new = jnp.maximum(m, jnp.max(z, axis=1, keepdims=True))
            alpha = jnp.exp(m - m_new)
            p = jnp.exp(z - m_new)
            l = alpha * l + jnp.sum(p, axis=1, keepdims=True)
            acc = alpha * acc + jnp.dot(p.astype(BF16), vh, preferred_element_type=F32)
            return m_new, l, acc

        m0 = jnp.full((tq, 1), NEG_BIG, F32)
        l0 = jnp.zeros((tq, 1), F32)
        a0 = jnp.zeros((tq, HEAD_DIM), F32)
        m, l, acc = lax.fori_loop(0, n_kt, attend, (m0, l0, a0))
        o_ref[:, hs] = (acc / l).astype(BF16)


def _dsa(main, ki, wi, *, tq, tk, top_k):
    n_seg, s_len, seg = main.shape
    n_kt_max = s_len // tk
    kern = functools.partial(_dsa_kernel, tq=tq, tk=tk, top_k=top_k, n_heads=N_HEADS_A,
                             n_idx=N_IDX_HEADS)
    resident = functools.partial(pl.BlockSpec, pipeline_mode=pl.Buffered(1))
    return pl.pallas_call(
        kern,
        grid=(s_len // tq,),
        in_specs=[
            pl.BlockSpec((1, tq, seg), lambda i: (0, i, 0)),
            pl.BlockSpec((1, tq, seg), lambda i: (3, i, 0)),
            pl.BlockSpec((tq, N_IDX_HEADS), lambda i: (i, 0)),
            resident((s_len, IDX_DIM), lambda i: (0, 0)),
            resident((1, s_len, seg), lambda i: (1, 0, 0)),
            resident((1, s_len, seg), lambda i: (2, 0, 0)),
        ],
        out_specs=pl.BlockSpec((tq, seg), lambda i: (i, 0)),
        out_shape=jax.ShapeDtypeStruct((s_len, seg), BF16),
        scratch_shapes=[pltpu.VMEM((n_kt_max, tq, tk), jnp.int32)],
        compiler_params=_cparams(("arbitrary",)),
        name="dsa",
    )(main, main, wi, ki, main, main)


def _fox_kernel(qt_ref, kt_ref, q_ref, k_ref, v_ref, gate_ref, ctq_ref, ctk_ref, o_ref,
                m_scr, l_scr, acc_scr, *, t, n_heads):
    p = pl.program_id(0)
    qi = qt_ref[p]
    ki = kt_ref[p]

    @pl.when(ki == 0)
    def _():
        m_scr[...] = jnp.full_like(m_scr, NEG_BIG)
        l_scr[...] = jnp.zeros_like(l_scr)
        acc_scr[...] = jnp.zeros_like(acc_scr)

    row = lax.broadcasted_iota(jnp.int32, (t, t), 0)
    col = lax.broadcasted_iota(jnp.int32, (t, t), 1)
    visible = (col <= row) | (ki < qi)

    for h in range(n_heads):
        hs = slice(h * HEAD_DIM, (h + 1) * HEAD_DIM)
        bias = ctq_ref[h:h + 1, 0:1] - ctk_ref[h:h + 1, :]
        z = _nt_dot(q_ref[0, :, hs], k_ref[0, :, hs]) + bias
        z = jnp.where(visible, z, NEG_BIG)
        m_old = m_scr[h]
        m_new = jnp.maximum(m_old, jnp.max(z, axis=1, keepdims=True))
        alpha = jnp.exp(m_old - m_new)
        pr = jnp.exp(z - m_new)
        l_scr[h] = alpha * l_scr[h] + jnp.sum(pr, axis=1, keepdims=True)
        acc_scr[:, hs] = alpha * acc_scr[:, hs] + jnp.dot(pr.astype(BF16), v_ref[0, :, hs],
                                                           preferred_element_type=F32)
        m_scr[h] = m_new

    @pl.when(ki == qi)
    def _():
        for h in range(n_heads):
            hs = slice(h * HEAD_DIM, (h + 1) * HEAD_DIM)
            o = acc_scr[:, hs] / l_scr[h]
            o_ref[:, hs] = (o * gate_ref[0, :, hs].astype(F32)).astype(BF16)


def _fox(main, c_t, *, t):
    n_seg, s_len, seg = main.shape
    nq = s_len // t
    pairs = [(a, b) for a in range(nq) for b in range(a + 1)]
    q_tab = jnp.asarray([a for a, _ in pairs], jnp.int32)
    k_tab = jnp.asarray([b for _, b in pairs], jnp.int32)
    n_hb = c_t.shape[0]
    kern = functools.partial(_fox_kernel, t=t, n_heads=N_HEADS_B)
    grid_spec = pltpu.PrefetchScalarGridSpec(
        num_scalar_prefetch=2,
        grid=(len(pairs),),
        in_specs=[
            pl.BlockSpec((1, t, seg), lambda p, qt, kt: (4, qt[p], 0)),
            pl.BlockSpec((1, t, seg), lambda p, qt, kt: (5, kt[p], 0)),
            pl.BlockSpec((1, t, seg), lambda p, qt, kt: (6, kt[p], 0)),
            pl.BlockSpec((1, t, seg), lambda p, qt, kt: (7, qt[p], 0)),
            pl.BlockSpec((n_hb, t), lambda p, qt, kt: (0, qt[p])),
            pl.BlockSpec((n_hb, t), lambda p, qt, kt: (0, kt[p])),
        ],
        out_specs=pl.BlockSpec((t, seg), lambda p, qt, kt: (qt[p], 0)),
        scratch_shapes=[
            pltpu.VMEM((N_HEADS_B, t, 1), F32),
            pltpu.VMEM((N_HEADS_B, t, 1), F32),
            pltpu.VMEM((t, seg), F32),
        ],
    )
    return pl.pallas_call(
        kern,
        grid_spec=grid_spec,
        out_shape=jax.ShapeDtypeStruct((s_len, seg), BF16),
        compiler_params=_cparams(("arbitrary",)),
        name="fox",
    )(q_tab, k_tab, main, main, main, main, c_t, c_t)


def _transpose_heads(src, dst_scr):
    for s in range(src.shape[1] // LANES):
        sl = slice(s * LANES, (s + 1) * LANES)
        dst_scr[sl, :] = src[:, sl].astype(F32).T.astype(BF16)


def _logits_pass(h, z_t, z_scr, m_scr, mnew_scr):
    z_scr[h] = z_t
    mnew_scr[h:h + 1, :] = jnp.maximum(m_scr[h:h + 1, :], jnp.max(z_t, axis=0, keepdims=True))


def _softmax_pass(h, v_t, z_scr, m_scr, mnew_scr, l_scr, acc_scr):
    hs = slice(h * HEAD_DIM, (h + 1) * HEAD_DIM)
    m_new = mnew_scr[h:h + 1, :]
    alpha = jnp.exp(m_scr[h:h + 1, :] - m_new)
    p_t = jnp.exp(z_scr[h] - m_new)
    l_scr[h:h + 1, :] = alpha * l_scr[h:h + 1, :] + jnp.sum(p_t, axis=0, keepdims=True)
    acc_scr[hs, :] = alpha * acc_scr[hs, :] + jnp.dot(v_t, p_t.astype(BF16),
                                                      preferred_element_type=F32)
    m_scr[h:h + 1, :] = m_new


def _init_attn_state(m_scr, l_scr, acc_scr):
    m_scr[...] = jnp.full_like(m_scr, NEG_BIG)
    l_scr[...] = jnp.zeros_like(l_scr)
    acc_scr[...] = jnp.zeros_like(acc_scr)


def _dsa_t_kernel(qa_ref, qi_ref, wit_ref, ki_ref, ka_ref, vat_ref, o_ref,
                  key_scr, qat_scr, qit_scr, z_scr, m_scr, mnew_scr, l_scr, acc_scr,
                  *, tq, tk, top_k, n_heads, n_idx):
    i = pl.program_id(0)
    n_kt = ((i + 1) * tq + tk - 1) // tk
    t_glob = i * tq + lax.broadcasted_iota(jnp.int32, (1, tq), 1)
    adm_len = (t_glob // CHUNK + 1) * CHUNK
    key_pos = lax.broadcasted_iota(jnp.int32, (tk, 1), 0)

    _transpose_heads(qa_ref[0], qat_scr)
    _transpose_heads(qi_ref[0], qit_scr)

    def score_tile(kt, carry):
        k0 = pl.multiple_of(kt * tk, tk)
        k_idx = ki_ref[pl.ds(k0, tk), :]
        acc = jnp.zeros((tk, tq), F32)
        for h in range(n_idx):
            q_t = qit_scr[h * IDX_DIM:(h + 1) * IDX_DIM, :]
            rel = jnp.maximum(jnp.dot(k_idx, q_t, preferred_element_type=F32), 0.0)
            acc = acc + rel * wit_ref[h:h + 1, :]
        bits = lax.bitcast_convert_type(acc, jnp.int32)
        key = jnp.where(bits < 0, bits ^ INT_MAX, bits)
        key_scr[kt] = jnp.where(k0 + key_pos < adm_len, key, KEY_NONE)
        return carry

    lax.fori_loop(0, n_kt, score_tile, 0)

    def count(pred):
        def body(kt, cnt):
            hit = jnp.where(pred(key_scr[kt], kt * tk + key_pos), 1, 0)
            return cnt + jnp.sum(hit.reshape(tk // 8, 8, tq), axis=0)

        cnt = lax.fori_loop(0, n_kt, body, jnp.zeros((8, tq), jnp.int32))
        return jnp.sum(cnt, axis=0, keepdims=True)

    def bisect(_, state):
        lo, hi, c_lo, c_hi = state
        mid = (lo >> 1) + (hi >> 1) + (lo & hi & 1)
        c_mid = count(lambda k, _: k >= mid)
        take = c_mid >= top_k
        return (jnp.where(take, mid, lo), jnp.where(take, hi, mid),
                jnp.where(take, c_mid, c_lo), jnp.where(take, c_hi, c_mid))

    row = lambda v: jnp.full((1, tq), v, jnp.int32)
    thr, _, c_ge, c_gt = lax.fori_loop(0, 32, bisect,
                                       (row(INT_MIN), row(INT_MAX), n_kt * tk + row(0), row(0)))
    short = thr <= KEY_NONE
    thr = jnp.maximum(thr, KEY_NONE)
    need = top_k - c_gt

    def tie_cut():
        def step(_, lohi):
            lo, hi = lohi
            mid = (lo + hi) >> 1
            ok = count(lambda k, pos: (k == thr) & (pos <= mid)) >= need
            return jnp.where(ok, lo, mid), jnp.where(ok, mid, hi)

        n_steps = int(np.ceil(np.log2(key_scr.shape[0] * tk))) + 1
        return lax.fori_loop(0, n_steps, step, (row(-1), n_kt * tk - 1 + row(0)))[1]

    has_tie = jnp.max(jnp.where((c_ge > top_k) & ~short, 1.0, 0.0)) > 0.5
    cut = lax.cond(has_tie, tie_cut, lambda: row(INT_MAX))
    cut = jnp.where(short, -1, cut)

    def bias_tile(kt, carry):
        keys = key_scr[kt]
        sel = (keys > thr) | ((keys == thr) & (kt * tk + key_pos <= cut))
        key_scr[kt] = lax.bitcast_convert_type(jnp.where(sel, 0.0, NEG_BIG).astype(F32), jnp.int32)
        return carry

    lax.fori_loop(0, n_kt, bias_tile, 0)

    _init_attn_state(m_scr, l_scr, acc_scr)

    def attend(kt, carry):
        k0 = pl.multiple_of(kt * tk, tk)
        bias = lax.bitcast_convert_type(key_scr[kt], F32)
        for h in range(n_heads):
            hs = slice(h * HEAD_DIM, (h + 1) * HEAD_DIM)
            z_t = jnp.dot(ka_ref[0, pl.ds(k0, tk), hs], qat_scr[hs, :],
                          preferred_element_type=F32) + bias
            _logits_pass(h, z_t, z_scr, m_scr, mnew_scr)
        for h in range(n_heads):
            hs = slice(h * HEAD_DIM, (h + 1) * HEAD_DIM)
            _softmax_pass(h, vat_ref[0, kt, hs, :], z_scr, m_scr, mnew_scr, l_scr, acc_scr)
        return carry

    lax.fori_loop(0, n_kt, attend, 0)
    for h in range(n_heads):
        hs = slice(h * HEAD_DIM, (h + 1) * HEAD_DIM)
        o_ref[:, hs] = (acc_scr[hs, :] / l_scr[h:h + 1, :]).T.astype(BF16)


def _dsa_t(main, vt, ki, wi_t, *, tq, tk, top_k):
    n_seg, s_len, seg = main.shape
    assert vt.shape == (2, s_len // tk, seg, tk)
    kern = functools.partial(_dsa_t_kernel, tq=tq, tk=tk, top_k=top_k, n_heads=N_HEADS_A,
                             n_idx=N_IDX_HEADS)
    resident = functools.partial(pl.BlockSpec, pipeline_mode=pl.Buffered(1))
    return pl.pallas_call(
        kern,
        grid=(s_len // tq,),
        in_specs=[
            pl.BlockSpec((1, tq, seg), lambda i: (0, i, 0)),
            pl.BlockSpec((1, tq, seg), lambda i: (3, i, 0)),
            pl.BlockSpec((N_IDX_HEADS, tq), lambda i: (0, i)),
            resident((s_len, IDX_DIM), lambda i: (0, 0)),
            resident((1, s_len, seg), lambda i: (1, 0, 0)),
            resident((1, s_len // tk, seg, tk), lambda i: (0, 0, 0, 0)),
        ],
        out_specs=pl.BlockSpec((tq, seg), lambda i: (i, 0)),
        out_shape=jax.ShapeDtypeStruct((s_len, seg), BF16),
        scratch_shapes=[
            pltpu.VMEM((s_len // tk, tk, tq), jnp.int32),
            pltpu.VMEM((seg, tq), BF16),
            pltpu.VMEM((seg, tq), BF16),
            pltpu.VMEM((N_HEADS_A, tk, tq), F32),
            pltpu.VMEM((N_HEADS_A, tq), F32),
            pltpu.VMEM((N_HEADS_A, tq), F32),
            pltpu.VMEM((N_HEADS_A, tq), F32),
            pltpu.VMEM((seg, tq), F32),
        ],
        compiler_params=_cparams(("arbitrary",)),
        name="dsa",
    )(main, main, wi_t, ki, main, vt)


def _fox_t_kernel(qt_ref, kt_ref, q_ref, k_ref, vt_ref, gate_ref, cneg_ref, o_ref,
                  qt_scr, z_scr, m_scr, mnew_scr, l_scr, acc_scr, *, t, n_heads):
    p = pl.program_id(0)
    qi = qt_ref[p]
    ki = kt_ref[p]

    @pl.when(ki == 0)
    def _():
        _transpose_heads(q_ref[0], qt_scr)
        _init_attn_state(m_scr, l_scr, acc_scr)

    def update(causal):
        key_i = lax.broadcasted_iota(jnp.int32, (t, LANES), 0)
        qry_i = lax.broadcasted_iota(jnp.int32, (t, LANES), 1)
        for h in range(n_heads):
            hs = slice(h * HEAD_DIM, (h + 1) * HEAD_DIM)
            z_t = jnp.dot(k_ref[0, :, hs], qt_scr[hs, :], preferred_element_type=F32)
            cols = []
            for c in range(t // LANES):
                zc = z_t[:, c * LANES:(c + 1) * LANES] + cneg_ref[h]
                if causal:
                    zc = jnp.where(key_i <= qry_i + c * LANES, zc, NEG_BIG)
                cols.append(zc)
            _logits_pass(h, jnp.concatenate(cols, axis=1), z_scr, m_scr, mnew_scr)
        for h in range(n_heads):
            hs = slice(h * HEAD_DIM, (h + 1) * HEAD_DIM)
            _softmax_pass(h, vt_ref[0, 0, hs, :], z_scr, m_scr, mnew_scr, l_scr, acc_scr)

    @pl.when(ki < qi)
    def _():
        update(causal=False)

    @pl.when(ki == qi)
    def _():
        update(causal=True)
        for h in range(n_heads):
            hs = slice(h * HEAD_DIM, (h + 1) * HEAD_DIM)
            o = (acc_scr[hs, :] / l_scr[h:h + 1, :]).T
            o_ref[:, hs] = (o * gate_ref[0, :, hs].astype(F32)).astype(BF16)


def _fox_t(main, vt, cneg, *, t):
    n_seg, s_len, seg = main.shape
    assert vt.shape == (2, s_len // t, seg, t)
    nq = s_len // t
    pairs = [(a, b) for a in range(nq) for b in range(a + 1)]
    q_tab = jnp.asarray([a for a, _ in pairs], jnp.int32)
    k_tab = jnp.asarray([b for _, b in pairs], jnp.int32)
    kern = functools.partial(_fox_t_kernel, t=t, n_heads=N_HEADS_B)
    grid_spec = pltpu.PrefetchScalarGridSpec(
        num_scalar_prefetch=2,
        grid=(len(pairs),),
        in_specs=[
            pl.BlockSpec((1, t, seg), lambda p, qt, kt: (4, qt[p], 0)),
            pl.BlockSpec((1, t, seg), lambda p, qt, kt: (5, kt[p], 0)),
            pl.BlockSpec((1, 1, seg, t), lambda p, qt, kt: (1, kt[p], 0, 0)),
            pl.BlockSpec((1, t, seg), lambda p, qt, kt: (7, qt[p], 0)),
            pl.BlockSpec((N_HEADS_B, t, LANES), lambda p, qt, kt: (0, kt[p], 0)),
        ],
        out_specs=pl.BlockSpec((t, seg), lambda p, qt, kt: (qt[p], 0)),
        scratch_shapes=[
            pltpu.VMEM((seg, t), BF16),
            pltpu.VMEM((N_HEADS_B, t, t), F32),
            pltpu.VMEM((N_HEADS_B, t), F32),
            pltpu.VMEM((N_HEADS_B, t), F32),
            pltpu.VMEM((N_HEADS_B, t), F32),
            pltpu.VMEM((seg, t), F32),
        ],
    )
    return pl.pallas_call(
        kern,
        grid_spec=grid_spec,
        out_shape=jax.ShapeDtypeStruct((s_len, seg), BF16),
        compiler_params=_cparams(("arbitrary",)),
        name="fox",
    )(q_tab, k_tab, main, main, vt, main, cneg)


def _out_proj_kernel(x_ref, oa_ref, ob_ref, wa_ref, wb_ref, y_ref):
    y = x_ref[...] + jnp.dot(oa_ref[...], wa_ref[...], preferred_element_type=F32)
    y_ref[...] = y + jnp.dot(ob_ref[...], wb_ref[...], preferred_element_type=F32)


def _out_proj(x, o_a, o_b, w_a, w_b, *, tm):
    s_len, d = x.shape
    seg = o_a.shape[1]
    resident = functools.partial(pl.BlockSpec, pipeline_mode=pl.Buffered(1))
    return pl.pallas_call(
        _out_proj_kernel,
        grid=(s_len // tm,),
        in_specs=[
            pl.BlockSpec((tm, d), lambda i: (i, 0)),
            pl.BlockSpec((tm, seg), lambda i: (i, 0)),
            pl.BlockSpec((tm, seg), lambda i: (i, 0)),
            resident((seg, d), lambda i: (0, 0)),
            resident((seg, d), lambda i: (0, 0)),
        ],
        out_specs=pl.BlockSpec((tm, d), lambda i: (i, 0)),
        out_shape=jax.ShapeDtypeStruct((s_len, d), F32),
        compiler_params=_cparams(("arbitrary",)),
        name="out_proj",
    )(x, o_a, o_b, w_a, w_b)


def _swiglu_step(h, wg, wu, wd):
    g = jnp.dot(h, wg.astype(BF16), preferred_element_type=F32)
    u = jnp.dot(h, wu.astype(BF16), preferred_element_type=F32)
    a = (g * jax.nn.sigmoid(g) * u).astype(BF16)
    return jnp.dot(a, wd.astype(BF16), preferred_element_type=F32)


def _ffn_kernel(x_ref, g_ref, wg_ref, wu_ref, wd_ref, y_ref, h_scr):
    @pl.when(pl.program_id(1) == 0)
    def _():
        x = x_ref[...]
        h_scr[...] = _rms(x, g_ref[...]).astype(BF16)
        y_ref[...] = x

    y_ref[...] += _swiglu_step(h_scr[...], wg_ref[...], wu_ref[...], wd_ref[...])


def _ffn(x, g, wg, wu, wd, *, tm, tf):
    s_len, d = x.shape
    f = wg.shape[1]
    return pl.pallas_call(
        _ffn_kernel,
        grid=(s_len // tm, f // tf),
        in_specs=[
            pl.BlockSpec((tm, d), lambda i, j: (i, 0), pipeline_mode=pl.Buffered(1)),
            pl.BlockSpec((1, d), lambda i, j: (0, 0)),
            pl.BlockSpec((d, tf), lambda i, j: (0, j)),
            pl.BlockSpec((d, tf), lambda i, j: (0, j)),
            pl.BlockSpec((tf, d), lambda i, j: (j, 0)),
        ],
        out_specs=pl.BlockSpec((tm, d), lambda i, j: (i, 0)),
        out_shape=jax.ShapeDtypeStruct((s_len, d), F32),
        scratch_shapes=[pltpu.VMEM((tm, d), BF16)],
        compiler_params=_cparams(("arbitrary", "arbitrary")),
        name="ffn",
    )(x, g, wg, wu, wd)


def _pool_kernel(x_ref, g_ref, w_ref, sc_ref, y_ref, ext_scr, *, halo):
    i = pl.program_id(0)
    tm, d = x_ref.shape
    n_grp = len(POOL_WINDOWS)
    width = d // n_grp

    @pl.when(i == 0)
    def _():
        ext_scr[0:halo, :] = jnp.zeros((halo, d), F32)

    x = x_ref[...]
    ext_scr[halo:halo + tm, :] = _rms(x, g_ref[...])
    t_glob = i * tm + lax.broadcasted_iota(jnp.int32, (tm, 1), 0)
    for gi, win in enumerate(POOL_WINDOWS):
        cs = slice(gi * width, (gi + 1) * width)
        hn = ext_scr[halo:halo + tm, cs]
        tot = hn
        for dlt in range(1, win):
            tot = tot + ext_scr[halo - dlt:halo - dlt + tm, cs]
        count = jnp.minimum(t_glob + 1, win).astype(F32)
        pooled = (tot / count - hn).astype(BF16)
        y = jnp.dot(pooled, w_ref[gi], preferred_element_type=F32)
        y_ref[:, cs] = x[:, cs] + y * sc_ref[:, cs]
    ext_scr[0:halo, :] = ext_scr[tm:tm + halo, :]


def _pool(x, g, w_pool, scale, *, tm):
    s_len, d = x.shape
    halo = 16
    assert max(POOL_WINDOWS) <= halo
    n_grp, width, _ = w_pool.shape
    return pl.pallas_call(
        functools.partial(_pool_kernel, halo=halo),
        grid=(s_len // tm,),
        in_specs=[
            pl.BlockSpec((tm, d), lambda i: (i, 0)),
            pl.BlockSpec((1, d), lambda i: (0, 0)),
            pl.BlockSpec((n_grp, width, width), lambda i: (0, 0, 0)),
            pl.BlockSpec((1, d), lambda i: (0, 0)),
        ],
        out_specs=pl.BlockSpec((tm, d), lambda i: (i, 0)),
        out_shape=jax.ShapeDtypeStruct((s_len, d), F32),
        scratch_shapes=[pltpu.VMEM((tm + halo, d), F32)],
        compiler_params=_cparams(("arbitrary",)),
        name="pool",
    )(x, g, w_pool, scale)


R_E1, R_E2, R_G1, R_G2, R_K1, R_K2 = range(6)


def _router_kernel(x_ref, g_ref, r_ref, route_ref, cnt_ref, carry_scr, *, n_exp):
    i = pl.program_id(0)
    tm = x_ref.shape[0]

    @pl.when(i == 0)
    def _():
        carry_scr[...] = jnp.zeros_like(carry_scr)

    h = _rms(x_ref[...], g_ref[...])
    logits = jnp.dot(h, r_ref[...], preferred_element_type=F32, precision=lax.Precision.HIGHEST)
    lane = lax.broadcasted_iota(jnp.int32, logits.shape, 1)
    logits = jnp.where(lane < n_exp, logits, -jnp.inf)
    lane_f = lane.astype(F32)
    v1 = jnp.max(logits, axis=1, keepdims=True)
    e1 = jnp.min(jnp.where(logits == v1, lane_f, float(LANES)), axis=1, keepdims=True)
    rest = jnp.where(lane_f == e1, -jnp.inf, logits)
    v2 = jnp.max(rest, axis=1, keepdims=True)
    e2 = jnp.min(jnp.where(rest == v2, lane_f, float(LANES)), axis=1, keepdims=True)
    ex = jnp.exp(v2 - v1)
    g1 = 1.0 / (1.0 + ex)
    g2 = ex / (1.0 + ex)
    oh1 = (lane_f == e1).astype(F32)
    oh2 = (lane_f == e2).astype(F32)
    both = oh1 + oh2
    row = lax.broadcasted_iota(jnp.int32, (tm, tm), 0)
    col = lax.broadcasted_iota(jnp.int32, (tm, tm), 1)
    tri = (col < row).astype(BF16)
    before = jnp.dot(tri, both.astype(BF16), preferred_element_type=F32) + carry_scr[...]
    k1 = jnp.sum(before * oh1, axis=1, keepdims=True)
    k2 = jnp.sum(before * oh2, axis=1, keepdims=True)
    carry_scr[...] = carry_scr[...] + jnp.sum(both, axis=0, keepdims=True)
    cnt_ref[...] = carry_scr[...]
    vals = (e1, e2, g1, g2, k1, k2)
    out = jnp.zeros(logits.shape, F32)
    for li, v in enumerate(vals):
        out = jnp.where(lane == li, v, out)
    route_ref[...] = out


def _router(x, g, r_pad, *, tm, n_exp):
    s_len, d = x.shape
    return pl.pallas_call(
        functools.partial(_router_kernel, n_exp=n_exp),
        grid=(s_len // tm,),
        in_specs=[
            pl.BlockSpec((tm, d), lambda i: (i, 0)),
            pl.BlockSpec((1, d), lambda i: (0, 0)),
            pl.BlockSpec((d, LANES), lambda i: (0, 0)),
        ],
        out_specs=[
            pl.BlockSpec((tm, LANES), lambda i: (i, 0)),
            pl.BlockSpec((1, LANES), lambda i: (0, 0)),
        ],
        out_shape=[
            jax.ShapeDtypeStruct((s_len, LANES), F32),
            jax.ShapeDtypeStruct((1, LANES), F32),
        ],
        scratch_shapes=[pltpu.VMEM((1, LANES), F32)],
        compiler_params=_cparams(("arbitrary",)),
        name="router",
    )(x, g, r_pad)


def _pack_bf16_pairs(h):
    half = h.shape[1] // 2
    lo = lax.bitcast_convert_type(h[:, :half].astype(BF16).astype(F32), jnp.uint32)
    hi = lax.bitcast_convert_type(h[:, half:].astype(BF16).astype(F32), jnp.uint32)
    return (lo >> 16) | (hi & jnp.uint32(0xFFFF0000))


def _unpack_bf16_pairs(u):
    lo = lax.bitcast_convert_type(u << 16, F32).astype(BF16)
    hi = lax.bitcast_convert_type(u & jnp.uint32(0xFFFF0000), F32).astype(BF16)
    return lo, hi


def _scatter_kernel(p1_ref, p2_ref, x_ref, g_ref, xs_in_ref, xs_ref, row_scr, sem):
    del xs_in_ref
    i = pl.program_id(0)
    tm = x_ref.shape[0]
    row_scr[...] = _pack_bf16_pairs(_rms(x_ref[...], g_ref[...]))

    def copies(t):
        src = row_scr.at[pl.ds(t, 1), :]
        return (pltpu.make_async_copy(src, xs_ref.at[pl.ds(p1_ref[i * tm + t], 1), :], sem.at[0]),
                pltpu.make_async_copy(src, xs_ref.at[pl.ds(p2_ref[i * tm + t], 1), :], sem.at[1]))

    def start(t, c):
        a, b = copies(t)
        a.start()
        b.start()
        return c

    def wait(t, c):
        a, b = copies(t)
        a.wait()
        b.wait()
        return c

    lax.fori_loop(0, tm, start, 0)
    lax.fori_loop(0, tm, wait, 0)


def _scatter(pos1, pos2, x, g, xs_zero, *, tm):
    s_len, d = x.shape
    grid_spec = pltpu.PrefetchScalarGridSpec(
        num_scalar_prefetch=2,
        grid=(s_len // tm,),
        in_specs=[
            pl.BlockSpec((tm, d), lambda i, p1, p2: (i, 0)),
            pl.BlockSpec((1, d), lambda i, p1, p2: (0, 0)),
            pl.BlockSpec(memory_space=pl.ANY),
        ],
        out_specs=pl.BlockSpec(memory_space=pl.ANY),
        scratch_shapes=[
            pltpu.VMEM((tm, d // 2), jnp.uint32),
            pltpu.SemaphoreType.DMA((2,)),
        ],
    )
    return pl.pallas_call(
        _scatter_kernel,
        grid_spec=grid_spec,
        out_shape=jax.ShapeDtypeStruct(xs_zero.shape, jnp.uint32),
        input_output_aliases={4: 0},
        compiler_params=_cparams(("arbitrary",)),
        name="moe_scatter",
    )(pos1, pos2, x, g, xs_zero)


def _experts_kernel(e_ref, v_ref, xs_ref, wg_ref, wu_ref, wd_ref, o_ref, xb_scr):
    j = pl.program_id(0)
    f = pl.program_id(1)
    half = xs_ref.shape[1]

    @pl.when(f == 0)
    def _():
        lo, hi = _unpack_bf16_pairs(xs_ref[...])
        xb_scr[:, :half] = lo
        xb_scr[:, half:] = hi
        o_ref[...] = jnp.zeros_like(o_ref)

    @pl.when(v_ref[j] > 0)
    def _():
        o_ref[...] += _swiglu_step(xb_scr[...], wg_ref[0], wu_ref[0], wd_ref[0])


def _experts(item_e, item_valid, xs, wg, wu, wd, *, r, tf):
    rows, half = xs.shape
    n_exp, d, f = wg.shape
    n_items = rows // r
    n_f = f // tf

    def f_idx(fi, j, v):
        return jnp.where(v[j] > 0, fi, n_f - 1)

    grid_spec = pltpu.PrefetchScalarGridSpec(
        num_scalar_prefetch=2,
        grid=(n_items, n_f),
        in_specs=[
            pl.BlockSpec((r, half), lambda j, fi, e, v: (j, 0)),
            pl.BlockSpec((1, d, tf), lambda j, fi, e, v: (e[j], 0, f_idx(fi, j, v))),
            pl.BlockSpec((1, d, tf), lambda j, fi, e, v: (e[j], 0, f_idx(fi, j, v))),
            pl.BlockSpec((1, tf, d), lambda j, fi, e, v: (e[j], f_idx(fi, j, v), 0)),
        ],
        out_specs=pl.BlockSpec((r, d), lambda j, fi, e, v: (j, 0)),
        scratch_shapes=[pltpu.VMEM((r, d), BF16)],
    )
    return pl.pallas_call(
        _experts_kernel,
        grid_spec=grid_spec,
        out_shape=jax.ShapeDtypeStruct((rows, d), F32),
        compiler_params=_cparams(("arbitrary", "arbitrary")),
        name="moe_experts",
    )(item_e, item_valid, xs, wg, wu, wd)


def _combine_kernel(p1_ref, p2_ref, x_ref, route_ref, o_hbm, y_ref, a_scr, b_scr, sem):
    i = pl.program_id(0)
    tm = x_ref.shape[0]

    def copies(t):
        return (pltpu.make_async_copy(o_hbm.at[pl.ds(p1_ref[i * tm + t], 1), :],
                                      a_scr.at[pl.ds(t, 1), :], sem.at[0]),
                pltpu.make_async_copy(o_hbm.at[pl.ds(p2_ref[i * tm + t], 1), :],
                                      b_scr.at[pl.ds(t, 1), :], sem.at[1]))

    def start(t, c):
        a, b = copies(t)
        a.start()
        b.start()
        return c

    def wait(t, c):
        a, b = copies(t)
        a.wait()
        b.wait()
        return c

    lax.fori_loop(0, tm, start, 0)
    lax.fori_loop(0, tm, wait, 0)
    route = route_ref[...]
    g1 = route[:, R_G1:R_G1 + 1]
    g2 = route[:, R_G2:R_G2 + 1]
    y_ref[...] = x_ref[...] + g1 * a_scr[...] + g2 * b_scr[...]


def _combine(pos1, pos2, x, route, o_sorted, *, tm):
    s_len, d = x.shape
    grid_spec = pltpu.PrefetchScalarGridSpec(
        num_scalar_prefetch=2,
        grid=(s_len // tm,),
        in_specs=[
            pl.BlockSpec((tm, d), lambda i, p1, p2: (i, 0)),
            pl.BlockSpec((tm, LANES), lambda i, p1, p2: (i, 0)),
            pl.BlockSpec(memory_space=pl.ANY),
        ],
        out_specs=pl.BlockSpec((tm, d), lambda i, p1, p2: (i, 0)),
        scratch_shapes=[
            pltpu.VMEM((tm, d), F32),
            pltpu.VMEM((tm, d), F32),
            pltpu.SemaphoreType.DMA((2,)),
        ],
    )
    return pl.pallas_call(
        _combine_kernel,
        grid_spec=grid_spec,
        out_shape=jax.ShapeDtypeStruct((s_len, d), F32),
        compiler_params=_cparams(("arbitrary",)),
        name="moe_combine",
    )(pos1, pos2, x, route, o_sorted)


def _rotary_tables(s_len, rot_dim, period):
    half = rot_dim // 2
    inv_freq = 1.0 / (ROPE_THETA ** (jnp.arange(half, dtype=F32) * 2.0 / rot_dim))
    ang = jnp.arange(s_len, dtype=F32)[:, None] * inv_freq[None, :]
    cos, sin = jnp.cos(ang), jnp.sin(ang)
    pad = period - rot_dim
    cos_p = jnp.concatenate([cos, cos, jnp.ones((s_len, pad), F32)], axis=1)
    sin_p = jnp.concatenate([-sin, sin, jnp.zeros((s_len, pad), F32)], axis=1)
    reps = LANES // period
    return jnp.tile(cos_p, (1, reps)), jnp.tile(sin_p, (1, reps))


def _split_w_in(w_in):
    seg_a = N_HEADS_A * HEAD_DIM
    seg_i = N_IDX_HEADS * IDX_DIM
    seg_b = N_HEADS_B * HEAD_DIM
    sizes = (seg_a, seg_a, seg_a, seg_i, IDX_DIM, N_IDX_HEADS, seg_b, seg_b, seg_b, N_HEADS_B, seg_b)
    offs = np.cumsum(sizes)[:-1].tolist()
    return jnp.split(w_in, offs, axis=1)


def _moe_tables(counts, e1, e2, k1, k2, *, r, n_items):
    n_exp = counts.shape[0]
    blocks = (counts + r - 1) // r
    bend = jnp.cumsum(blocks)
    bstart = bend - blocks
    j = jnp.arange(n_items, dtype=jnp.int32)
    item_e = jnp.sum((j[:, None] >= bend[None, :]).astype(jnp.int32), axis=1)
    valid = item_e < n_exp
    last_e = jnp.minimum(item_e[jnp.maximum(bend[-1] - 1, 0)], n_exp - 1)
    item_e = jnp.where(valid, item_e, last_e).astype(jnp.int32)
    pos1 = bstart[e1] * r + k1
    pos2 = bstart[e2] * r + k2
    return item_e, valid.astype(jnp.int32), pos1.astype(jnp.int32), pos2.astype(jnp.int32)


def _moe_block_rows(n_assign, n_exp):
    r = -(-n_assign // (2 * n_exp))
    r = -(-(r + r // 16) // 64) * 64
    return max(r, 64)


def kernel(x, attn_norm_e, w_in_e, b_forget_e, q_norm_a_e, k_norm_a_e, q_norm_b_e, k_norm_b_e,
           w_out_e, ffn_norm_e, w_gate_e, w_up_e, w_down_e, pool_norm_o, w_pool_o, pool_scale_o,
           moe_norm_o, router_o, w_gate_o, w_up_o, w_down_o):
    b, s_len, d = x.shape
    assert b == 1
    x0 = x[0]
    seg = N_HEADS_A * HEAD_DIM
    assert seg == N_IDX_HEADS * IDX_DIM == N_HEADS_B * HEAD_DIM
    assert N_IDX_HEADS <= FB_OFF - WI_OFF and N_HEADS_B <= LANES - FB_OFF
    row = lambda v: v.reshape(1, -1)

    qa, ka, va, qi, ki, wi, qb, kb, vb, fb, gb = _split_w_in(w_in_e[0])
    w_main = jnp.concatenate([qa, ka, va, qi, qb, kb, vb, gb], axis=1).astype(BF16)
    w_small = jnp.concatenate(
        [ki, wi, jnp.zeros((d, FB_OFF - WI_OFF - N_IDX_HEADS), F32), fb,
         jnp.zeros((d, LANES - FB_OFF - N_HEADS_B), F32)], axis=1).astype(BF16)
    bf_pad = jnp.zeros((1, LANES), F32).at[0, FB_OFF:FB_OFF + N_HEADS_B].set(b_forget_e[0])
    scale = HEAD_DIM ** -0.5
    gains = jnp.stack([q_norm_a_e[0] * scale, k_norm_a_e[0], q_norm_b_e[0] * scale, k_norm_b_e[0]])
    tabs = _rotary_tables(s_len, ROT_DIM, HEAD_DIM) + _rotary_tables(s_len, IDX_ROT_DIM, IDX_DIM)
    t_attn = _tile(s_len, 512)
    main, small, v_t, c_neg = _in_proj(x0, row(attn_norm_e[0]), w_main, w_small, bf_pad, gains, tabs,
                                       n_seg=8, seg=seg, tm=t_attn)
    ki_rot = small[:, KI_OFF:KI_OFF + IDX_DIM].astype(BF16)
    wi_t = small[:, WI_OFF:WI_OFF + N_IDX_HEADS].T

    top_k = min(TOPK_MAX, s_len // 4)
    o_a = _dsa_t(main, v_t, ki_rot, wi_t, tq=_tile(s_len, 256), tk=t_attn, top_k=top_k)
    o_b = _fox_t(main, v_t, c_neg, t=t_attn)
    w_out = w_out_e[0].astype(BF16)
    x1 = _out_proj(x0, o_a, o_b, w_out[:seg], w_out[seg:], tm=_tile(s_len, 512))

    x2 = _ffn(x1, row(ffn_norm_e[0]), w_gate_e[0], w_up_e[0], w_down_e[0],
              tm=_tile(s_len, 1024), tf=_tile(w_gate_e.shape[2], 256))

    x3 = _pool(x2, row(pool_norm_o[0]), w_pool_o[0].astype(BF16), row(pool_scale_o[0]),
               tm=_tile(s_len, 512))

    n_exp = router_o.shape[2]
    r_pad = jnp.zeros((d, LANES), F32).at[:, :n_exp].set(router_o[0])
    route, counts = _router(x3, row(moe_norm_o[0]), r_pad, tm=_tile(s_len, 512), n_exp=n_exp)
    r_rows = _moe_block_rows(2 * s_len, n_exp)
    n_items = (2 * s_len) // r_rows + n_exp
    as_int = lambda c: route[:, c].astype(jnp.int32)
    item_e, item_valid, pos1, pos2 = _moe_tables(
        counts[0, :n_exp].astype(jnp.int32), as_int(R_E1), as_int(R_E2), as_int(R_K1), as_int(R_K2),
        r=r_rows, n_items=n_items)
    xs_zero = jnp.zeros((n_items * r_rows, d // 2), jnp.uint32)
    xs = _scatter(pos1, pos2, x3, row(moe_norm_o[0]), xs_zero, tm=_tile(s_len, 256))
    o_sorted = _experts(item_e, item_valid, xs, w_gate_o[0], w_up_o[0], w_down_o[0],
                        r=r_rows, tf=_tile(w_gate_o.shape[3], 256))
    y = _combine(pos1, pos2, x3, route, o_sorted, tm=_tile(s_len, 256))
    return y[None]
```

```python
import functools

import jax
import jax.numpy as jnp
import numpy as np
from jax import lax
from jax.experimental import pallas as pl
from jax.experimental.pallas import tpu as pltpu

HEAD_DIM = 128
N_HEADS_A = 8
N_HEADS_B = 8
N_IDX_HEADS = 16
IDX_DIM = 64
ROT_DIM = HEAD_DIM // 4
IDX_ROT_DIM = IDX_DIM // 4
ROPE_THETA = 500000.0
CHUNK = 64
TOPK_MAX = 256
POOL_WINDOWS = (2, 4, 8, 16)
RMS_EPS = 1e-6

LANES = 128
MXU_COLS = 256
VMEM_LIMIT_BYTES = 56 * 1024 * 1024

KI_OFF = 0
WI_OFF = IDX_DIM
FB_OFF = 80

NEG_BIG = -1e30
INT_MIN = -(2 ** 31)
INT_MAX = 2 ** 31 - 1
KEY_NONE = INT_MIN + 1
HALF_MIN = -(2 ** 15)
HALF_MAX = 2 ** 15 - 1

F32 = jnp.float32
BF16 = jnp.bfloat16


def _cparams(sem):
    return pltpu.CompilerParams(dimension_semantics=sem, vmem_limit_bytes=VMEM_LIMIT_BYTES)


def _tile(n, pref):
    t = min(n, pref)
    while n % t:
        t //= 2
    return t


def _rms(x, g):
    ms = jnp.mean(x * x, axis=-1, keepdims=True)
    return x * lax.rsqrt(ms + RMS_EPS) * g


def _nt_dot(a, b):
    return lax.dot_general(a, b, (((1,), (1,)), ((), ())), preferred_element_type=F32)


def _rotate(y, cos_t, sin_t, half, period):
    lane = lax.broadcasted_iota(jnp.int32, y.shape, 1)
    up = pltpu.roll(y, LANES - half, 1)
    dn = pltpu.roll(y, half, 1)
    rot = jnp.where((lane & (period - 1)) < half, up, dn)
    return y * cos_t + rot * sin_t


def _in_proj_kernel(x_ref, g_ref, wm_ref, ws_ref, bf_ref, gains_ref, cosa_ref, sina_ref,
                    cosi_ref, sini_ref, main_ref, small_ref, vt_ref, cneg_ref, ki_ref, h_scr,
                    carry_scr, *, n_seg, idx_scale, n_fb):
    i = pl.program_id(0)
    j = pl.program_id(1)
    tm = h_scr.shape[0]
    seg = wm_ref.shape[1]

    @pl.when(j == 0)
    def _():
        h_scr[...] = _rms(x_ref[...], g_ref[...]).astype(BF16)

    def head_norm(slab, gain):
        ms = jnp.mean(slab * slab, axis=-1, keepdims=True)
        return slab * lax.rsqrt(ms + RMS_EPS) * gain

    def project(epilogue):
        for c0 in range(0, seg, MXU_COLS):
            acc = jnp.dot(h_scr[...], wm_ref[:, c0:c0 + MXU_COLS], preferred_element_type=F32)
            epilogue(acc, c0)

    def per_slab(fn):
        def epilogue(acc, c0):
            for s0 in range(0, MXU_COLS, LANES):
                y = fn(acc[:, s0:s0 + LANES])
                main_ref[0, :, c0 + s0:c0 + s0 + LANES] = y.astype(BF16)
        return epilogue

    @pl.when(j < 2)
    def _():
        gain = gains_ref[pl.ds(j, 1), :]
        project(per_slab(lambda y: _rotate(head_norm(y, gain), cosa_ref[...], sina_ref[...],
                                           ROT_DIM // 2, LANES)))

    @pl.when((j == 4) | (j == 5))
    def _():
        gain = gains_ref[pl.ds(j - 2, 1), :]
        project(per_slab(lambda y: head_norm(y, gain)))

    @pl.when((j == 2) | (j == 6))
    def _():
        def epilogue(acc, c0):
            main_ref[0, :, c0:c0 + MXU_COLS] = acc.astype(BF16)
            vt_ref[0, 0, c0:c0 + MXU_COLS, :] = acc.T.astype(BF16)
        project(epilogue)

    @pl.when(j == 3)
    def _():
        project(per_slab(lambda y: _rotate(y, cosi_ref[...], sini_ref[...], IDX_ROT_DIM // 2, IDX_DIM)))

    @pl.when(j == 7)
    def _():
        project(per_slab(jax.nn.sigmoid))

    @pl.when(j == n_seg)
    def _():
        acc = jnp.dot(h_scr[...], ws_ref[...], preferred_element_type=F32)
        lane = lax.broadcasted_iota(jnp.int32, acc.shape, 1)
        is_ki = lane < WI_OFF
        cos_k = jnp.where(is_ki, cosi_ref[...], 1.0)
        sin_k = jnp.where(is_ki, sini_ref[...], 0.0)
        ki = _rotate(acc, cos_k, sin_k, IDX_ROT_DIM // 2, IDX_DIM)
        wi = acc * idx_scale
        z = acc + bf_ref[...]
        ls = jnp.minimum(z, 0.0) - jnp.log1p(jnp.exp(-jnp.abs(z)))
        is_fb = (lane >= FB_OFF) & (lane < FB_OFF + n_fb)
        ls = jnp.where(is_fb, ls, 0.0)
        row = lax.broadcasted_iota(jnp.int32, (tm, tm), 0)
        col = lax.broadcasted_iota(jnp.int32, (tm, tm), 1)
        tri = (col <= row).astype(BF16)
        p1 = ls.astype(BF16)
        r1 = ls - p1.astype(F32)
        p2 = r1.astype(BF16)
        p3 = (r1 - p2.astype(F32)).astype(BF16)
        c = (jnp.dot(tri, p1, preferred_element_type=F32)
             + jnp.dot(tri, p2, preferred_element_type=F32)
             + jnp.dot(tri, p3, preferred_element_type=F32))

        @pl.when(i == 0)
        def _():
            carry_scr[...] = jnp.zeros_like(carry_scr)

        c = c + carry_scr[...]
        carry_scr[...] = c[tm - 1:tm, :]
        small_ref[...] = jnp.where(is_ki, ki, jnp.where(lane < FB_OFF, wi, c))
        ki_ref[...] = jnp.where(is_ki, ki, 0.0).astype(BF16)
        for hb in range(n_fb):
            cneg_ref[hb] = jnp.broadcast_to(-c[:, FB_OFF + hb:FB_OFF + hb + 1], (tm, LANES))


def _in_proj(x, g, w_main, w_small, bf_pad, gains, tabs, *, n_seg, seg, tm):
    s_len, d = x.shape
    cosa, sina, cosi, sini = tabs
    grid = (s_len // tm, n_seg + 1)
    row_spec = lambda w: pl.BlockSpec((tm, w), lambda i, j: (i, 0))
    kern = functools.partial(_in_proj_kernel, n_seg=n_seg,
                             idx_scale=float((N_IDX_HEADS * IDX_DIM) ** -0.5), n_fb=N_HEADS_B)
    return pl.pallas_call(
        kern,
        grid=grid,
        in_specs=[
            pl.BlockSpec((tm, d), lambda i, j: (i, 0), pipeline_mode=pl.Buffered(1)),
            pl.BlockSpec((1, d), lambda i, j: (0, 0)),
            pl.BlockSpec((d, seg), lambda i, j: (0, jnp.minimum(j, n_seg - 1))),
            pl.BlockSpec((d, LANES), lambda i, j: (0, 0)),
            pl.BlockSpec((1, LANES), lambda i, j: (0, 0)),
            pl.BlockSpec((4, LANES), lambda i, j: (0, 0)),
            row_spec(LANES), row_spec(LANES), row_spec(LANES), row_spec(LANES),
        ],
        out_specs=[
            pl.BlockSpec((1, tm, seg), lambda i, j: (jnp.minimum(j, n_seg - 1), i, 0)),
            pl.BlockSpec((tm, LANES), lambda i, j: (i, 0)),
            pl.BlockSpec((1, 1, seg, tm), lambda i, j: (jnp.where(j <= 2, 0, 1), i, 0, 0)),
            pl.BlockSpec((N_HEADS_B, tm, LANES), lambda i, j: (0, i, 0)),
            pl.BlockSpec((tm, LANES), lambda i, j: (i, 0)),
        ],
        out_shape=[
            jax.ShapeDtypeStruct((n_seg, s_len, seg), BF16),
            jax.ShapeDtypeStruct((s_len, LANES), F32),
            jax.ShapeDtypeStruct((2, s_len // tm, seg, tm), BF16),
            jax.ShapeDtypeStruct((N_HEADS_B, s_len, LANES), F32),
            jax.ShapeDtypeStruct((s_len, LANES), BF16),
        ],
        scratch_shapes=[
            pltpu.VMEM((tm, d), BF16),
            pltpu.VMEM((1, LANES), F32),
        ],
        compiler_params=_cparams(("arbitrary", "arbitrary")),
        name="in_proj",
    )(x, g, w_main, w_small, bf_pad, gains, cosa, sina, cosi, sini)


def _dsa_kernel(qa_ref, qi_ref, wi_ref, ki_ref, ka_ref, va_ref, o_ref, key_scr,
                *, tq, tk, top_k, n_heads, n_idx):
    i = pl.program_id(0)
    n_kt = ((i + 1) * tq + tk - 1) // tk
    t_glob = i * tq + lax.broadcasted_iota(jnp.int32, (tq, 1), 0)
    adm_len = (t_glob // CHUNK + 1) * CHUNK

    wi = wi_ref[...]

    def score_tile(kt, carry):
        k0 = pl.multiple_of(kt * tk, tk)
        kt_i = ki_ref[pl.ds(k0, tk), :]
        acc = jnp.zeros((tq, tk), F32)
        for h in range(n_idx):
            qh = qi_ref[0, :, h * IDX_DIM:(h + 1) * IDX_DIM]
            rel = jnp.maximum(_nt_dot(qh, kt_i), 0.0)
            acc = acc + rel * wi[:, h:h + 1]
        bits = lax.bitcast_convert_type(acc, jnp.int32)
        key = jnp.where(bits < 0, bits ^ INT_MAX, bits)
        kpos = k0 + lax.broadcasted_iota(jnp.int32, (1, tk), 1)
        key_scr[kt] = jnp.where(kpos < adm_len, key, KEY_NONE)
        return carry

    lax.fori_loop(0, n_kt, score_tile, 0)

    lane_pos = lax.broadcasted_iota(jnp.int32, (1, LANES), 1)

    def count(pred):
        def body(kt, cnt):
            keys = key_scr[kt]
            for c in range(tk // LANES):
                hit = pred(keys[:, c * LANES:(c + 1) * LANES], kt * tk + c * LANES + lane_pos)
                cnt = cnt + jnp.where(hit, 1, 0)
            return cnt

        cnt = lax.fori_loop(0, n_kt, body, jnp.zeros((tq, LANES), jnp.int32))
        return jnp.sum(cnt, axis=1, keepdims=True)

    def bisect(_, state):
        lo, hi, c_lo, c_hi = state
        mid = (lo >> 1) + (hi >> 1) + (lo & hi & 1)
        mid_b = jnp.broadcast_to(mid, (tq, LANES))
        c_mid = count(lambda k, _: k >= mid_b)
        take = c_mid >= top_k
        return (jnp.where(take, mid, lo), jnp.where(take, hi, mid),
                jnp.where(take, c_mid, c_lo), jnp.where(take, c_hi, c_mid))

    col = lambda v: jnp.full((tq, 1), v, jnp.int32)
    thr, _, c_ge, c_gt = lax.fori_loop(0, 32, bisect,
                                       (col(INT_MIN), col(INT_MAX), n_kt * tk + col(0), col(0)))
    short = thr <= KEY_NONE
    thr = jnp.maximum(thr, KEY_NONE)
    need = top_k - c_gt
    thr_b = jnp.broadcast_to(thr, (tq, LANES))

    def tie_cut():
        def step(_, lohi):
            lo, hi = lohi
            mid = (lo + hi) >> 1
            mid_b = jnp.broadcast_to(mid, (tq, LANES))
            ok = count(lambda k, pos: (k == thr_b) & (pos <= mid_b)) >= need
            return jnp.where(ok, lo, mid), jnp.where(ok, mid, hi)

        n_steps = int(np.ceil(np.log2(key_scr.shape[0] * tk))) + 1
        return lax.fori_loop(0, n_steps, step, (col(-1), n_kt * tk - 1 + col(0)))[1]

    has_tie = jnp.max(jnp.where((c_ge > top_k) & ~short, 1.0, 0.0)) > 0.5
    cut = lax.cond(has_tie, tie_cut, lambda: col(INT_MAX))
    cut = jnp.where(short, -1, cut)

    def bias_tile(kt, carry):
        keys = key_scr[kt]
        kpos = kt * tk + lax.broadcasted_iota(jnp.int32, (1, tk), 1)
        sel = (keys > thr) | ((keys == thr) & (kpos <= cut))
        key_scr[kt] = lax.bitcast_convert_type(jnp.where(sel, 0.0, NEG_BIG).astype(F32), jnp.int32)
        return carry

    lax.fori_loop(0, n_kt, bias_tile, 0)

    for h in range(n_heads):
        hs = slice(h * HEAD_DIM, (h + 1) * HEAD_DIM)
        qh = qa_ref[0, :, hs]

        def attend(kt, mla):
            m, l, acc = mla
            k0 = pl.multiple_of(kt * tk, tk)
            kh = ka_ref[0, pl.ds(k0, tk), hs]
            vh = va_ref[0, pl.ds(k0, tk), hs]
            z = _nt_dot(qh, kh) + lax.bitcast_convert_type(key_scr[kt], F32)
            m_new = jnp.maximum(m, jnp.max(z, axis=1, keepdims=True))
            alpha = jnp.exp(m - m_new)
            p = jnp.exp(z - m_new)
            l = alpha * l + jnp.sum(p, axis=1, keepdims=True)
            acc = alpha * acc + jnp.dot(p.astype(BF16), vh, preferred_element_type=F32)
            return m_new, l, acc

        m0 = jnp.full((tq, 1), NEG_BIG, F32)
        l0 = jnp.zeros((tq, 1), F32)
        a0 = jnp.zeros((tq, HEAD_DIM), F32)
        m, l, acc = lax.fori_loop(0, n_kt, attend, (m0, l0, a0))
        o_ref[:, hs] = (acc / l).astype(BF16)


def _dsa(main, ki, wi, *, tq, tk, top_k):
    n_seg, s_len, seg = main.shape
    n_kt_max = s_len // tk
    kern = functools.partial(_dsa_kernel, tq=tq, tk=tk, top_k=top_k, n_heads=N_HEADS_A,
                             n_idx=N_IDX_HEADS)
    resident = functools.partial(pl.BlockSpec, pipeline_mode=pl.Buffered(1))
    return pl.pallas_call(
        kern,
        grid=(s_len // tq,),
        in_specs=[
            pl.BlockSpec((1, tq, seg), lambda i: (0, i, 0)),
            pl.BlockSpec((1, tq, seg), lambda i: (3, i, 0)),
            pl.BlockSpec((tq, N_IDX_HEADS), lambda i: (i, 0)),
            resident((s_len, IDX_DIM), lambda i: (0, 0)),
            resident((1, s_len, seg), lambda i: (1, 0, 0)),
            resident((1, s_len, seg), lambda i: (2, 0, 0)),
        ],
        out_specs=pl.BlockSpec((tq, seg), lambda i: (i, 0)),
        out_shape=jax.ShapeDtypeStruct((s_len, seg), BF16),
        scratch_shapes=[pltpu.VMEM((n_kt_max, tq, tk), jnp.int32)],
        compiler_params=_cparams(("arbitrary",)),
        name="dsa",
    )(main, main, wi, ki, main, main)


def _fox_kernel(qt_ref, kt_ref, q_ref, k_ref, v_ref, gate_ref, ctq_ref, ctk_ref, o_ref,
                m_scr, l_scr, acc_scr, *, t, n_heads):
    p = pl.program_id(0)
    qi = qt_ref[p]
    ki = kt_ref[p]

    @pl.when(ki == 0)
    def _():
        m_scr[...] = jnp.full_like(m_scr, NEG_BIG)
        l_scr[...] = jnp.zeros_like(l_scr)
        acc_scr[...] = jnp.zeros_like(acc_scr)

    row = lax.broadcasted_iota(jnp.int32, (t, t), 0)
    col = lax.broadcasted_iota(jnp.int32, (t, t), 1)
    visible = (col <= row) | (ki < qi)

    for h in range(n_heads):
        hs = slice(h * HEAD_DIM, (h + 1) * HEAD_DIM)
        bias = ctq_ref[h:h + 1, 0:1] - ctk_ref[h:h + 1, :]
        z = _nt_dot(q_ref[0, :, hs], k_ref[0, :, hs]) + bias
        z = jnp.where(visible, z, NEG_BIG)
        m_old = m_scr[h]
        m_new = jnp.maximum(m_old, jnp.max(z, axis=1, keepdims=True))
        alpha = jnp.exp(m_old - m_new)
        pr = jnp.exp(z - m_new)
        l_scr[h] = alpha * l_scr[h] + jnp.sum(pr, axis=1, keepdims=True)
        acc_scr[:, hs] = alpha * acc_scr[:, hs] + jnp.dot(pr.astype(BF16), v_ref[0, :, hs],
                                                           preferred_element_type=F32)
        m_scr[h] = m_new

    @pl.when(ki == qi)
    def _():
        for h in range(n_heads):
            hs = slice(h * HEAD_DIM, (h + 1) * HEAD_DIM)
            o = acc_scr[:, hs] / l_scr[h]
            o_ref[:, hs] = (o * gate_ref[0, :, hs].astype(F32)).astype(BF16)


def _fox(main, c_t, *, t):
    n_seg, s_len, seg = main.shape
    nq = s_len // t
    pairs = [(a, b) for a in range(nq) for b in range(a + 1)]
    q_tab = jnp.asarray([a for a, _ in pairs], jnp.int32)
    k_tab = jnp.asarray([b for _, b in pairs], jnp.int32)
    n_hb = c_t.shape[0]
    kern = functools.partial(_fox_kernel, t=t, n_heads=N_HEADS_B)
    grid_spec = pltpu.PrefetchScalarGridSpec(
        num_scalar_prefetch=2,
        grid=(len(pairs),),
        in_specs=[
            pl.BlockSpec((1, t, seg), lambda p, qt, kt: (4, qt[p], 0)),
            pl.BlockSpec((1, t, seg), lambda p, qt, kt: (5, kt[p], 0)),
            pl.BlockSpec((1, t, seg), lambda p, qt, kt: (6, kt[p], 0)),
            pl.BlockSpec((1, t, seg), lambda p, qt, kt: (7, qt[p], 0)),
            pl.BlockSpec((n_hb, t), lambda p, qt, kt: (0, qt[p])),
            pl.BlockSpec((n_hb, t), lambda p, qt, kt: (0, kt[p])),
        ],
        out_specs=pl.BlockSpec((t, seg), lambda p, qt, kt: (qt[p], 0)),
        scratch_shapes=[
            pltpu.VMEM((N_HEADS_B, t, 1), F32),
            pltpu.VMEM((N_HEADS_B, t, 1), F32),
            pltpu.VMEM((t, seg), F32),
        ],
    )
    return pl.pallas_call(
        kern,
        grid_spec=grid_spec,
        out_shape=jax.ShapeDtypeStruct((s_len, seg), BF16),
        compiler_params=_cparams(("arbitrary",)),
        name="fox",
    )(q_tab, k_tab, main, main, main, main, c_t, c_t)


def _transpose_heads(src, dst_scr):
    for s in range(src.shape[1] // LANES):
        sl = slice(s * LANES, (s + 1) * LANES)
        dst_scr[sl, :] = src[:, sl].astype(F32).T.astype(BF16)


def _logits_pass(h, z_t, z_scr, m_scr, mnew_scr):
    z_scr[h] = z_t
    mnew_scr[h:h + 1, :] = jnp.maximum(m_scr[h:h + 1, :], jnp.max(z_t, axis=0, keepdims=True))


def _softmax_pass(h, v_t, z_scr, m_scr, mnew_scr, l_scr, acc_scr):
    hs = slice(h * HEAD_DIM, (h + 1) * HEAD_DIM)
    m_new = mnew_scr[h:h + 1, :]
    alpha = jnp.exp(m_scr[h:h + 1, :] - m_new)
    p_t = jnp.exp(z_scr[h] - m_new)
    l_scr[h:h + 1, :] = alpha * l_scr[h:h + 1, :] + jnp.sum(p_t, axis=0, keepdims=True)
    acc_scr[hs, :] = alpha * acc_scr[hs, :] + jnp.dot(v_t, p_t.astype(BF16),
                                                      preferred_element_type=F32)
    m_scr[h:h + 1, :] = m_new


def _init_attn_state(m_scr, l_scr, acc_scr):
    m_scr[...] = jnp.full_like(m_scr, NEG_BIG)
    l_scr[...] = jnp.zeros_like(l_scr)
    acc_scr[...] = jnp.zeros_like(acc_scr)


def _dsa_t_kernel(qa_ref, qi_ref, wit_ref, ki_ref, ka_ref, vat_ref, o_ref,
                  key_scr, half_scr, qat_scr, qit_scr, z_scr, m_scr, mnew_scr, l_scr, acc_scr,
                  *, tq, tk, top_k, n_heads, n_idx):
    i = pl.program_id(0)
    n_kt = ((i + 1) * tq + tk - 1) // tk
    t_glob = i * tq + lax.broadcasted_iota(jnp.int32, (1, tq), 1)
    adm_len = (t_glob // CHUNK + 1) * CHUNK
    key_pos = lax.broadcasted_iota(jnp.int32, (tk, 1), 0)

    _transpose_heads(qa_ref[0], qat_scr)
    _transpose_heads(qi_ref[0], qit_scr)

    def score_tile(kt, carry):
        k0 = pl.multiple_of(kt * tk, tk)
        k_idx = ki_ref[pl.ds(k0, tk), 0:IDX_DIM]
        acc = jnp.zeros((tk, tq), F32)
        for h in range(n_idx):
            q_t = qit_scr[h * IDX_DIM:(h + 1) * IDX_DIM, :]
            rel = jnp.maximum(jnp.dot(k_idx, q_t, preferred_element_type=F32), 0.0)
            acc = acc + rel * wit_ref[h:h + 1, :]
        bits = lax.bitcast_convert_type(acc, jnp.int32)
        key = jnp.where(bits < 0, bits ^ INT_MAX, bits)
        key = jnp.where(k0 + key_pos < adm_len, key, KEY_NONE)
        key_scr[kt] = key
        half_scr[kt] = (key >> 16).astype(jnp.int16)
        return carry

    lax.fori_loop(0, n_kt, score_tile, 0)

    def count(pred):
        def body(kt, cnt):
            hit = jnp.where(pred(key_scr[kt], kt * tk + key_pos), 1, 0)
            return cnt + jnp.sum(hit.reshape(tk // 8, 8, tq), axis=0)

        cnt = lax.fori_loop(0, n_kt, body, jnp.zeros((8, tq), jnp.int32))
        return jnp.sum(cnt, axis=0, keepdims=True)

    def count_half_ge(mid):
        mid16 = mid.astype(jnp.int16)

        def body(kt, cnt):
            hit = jnp.where(half_scr[kt] >= mid16, jnp.int16(1), jnp.int16(0))
            parts = [hit[r * 16:(r + 1) * 16] for r in range(tk // 16)]
            while len(parts) > 1:
                parts = [parts[a] + parts[a + 1] for a in range(0, len(parts), 2)]
            return cnt + parts[0].astype(jnp.int32)

        cnt = lax.fori_loop(0, n_kt, body, jnp.zeros((16, tq), jnp.int32))
        return jnp.sum(cnt, axis=0, keepdims=True)

    def bisect_half(need_k, c_all):
        def step(_, state):
            lo, hi, c_lo, c_hi = state
            mid = (lo + hi) >> 1
            c_mid = count_half_ge(mid)
            take = c_mid >= need_k
            return (jnp.where(take, mid, lo), jnp.where(take, hi, mid),
                    jnp.where(take, c_mid, c_lo), jnp.where(take, c_hi, c_mid))

        x, _, c_x, c_above = lax.fori_loop(0, 16, step,
                                           (row(HALF_MIN), row(HALF_MAX + 1), c_all, row(0)))
        return x, c_x, c_above

    row = lambda v: jnp.full((1, tq), v, jnp.int32)
    thr_hi, c_ge_hi, c_gt_hi = bisect_half(top_k, n_kt * tk + row(0))
    thr_hi16 = thr_hi.astype(jnp.int16)

    def low_half_tile(kt, carry):
        low = ((key_scr[kt] & 0xFFFF) + HALF_MIN).astype(jnp.int16)
        half_scr[kt] = jnp.where(half_scr[kt] == thr_hi16, low, jnp.int16(HALF_MIN))
        return carry

    lax.fori_loop(0, n_kt, low_half_tile, 0)
    thr_lo, c_ge_lo, c_gt_lo = bisect_half(top_k - c_gt_hi, c_ge_hi - c_gt_hi)
    thr = thr_hi * 65536 + (thr_lo - HALF_MIN)
    c_ge = c_gt_hi + c_ge_lo
    c_gt = c_gt_hi + c_gt_lo
    short = thr <= KEY_NONE
    thr = jnp.maximum(thr, KEY_NONE)
    need = top_k - c_gt

    def tie_cut():
        def step(_, lohi):
            lo, hi = lohi
            mid = (lo + hi) >> 1
            ok = count(lambda k, pos: (k == thr) & (pos <= mid)) >= need
            return jnp.where(ok, lo, mid), jnp.where(ok, mid, hi)

        n_steps = int(np.ceil(np.log2(key_scr.shape[0] * tk))) + 1
        return lax.fori_loop(0, n_steps, step, (row(-1), n_kt * tk - 1 + row(0)))[1]

    has_tie = jnp.max(jnp.where((c_ge > top_k) & ~short, 1.0, 0.0)) > 0.5
    cut = lax.cond(has_tie, tie_cut, lambda: row(INT_MAX))
    cut = jnp.where(short, -1, cut)

    def bias_tile(kt, carry):
        keys = key_scr[kt]
        sel = (keys > thr) | ((keys == thr) & (kt * tk + key_pos <= cut))
        key_scr[kt] = lax.bitcast_convert_type(jnp.where(sel, 0.0, NEG_BIG).astype(F32), jnp.int32)
        return carry

    lax.fori_loop(0, n_kt, bias_tile, 0)

    _init_attn_state(m_scr, l_scr, acc_scr)

    def attend(kt, carry):
        k0 = pl.multiple_of(kt * tk, tk)
        bias = lax.bitcast_convert_type(key_scr[kt], F32)
        for h in range(n_heads):
            hs = slice(h * HEAD_DIM, (h + 1) * HEAD_DIM)
            z_t = jnp.dot(ka_ref[0, pl.ds(k0, tk), hs], qat_scr[hs, :],
                          preferred_element_type=F32) + bias
            _logits_pass(h, z_t, z_scr, m_scr, mnew_scr)
        for h in range(n_heads):
            hs = slice(h * HEAD_DIM, (h + 1) * HEAD_DIM)
            _softmax_pass(h, vat_ref[0, kt, hs, :], z_scr, m_scr, mnew_scr, l_scr, acc_scr)
        return carry

    lax.fori_loop(0, n_kt, attend, 0)
    for h in range(n_heads):
        hs = slice(h * HEAD_DIM, (h + 1) * HEAD_DIM)
        o_ref[:, hs] = (acc_scr[hs, :] / l_scr[h:h + 1, :]).T.astype(BF16)


def _dsa_t(main, vt, ki, wi_t, *, tq, tk, top_k):
    n_seg, s_len, seg = main.shape
    assert vt.shape == (2, s_len // tk, seg, tk)
    kern = functools.partial(_dsa_t_kernel, tq=tq, tk=tk, top_k=top_k, n_heads=N_HEADS_A,
                             n_idx=N_IDX_HEADS)
    resident = functools.partial(pl.BlockSpec, pipeline_mode=pl.Buffered(1))
    return pl.pallas_call(
        kern,
        grid=(s_len // tq,),
        in_specs=[
            pl.BlockSpec((1, tq, seg), lambda i: (0, i, 0)),
            pl.BlockSpec((1, tq, seg), lambda i: (3, i, 0)),
            pl.BlockSpec((N_IDX_HEADS, tq), lambda i: (0, i)),
            resident((s_len, LANES), lambda i: (0, 0)),
            resident((1, s_len, seg), lambda i: (1, 0, 0)),
            resident((1, s_len // tk, seg, tk), lambda i: (0, 0, 0, 0)),
        ],
        out_specs=pl.BlockSpec((tq, seg), lambda i: (i, 0)),
        out_shape=jax.ShapeDtypeStruct((s_len, seg), BF16),
        scratch_shapes=[
            pltpu.VMEM((s_len // tk, tk, tq), jnp.int32),
            pltpu.VMEM((s_len // tk, tk, tq), jnp.int16),
            pltpu.VMEM((seg, tq), BF16),
            pltpu.VMEM((seg, tq), BF16),
            pltpu.VMEM((N_HEADS_A, tk, tq), F32),
            pltpu.VMEM((N_HEADS_A, tq), F32),
            pltpu.VMEM((N_HEADS_A, tq), F32),
            pltpu.VMEM((N_HEADS_A, tq), F32),
            pltpu.VMEM((seg, tq), F32),
        ],
        compiler_params=_cparams(("arbitrary",)),
        name="dsa",
    )(main, main, wi_t, ki, main, vt)


def _fox_t_kernel(qt_ref, kt_ref, q_ref, k_ref, vt_ref, gate_ref, cneg_ref, o_ref,
                  qt_scr, z_scr, m_scr, mnew_scr, l_scr, acc_scr, *, t, n_heads):
    p = pl.program_id(0)
    qi = qt_ref[p]
    ki = kt_ref[p]

    @pl.when(ki == 0)
    def _():
        _transpose_heads(q_ref[0], qt_scr)
        _init_attn_state(m_scr, l_scr, acc_scr)

    def update(causal):
        key_i = lax.broadcasted_iota(jnp.int32, (t, LANES), 0)
        qry_i = lax.broadcasted_iota(jnp.int32, (t, LANES), 1)
        for h in range(n_heads):
            hs = slice(h * HEAD_DIM, (h + 1) * HEAD_DIM)
            z_t = jnp.dot(k_ref[0, :, hs], qt_scr[hs, :], preferred_element_type=F32)
            cols = []
            for c in range(t // LANES):
                zc = z_t[:, c * LANES:(c + 1) * LANES] + cneg_ref[h]
                if causal:
                    zc = jnp.where(key_i <= qry_i + c * LANES, zc, NEG_BIG)
                cols.append(zc)
            _logits_pass(h, jnp.concatenate(cols, axis=1), z_scr, m_scr, mnew_scr)
        for h in range(n_heads):
            hs = slice(h * HEAD_DIM, (h + 1) * HEAD_DIM)
            _softmax_pass(h, vt_ref[0, 0, hs, :], z_scr, m_scr, mnew_scr, l_scr, acc_scr)

    @pl.when(ki < qi)
    def _():
        update(causal=False)

    @pl.when(ki == qi)
    def _():
        update(causal=True)
        for h in range(n_heads):
            hs = slice(h * HEAD_DIM, (h + 1) * HEAD_DIM)
            o = (acc_scr[hs, :] / l_scr[h:h + 1, :]).T
            o_ref[:, hs] = (o * gate_ref[0, :, hs].astype(F32)).astype(BF16)


def _fox_t(main, vt, cneg, *, t):
    n_seg, s_len, seg = main.shape
    assert vt.shape == (2, s_len // t, seg, t)
    nq = s_len // t
    pairs = [(a, b) for a in range(nq) for b in range(a + 1)]
    q_tab = jnp.asarray([a for a, _ in pairs], jnp.int32)
    k_tab = jnp.asarray([b for _, b in pairs], jnp.int32)
    kern = functools.partial(_fox_t_kernel, t=t, n_heads=N_HEADS_B)
    grid_spec = pltpu.PrefetchScalarGridSpec(
        num_scalar_prefetch=2,
        grid=(len(pairs),),
        in_specs=[
            pl.BlockSpec((1, t, seg), lambda p, qt, kt: (4, qt[p], 0)),
            pl.BlockSpec((1, t, seg), lambda p, qt, kt: (5, kt[p], 0)),
            pl.BlockSpec((1, 1, seg, t), lambda p, qt, kt: (1, kt[p], 0, 0)),
            pl.BlockSpec((1, t, seg), lambda p, qt, kt: (7, qt[p], 0)),
            pl.BlockSpec((N_HEADS_B, t, LANES), lambda p, qt, kt: (0, kt[p], 0)),
        ],
        out_specs=pl.BlockSpec((t, seg), lambda p, qt, kt: (qt[p], 0)),
        scratch_shapes=[
            pltpu.VMEM((seg, t), BF16),
            pltpu.VMEM((N_HEADS_B, t, t), F32),
            pltpu.VMEM((N_HEADS_B, t), F32),
            pltpu.VMEM((N_HEADS_B, t), F32),
            pltpu.VMEM((N_HEADS_B, t), F32),
            pltpu.VMEM((seg, t), F32),
        ],
    )
    return pl.pallas_call(
        kern,
        grid_spec=grid_spec,
        out_shape=jax.ShapeDtypeStruct((s_len, seg), BF16),
        compiler_params=_cparams(("arbitrary",)),
        name="fox",
    )(q_tab, k_tab, main, main, vt, main, cneg)


def _out_proj_kernel(x_ref, oa_ref, ob_ref, wa_ref, wb_ref, y_ref):
    y = x_ref[...] + jnp.dot(oa_ref[...], wa_ref[...], preferred_element_type=F32)
    y_ref[...] = y + jnp.dot(ob_ref[...], wb_ref[...], preferred_element_type=F32)


def _out_proj(x, o_a, o_b, w_a, w_b, *, tm):
    s_len, d = x.shape
    seg = o_a.shape[1]
    resident = functools.partial(pl.BlockSpec, pipeline_mode=pl.Buffered(1))
    return pl.pallas_call(
        _out_proj_kernel,
        grid=(s_len // tm,),
        in_specs=[
            pl.BlockSpec((tm, d), lambda i: (i, 0)),
            pl.BlockSpec((tm, seg), lambda i: (i, 0)),
            pl.BlockSpec((tm, seg), lambda i: (i, 0)),
            resident((seg, d), lambda i: (0, 0)),
            resident((seg, d), lambda i: (0, 0)),
        ],
        out_specs=pl.BlockSpec((tm, d), lambda i: (i, 0)),
        out_shape=jax.ShapeDtypeStruct((s_len, d), F32),
        compiler_params=_cparams(("arbitrary",)),
        name="out_proj",
    )(x, o_a, o_b, w_a, w_b)


def _swiglu_step(h, wg, wu, wd):
    g = jnp.dot(h, wg.astype(BF16), preferred_element_type=F32)
    u = jnp.dot(h, wu.astype(BF16), preferred_element_type=F32)
    a = (g * jax.nn.sigmoid(g) * u).astype(BF16)
    return jnp.dot(a, wd.astype(BF16), preferred_element_type=F32)


def _ffn_kernel(x_ref, g_ref, wg_ref, wu_ref, wd_ref, y_ref, h_scr):
    @pl.when(pl.program_id(1) == 0)
    def _():
        x = x_ref[...]
        h_scr[...] = _rms(x, g_ref[...]).astype(BF16)
        y_ref[...] = x

    y_ref[...] += _swiglu_step(h_scr[...], wg_ref[...], wu_ref[...], wd_ref[...])


def _ffn(x, g, wg, wu, wd, *, tm, tf):
    s_len, d = x.shape
    f = wg.shape[1]
    return pl.pallas_call(
        _ffn_kernel,
        grid=(s_len // tm, f // tf),
        in_specs=[
            pl.BlockSpec((tm, d), lambda i, j: (i, 0), pipeline_mode=pl.Buffered(1)),
            pl.BlockSpec((1, d), lambda i, j: (0, 0)),
            pl.BlockSpec((d, tf), lambda i, j: (0, j)),
            pl.BlockSpec((d, tf), lambda i, j: (0, j)),
            pl.BlockSpec((tf, d), lambda i, j: (j, 0)),
        ],
        out_specs=pl.BlockSpec((tm, d), lambda i, j: (i, 0)),
        out_shape=jax.ShapeDtypeStruct((s_len, d), F32),
        scratch_shapes=[pltpu.VMEM((tm, d), BF16)],
        compiler_params=_cparams(("arbitrary", "arbitrary")),
        name="ffn",
    )(x, g, wg, wu, wd)


def _pool_kernel(x_ref, g_ref, w_ref, sc_ref, y_ref, ext_scr, *, halo):
    i = pl.program_id(0)
    tm, d = x_ref.shape
    n_grp = len(POOL_WINDOWS)
    width = d // n_grp

    @pl.when(i == 0)
    def _():
        ext_scr[0:halo, :] = jnp.zeros((halo, d), F32)

    x = x_ref[...]
    ext_scr[halo:halo + tm, :] = _rms(x, g_ref[...])
    t_glob = i * tm + lax.broadcasted_iota(jnp.int32, (tm, 1), 0)
    for gi, win in enumerate(POOL_WINDOWS):
        cs = slice(gi * width, (gi + 1) * width)
        hn = ext_scr[halo:halo + tm, cs]
        tot = hn
        for dlt in range(1, win):
            tot = tot + ext_scr[halo - dlt:halo - dlt + tm, cs]
        count = jnp.minimum(t_glob + 1, win).astype(F32)
        pooled = (tot / count - hn).astype(BF16)
        y = jnp.dot(pooled, w_ref[gi], preferred_element_type=F32)
        y_ref[:, cs] = x[:, cs] + y * sc_ref[:, cs]
    ext_scr[0:halo, :] = ext_scr[tm:tm + halo, :]


def _pool(x, g, w_pool, scale, *, tm):
    s_len, d = x.shape
    halo = 16
    assert max(POOL_WINDOWS) <= halo
    n_grp, width, _ = w_pool.shape
    return pl.pallas_call(
        functools.partial(_pool_kernel, halo=halo),
        grid=(s_len // tm,),
        in_specs=[
            pl.BlockSpec((tm, d), lambda i: (i, 0)),
            pl.BlockSpec((1, d), lambda i: (0, 0)),
            pl.BlockSpec((n_grp, width, width), lambda i: (0, 0, 0)),
            pl.BlockSpec((1, d), lambda i: (0, 0)),
        ],
        out_specs=pl.BlockSpec((tm, d), lambda i: (i, 0)),
        out_shape=jax.ShapeDtypeStruct((s_len, d), F32),
        scratch_shapes=[pltpu.VMEM((tm + halo, d), F32)],
        compiler_params=_cparams(("arbitrary",)),
        name="pool",
    )(x, g, w_pool, scale)


R_E1, R_E2, R_G1, R_G2, R_K1, R_K2 = range(6)


def _router_kernel(x_ref, g_ref, r_ref, route_ref, cnt_ref, carry_scr, *, n_exp):
    i = pl.program_id(0)
    tm = x_ref.shape[0]

    @pl.when(i == 0)
    def _():
        carry_scr[...] = jnp.zeros_like(carry_scr)

    h = _rms(x_ref[...], g_ref[...])
    logits = jnp.dot(h, r_ref[...], preferred_element_type=F32, precision=lax.Precision.HIGHEST)
    lane = lax.broadcasted_iota(jnp.int32, logits.shape, 1)
    logits = jnp.where(lane < n_exp, logits, -jnp.inf)
    lane_f = lane.astype(F32)
    v1 = jnp.max(logits, axis=1, keepdims=True)
    e1 = jnp.min(jnp.where(logits == v1, lane_f, float(LANES)), axis=1, keepdims=True)
    rest = jnp.where(lane_f == e1, -jnp.inf, logits)
    v2 = jnp.max(rest, axis=1, keepdims=True)
    e2 = jnp.min(jnp.where(rest == v2, lane_f, float(LANES)), axis=1, keepdims=True)
    ex = jnp.exp(v2 - v1)
    g1 = 1.0 / (1.0 + ex)
    g2 = ex / (1.0 + ex)
    oh1 = (lane_f == e1).astype(F32)
    oh2 = (lane_f == e2).astype(F32)
    both = oh1 + oh2
    row = lax.broadcasted_iota(jnp.int32, (tm, tm), 0)
    col = lax.broadcasted_iota(jnp.int32, (tm, tm), 1)
    tri = (col < row).astype(BF16)
    before = jnp.dot(tri, both.astype(BF16), preferred_element_type=F32) + carry_scr[...]
    k1 = jnp.sum(before * oh1, axis=1, keepdims=True)
    k2 = jnp.sum(before * oh2, axis=1, keepdims=True)
    carry_scr[...] = carry_scr[...] + jnp.sum(both, axis=0, keepdims=True)
    cnt_ref[...] = carry_scr[...]
    vals = (e1, e2, g1, g2, k1, k2)
    out = jnp.zeros(logits.shape, F32)
    for li, v in enumerate(vals):
        out = jnp.where(lane == li, v, out)
    route_ref[...] = out


def _router(x, g, r_pad, *, tm, n_exp):
    s_len, d = x.shape
    return pl.pallas_call(
        functools.partial(_router_kernel, n_exp=n_exp),
        grid=(s_len // tm,),
        in_specs=[
            pl.BlockSpec((tm, d), lambda i: (i, 0)),
            pl.BlockSpec((1, d), lambda i: (0, 0)),
            pl.BlockSpec((d, LANES), lambda i: (0, 0)),
        ],
        out_specs=[
            pl.BlockSpec((tm, LANES), lambda i: (i, 0)),
            pl.BlockSpec((1, LANES), lambda i: (0, 0)),
        ],
        out_shape=[
            jax.ShapeDtypeStruct((s_len, LANES), F32),
            jax.ShapeDtypeStruct((1, LANES), F32),
        ],
        scratch_shapes=[pltpu.VMEM((1, LANES), F32)],
        compiler_params=_cparams(("arbitrary",)),
        name="router",
    )(x, g, r_pad)


def _pack_bf16_pairs(h):
    half = h.shape[1] // 2
    lo = lax.bitcast_convert_type(h[:, :half].astype(BF16).astype(F32), jnp.uint32)
    hi = lax.bitcast_convert_type(h[:, half:].astype(BF16).astype(F32), jnp.uint32)
    return (lo >> 16) | (hi & jnp.uint32(0xFFFF0000))


def _unpack_bf16_pairs(u):
    lo = lax.bitcast_convert_type(u << 16, F32).astype(BF16)
    hi = lax.bitcast_convert_type(u & jnp.uint32(0xFFFF0000), F32).astype(BF16)
    return lo, hi


ROW_COPY_UNROLL = 8


def _run_row_copies(copies, n):
    def start(t, c):
        a, b = copies(t)
        a.start(priority=0)
        b.start(priority=1)
        return c

    def wait(t, c):
        a, b = copies(t)
        a.wait()
        b.wait()
        return c

    lax.fori_loop(0, n, start, 0, unroll=ROW_COPY_UNROLL)
    lax.fori_loop(0, n, wait, 0, unroll=ROW_COPY_UNROLL)


def _scatter_kernel(p1_ref, p2_ref, x_ref, g_ref, xs_in_ref, xs_ref, row_scr, sem):
    del xs_in_ref
    i = pl.program_id(0)
    tm = x_ref.shape[0]
    row_scr[...] = _pack_bf16_pairs(_rms(x_ref[...], g_ref[...]))

    def copies(t):
        src = row_scr.at[pl.ds(t, 1), :]
        return (pltpu.make_async_copy(src, xs_ref.at[pl.ds(p1_ref[i * tm + t], 1), :], sem.at[0]),
                pltpu.make_async_copy(src, xs_ref.at[pl.ds(p2_ref[i * tm + t], 1), :], sem.at[1]))

    _run_row_copies(copies, tm)


def _scatter(pos1, pos2, x, g, xs_zero, *, tm):
    s_len, d = x.shape
    grid_spec = pltpu.PrefetchScalarGridSpec(
        num_scalar_prefetch=2,
        grid=(s_len // tm,),
        in_specs=[
            pl.BlockSpec((tm, d), lambda i, p1, p2: (i, 0)),
            pl.BlockSpec((1, d), lambda i, p1, p2: (0, 0)),
            pl.BlockSpec(memory_space=pl.ANY),
        ],
        out_specs=pl.BlockSpec(memory_space=pl.ANY),
        scratch_shapes=[
            pltpu.VMEM((tm, d // 2), jnp.uint32),
            pltpu.SemaphoreType.DMA((2,)),
        ],
    )
    return pl.pallas_call(
        _scatter_kernel,
        grid_spec=grid_spec,
        out_shape=jax.ShapeDtypeStruct(xs_zero.shape, jnp.uint32),
        input_output_aliases={4: 0},
        compiler_params=_cparams(("arbitrary",)),
        name="moe_scatter",
    )(pos1, pos2, x, g, xs_zero)


def _experts_kernel(e_ref, v_ref, xs_ref, wg_ref, wu_ref, wd_ref, o_ref, xb_scr):
    j = pl.program_id(0)
    f = pl.program_id(1)
    half = xs_ref.shape[1]

    @pl.when(f == 0)
    def _():
        lo, hi = _unpack_bf16_pairs(xs_ref[...])
        xb_scr[:, :half] = lo
        xb_scr[:, half:] = hi
        o_ref[...] = jnp.zeros_like(o_ref)

    @pl.when(v_ref[j] > 0)
    def _():
        o_ref[...] += _swiglu_step(xb_scr[...], wg_ref[0], wu_ref[0], wd_ref[0])


def _experts(item_e, item_valid, xs, wg, wu, wd, *, r, tf):
    rows, half = xs.shape
    n_exp, d, f = wg.shape
    n_items = rows // r
    n_f = f // tf

    def f_idx(fi, j, v):
        return jnp.where(v[j] > 0, fi, n_f - 1)

    grid_spec = pltpu.PrefetchScalarGridSpec(
        num_scalar_prefetch=2,
        grid=(n_items, n_f),
        in_specs=[
            pl.BlockSpec((r, half), lambda j, fi, e, v: (j, 0)),
            pl.BlockSpec((1, d, tf), lambda j, fi, e, v: (e[j], 0, f_idx(fi, j, v))),
            pl.BlockSpec((1, d, tf), lambda j, fi, e, v: (e[j], 0, f_idx(fi, j, v))),
            pl.BlockSpec((1, tf, d), lambda j, fi, e, v: (e[j], f_idx(fi, j, v), 0)),
        ],
        out_specs=pl.BlockSpec((r, d), lambda j, fi, e, v: (j, 0)),
        scratch_shapes=[pltpu.VMEM((r, d), BF16)],
    )
    return pl.pallas_call(
        _experts_kernel,
        grid_spec=grid_spec,
        out_shape=jax.ShapeDtypeStruct((rows, d), F32),
        compiler_params=_cparams(("arbitrary", "arbitrary")),
        name="moe_experts",
    )(item_e, item_valid, xs, wg, wu, wd)


def _combine_kernel(p1_ref, p2_ref, x_ref, route_ref, o_hbm, y_ref, a_scr, b_scr, sem):
    i = pl.program_id(0)
    tm = x_ref.shape[0]

    def copies(t):
        return (pltpu.make_async_copy(o_hbm.at[pl.ds(p1_ref[i * tm + t], 1), :],
                                      a_scr.at[pl.ds(t, 1), :], sem.at[0]),
                pltpu.make_async_copy(o_hbm.at[pl.ds(p2_ref[i * tm + t], 1), :],
                                      b_scr.at[pl.ds(t, 1), :], sem.at[1]))

    _run_row_copies(copies, tm)
    route = route_ref[...]
    g1 = route[:, R_G1:R_G1 + 1]
    g2 = route[:, R_G2:R_G2 + 1]
    y_ref[...] = x_ref[...] + g1 * a_scr[...] + g2 * b_scr[...]


def _combine(pos1, pos2, x, route, o_sorted, *, tm):
    s_len, d = x.shape
    grid_spec = pltpu.PrefetchScalarGridSpec(
        num_scalar_prefetch=2,
        grid=(s_len // tm,),
        in_specs=[
            pl.BlockSpec((tm, d), lambda i, p1, p2: (i, 0)),
            pl.BlockSpec((tm, LANES), lambda i, p1, p2: (i, 0)),
            pl.BlockSpec(memory_space=pl.ANY),
        ],
        out_specs=pl.BlockSpec((tm, d), lambda i, p1, p2: (i, 0)),
        scratch_shapes=[
            pltpu.VMEM((tm, d), F32),
            pltpu.VMEM((tm, d), F32),
            pltpu.SemaphoreType.DMA((2,)),
        ],
    )
    return pl.pallas_call(
        _combine_kernel,
        grid_spec=grid_spec,
        out_shape=jax.ShapeDtypeStruct((s_len, d), F32),
        compiler_params=_cparams(("arbitrary",)),
        name="moe_combine",
    )(pos1, pos2, x, route, o_sorted)


def _rotary_tables(s_len, rot_dim, period):
    half = rot_dim // 2
    inv_freq = 1.0 / (ROPE_THETA ** (jnp.arange(half, dtype=F32) * 2.0 / rot_dim))
    ang = jnp.arange(s_len, dtype=F32)[:, None] * inv_freq[None, :]
    cos, sin = jnp.cos(ang), jnp.sin(ang)
    pad = period - rot_dim
    cos_p = jnp.concatenate([cos, cos, jnp.ones((s_len, pad), F32)], axis=1)
    sin_p = jnp.concatenate([-sin, sin, jnp.zeros((s_len, pad), F32)], axis=1)
    reps = LANES // period
    return jnp.tile(cos_p, (1, reps)), jnp.tile(sin_p, (1, reps))


def _split_w_in(w_in):
    seg_a = N_HEADS_A * HEAD_DIM
    seg_i = N_IDX_HEADS * IDX_DIM
    seg_b = N_HEADS_B * HEAD_DIM
    sizes = (seg_a, seg_a, seg_a, seg_i, IDX_DIM, N_IDX_HEADS, seg_b, seg_b, seg_b, N_HEADS_B, seg_b)
    offs = np.cumsum(sizes)[:-1].tolist()
    return jnp.split(w_in, offs, axis=1)


def _moe_tables(counts, e1, e2, k1, k2, *, r, n_items):
    n_exp = counts.shape[0]
    blocks = (counts + r - 1) // r
    bend = jnp.cumsum(blocks)
    bstart = bend - blocks
    j = jnp.arange(n_items, dtype=jnp.int32)
    item_e = jnp.sum((j[:, None] >= bend[None, :]).astype(jnp.int32), axis=1)
    valid = item_e < n_exp
    last_e = jnp.minimum(item_e[jnp.maximum(bend[-1] - 1, 0)], n_exp - 1)
    item_e = jnp.where(valid, item_e, last_e).astype(jnp.int32)
    pos1 = bstart[e1] * r + k1
    pos2 = bstart[e2] * r + k2
    return item_e, valid.astype(jnp.int32), pos1.astype(jnp.int32), pos2.astype(jnp.int32)


def _moe_block_rows(n_assign, n_exp):
    r = -(-n_assign // (2 * n_exp))
    r = -(-(r + r // 16) // 64) * 64
    return max(r, 64)


def kernel(x, attn_norm_e, w_in_e, b_forget_e, q_norm_a_e, k_norm_a_e, q_norm_b_e, k_norm_b_e,
           w_out_e, ffn_norm_e, w_gate_e, w_up_e, w_down_e, pool_norm_o, w_pool_o, pool_scale_o,
           moe_norm_o, router_o, w_gate_o, w_up_o, w_down_o):
    b, s_len, d = x.shape
    assert b == 1
    x0 = x[0]
    seg = N_HEADS_A * HEAD_DIM
    assert seg == N_IDX_HEADS * IDX_DIM == N_HEADS_B * HEAD_DIM
    assert N_IDX_HEADS <= FB_OFF - WI_OFF and N_HEADS_B <= LANES - FB_OFF
    row = lambda v: v.reshape(1, -1)

    qa, ka, va, qi, ki, wi, qb, kb, vb, fb, gb = _split_w_in(w_in_e[0])
    w_main = jnp.concatenate([qa, ka, va, qi, qb, kb, vb, gb], axis=1).astype(BF16)
    w_small = jnp.concatenate(
        [ki, wi, jnp.zeros((d, FB_OFF - WI_OFF - N_IDX_HEADS), F32), fb,
         jnp.zeros((d, LANES - FB_OFF - N_HEADS_B), F32)], axis=1).astype(BF16)
    bf_pad = jnp.zeros((1, LANES), F32).at[0, FB_OFF:FB_OFF + N_HEADS_B].set(b_forget_e[0])
    scale = HEAD_DIM ** -0.5
    gains = jnp.stack([q_norm_a_e[0] * scale, k_norm_a_e[0], q_norm_b_e[0] * scale, k_norm_b_e[0]])
    tabs = _rotary_tables(s_len, ROT_DIM, HEAD_DIM) + _rotary_tables(s_len, IDX_ROT_DIM, IDX_DIM)
    t_attn = _tile(s_len, 512)
    main, small, v_t, c_neg, ki_rot = _in_proj(x0, row(attn_norm_e[0]), w_main, w_small, bf_pad, gains,
                                               tabs, n_seg=8, seg=seg, tm=t_attn)
    wi_t = small[:, WI_OFF:WI_OFF + N_IDX_HEADS].T

    top_k = min(TOPK_MAX, s_len // 4)
    o_a = _dsa_t(main, v_t, ki_rot, wi_t, tq=_tile(s_len, 256), tk=t_attn, top_k=top_k)
    o_b = _fox_t(main, v_t, c_neg, t=t_attn)
    w_out = w_out_e[0].astype(BF16)
    x1 = _out_proj(x0, o_a, o_b, w_out[:seg], w_out[seg:], tm=_tile(s_len, 512))

    x2 = _ffn(x1, row(ffn_norm_e[0]), w_gate_e[0], w_up_e[0], w_down_e[0],
              tm=_tile(s_len, 1024), tf=_tile(w_gate_e.shape[2], 256))

    x3 = _pool(x2, row(pool_norm_o[0]), w_pool_o[0].astype(BF16), row(pool_scale_o[0]),
               tm=_tile(s_len, 512))

    n_exp = router_o.shape[2]
    r_pad = jnp.zeros((d, LANES), F32).at[:, :n_exp].set(router_o[0])
    route, counts = _router(x3, row(moe_norm_o[0]), r_pad, tm=_tile(s_len, 512), n_exp=n_exp)
    r_rows = _moe_block_rows(2 * s_len, n_exp)
    n_items = (2 * s_len) // r_rows + n_exp
    as_int = lambda c: route[:, c].astype(jnp.int32)
    item_e, item_valid, pos1, pos2 = _moe_tables(
        counts[0, :n_exp].astype(jnp.int32), as_int(R_E1), as_int(R_E2), as_int(R_K1), as_int(R_K2),
        r=r_rows, n_items=n_items)
    xs_zero = jnp.zeros((n_items * r_rows, d // 2), jnp.uint32)
    xs = _scatter(pos1, pos2, x3, row(moe_norm_o[0]), xs_zero, tm=_tile(s_len, 256))
    o_sorted = _experts(item_e, item_valid, xs, w_gate_o[0], w_up_o[0], w_down_o[0],
                        r=r_rows, tf=_tile(w_gate_o.shape[3], 256))
    y = _combine(pos1, pos2, x3, route, o_sorted, tm=_tile(s_len, 256))
    return y[None]
```

```python
import functools

import jax
import jax.numpy as jnp
import numpy as np
from jax import lax
from jax.experimental import pallas as pl
from jax.experimental.pallas import tpu as pltpu

HEAD_DIM = 128
N_HEADS_A = 8
N_HEADS_B = 8
N_IDX_HEADS = 16
IDX_DIM = 64
ROT_DIM = HEAD_DIM // 4
IDX_ROT_DIM = IDX_DIM // 4
ROPE_THETA = 500000.0
CHUNK = 64
TOPK_MAX = 256
POOL_WINDOWS = (2, 4, 8, 16)
RMS_EPS = 1e-6

LANES = 128
MXU_COLS = 256
VMEM_LIMIT_BYTES = 56 * 1024 * 1024

KI_OFF = 0
WI_OFF = IDX_DIM
FB_OFF = 80

LOG2_E = 1.4426950408889634
NEG_BIG = -1e30
INT_MIN = -(2 ** 31)
INT_MAX = 2 ** 31 - 1
KEY_NONE = INT_MIN + 1
HALF_MIN = -(2 ** 15)
HALF_MAX = 2 ** 15 - 1

F32 = jnp.float32
BF16 = jnp.bfloat16


def _cparams(sem):
    return pltpu.CompilerParams(dimension_semantics=sem, vmem_limit_bytes=VMEM_LIMIT_BYTES)


def _tile(n, pref):
    t = min(n, pref)
    while n % t:
        t //= 2
    return t


def _rms(x, g):
    ms = jnp.mean(x * x, axis=-1, keepdims=True)
    return x * lax.rsqrt(ms + RMS_EPS) * g


def _nt_dot(a, b):
    return lax.dot_general(a, b, (((1,), (1,)), ((), ())), preferred_element_type=F32)


def _rotate(y, cos_t, sin_t, half, period):
    lane = lax.broadcasted_iota(jnp.int32, y.shape, 1)
    up = pltpu.roll(y, LANES - half, 1)
    dn = pltpu.roll(y, half, 1)
    rot = jnp.where((lane & (period - 1)) < half, up, dn)
    return y * cos_t + rot * sin_t


def _in_proj_kernel(x_ref, g_ref, wm_ref, ws_ref, bf_ref, gains_ref, cosa_ref, sina_ref,
                    cosi_ref, sini_ref, main_ref, small_ref, vt_ref, cneg_ref, ki_ref, h_scr,
                    carry_scr, *, n_seg, idx_scale, n_fb):
    i = pl.program_id(0)
    j = pl.program_id(1)
    tm = h_scr.shape[0]
    seg = wm_ref.shape[1]

    @pl.when(j == 0)
    def _():
        h_scr[...] = _rms(x_ref[...], g_ref[...]).astype(BF16)

    def head_norm(slab, gain):
        ms = jnp.mean(slab * slab, axis=-1, keepdims=True)
        return slab * lax.rsqrt(ms + RMS_EPS) * gain

    def project(epilogue):
        for c0 in range(0, seg, MXU_COLS):
            acc = jnp.dot(h_scr[...], wm_ref[:, c0:c0 + MXU_COLS], preferred_element_type=F32)
            epilogue(acc, c0)

    def per_slab(fn):
        def epilogue(acc, c0):
            for s0 in range(0, MXU_COLS, LANES):
                y = fn(acc[:, s0:s0 + LANES])
                main_ref[0, :, c0 + s0:c0 + s0 + LANES] = y.astype(BF16)
        return epilogue

    @pl.when(j < 2)
    def _():
        gain = gains_ref[pl.ds(j, 1), :]
        project(per_slab(lambda y: _rotate(head_norm(y, gain), cosa_ref[...], sina_ref[...],
                                           ROT_DIM // 2, LANES)))

    @pl.when((j == 4) | (j == 5))
    def _():
        gain = gains_ref[pl.ds(j - 2, 1), :]
        project(per_slab(lambda y: head_norm(y, gain)))

    @pl.when((j == 2) | (j == 6))
    def _():
        def epilogue(acc, c0):
            main_ref[0, :, c0:c0 + MXU_COLS] = acc.astype(BF16)
            vt_ref[0, 0, c0:c0 + MXU_COLS, :] = acc.T.astype(BF16)
        project(epilogue)

    @pl.when(j == 3)
    def _():
        project(per_slab(lambda y: _rotate(y, cosi_ref[...], sini_ref[...], IDX_ROT_DIM // 2, IDX_DIM)))

    @pl.when(j == 7)
    def _():
        project(per_slab(jax.nn.sigmoid))

    @pl.when(j == n_seg)
    def _():
        acc = jnp.dot(h_scr[...], ws_ref[...], preferred_element_type=F32)
        lane = lax.broadcasted_iota(jnp.int32, acc.shape, 1)
        is_ki = lane < WI_OFF
        cos_k = jnp.where(is_ki, cosi_ref[...], 1.0)
        sin_k = jnp.where(is_ki, sini_ref[...], 0.0)
        ki = _rotate(acc, cos_k, sin_k, IDX_ROT_DIM // 2, IDX_DIM)
        wi = acc * idx_scale
        z = acc + bf_ref[...]
        ls = jnp.minimum(z, 0.0) - jnp.log1p(jnp.exp(-jnp.abs(z)))
        is_fb = (lane >= FB_OFF) & (lane < FB_OFF + n_fb)
        ls = jnp.where(is_fb, ls, 0.0)
        row = lax.broadcasted_iota(jnp.int32, (tm, tm), 0)
        col = lax.broadcasted_iota(jnp.int32, (tm, tm), 1)
        tri = (col <= row).astype(BF16)
        p1 = ls.astype(BF16)
        r1 = ls - p1.astype(F32)
        p2 = r1.astype(BF16)
        p3 = (r1 - p2.astype(F32)).astype(BF16)
        c = (jnp.dot(tri, p1, preferred_element_type=F32)
             + jnp.dot(tri, p2, preferred_element_type=F32)
             + jnp.dot(tri, p3, preferred_element_type=F32))

        @pl.when(i == 0)
        def _():
            carry_scr[...] = jnp.zeros_like(carry_scr)

        c = c + carry_scr[...]
        carry_scr[...] = c[tm - 1:tm, :]
        small_ref[...] = jnp.where(is_ki, ki, jnp.where(lane < FB_OFF, wi, c))
        ki_ref[...] = jnp.where(is_ki, ki, 0.0).astype(BF16)
        for hb in range(n_fb):
            cneg_ref[hb] = jnp.broadcast_to(-LOG2_E * c[:, FB_OFF + hb:FB_OFF + hb + 1], (tm, LANES))


def _in_proj(x, g, w_main, w_small, bf_pad, gains, tabs, *, n_seg, seg, tm):
    s_len, d = x.shape
    cosa, sina, cosi, sini = tabs
    grid = (s_len // tm, n_seg + 1)
    row_spec = lambda w: pl.BlockSpec((tm, w), lambda i, j: (i, 0))
    kern = functools.partial(_in_proj_kernel, n_seg=n_seg,
                             idx_scale=float((N_IDX_HEADS * IDX_DIM) ** -0.5), n_fb=N_HEADS_B)
    return pl.pallas_call(
        kern,
        grid=grid,
        in_specs=[
            pl.BlockSpec((tm, d), lambda i, j: (i, 0), pipeline_mode=pl.Buffered(1)),
            pl.BlockSpec((1, d), lambda i, j: (0, 0)),
            pl.BlockSpec((d, seg), lambda i, j: (0, jnp.minimum(j, n_seg - 1))),
            pl.BlockSpec((d, LANES), lambda i, j: (0, 0)),
            pl.BlockSpec((1, LANES), lambda i, j: (0, 0)),
            pl.BlockSpec((4, LANES), lambda i, j: (0, 0)),
            row_spec(LANES), row_spec(LANES), row_spec(LANES), row_spec(LANES),
        ],
        out_specs=[
            pl.BlockSpec((1, tm, seg), lambda i, j: (jnp.minimum(j, n_seg - 1), i, 0)),
            pl.BlockSpec((tm, LANES), lambda i, j: (i, 0)),
            pl.BlockSpec((1, 1, seg, tm), lambda i, j: (jnp.where(j <= 2, 0, 1), i, 0, 0)),
            pl.BlockSpec((N_HEADS_B, tm, LANES), lambda i, j: (0, i, 0)),
            pl.BlockSpec((tm, LANES), lambda i, j: (i, 0)),
        ],
        out_shape=[
            jax.ShapeDtypeStruct((n_seg, s_len, seg), BF16),
            jax.ShapeDtypeStruct((s_len, LANES), F32),
            jax.ShapeDtypeStruct((2, s_len // tm, seg, tm), BF16),
            jax.ShapeDtypeStruct((N_HEADS_B, s_len, LANES), F32),
            jax.ShapeDtypeStruct((s_len, LANES), BF16),
        ],
        scratch_shapes=[
            pltpu.VMEM((tm, d), BF16),
            pltpu.VMEM((1, LANES), F32),
        ],
        compiler_params=_cparams(("arbitrary", "arbitrary")),
        name="in_proj",
    )(x, g, w_main, w_small, bf_pad, gains, cosa, sina, cosi, sini)


def _dsa_kernel(qa_ref, qi_ref, wi_ref, ki_ref, ka_ref, va_ref, o_ref, key_scr,
                *, tq, tk, top_k, n_heads, n_idx):
    i = pl.program_id(0)
    n_kt = ((i + 1) * tq + tk - 1) // tk
    t_glob = i * tq + lax.broadcasted_iota(jnp.int32, (tq, 1), 0)
    adm_len = (t_glob // CHUNK + 1) * CHUNK

    wi = wi_ref[...]

    def score_tile(kt, carry):
        k0 = pl.multiple_of(kt * tk, tk)
        kt_i = ki_ref[pl.ds(k0, tk), :]
        acc = jnp.zeros((tq, tk), F32)
        for h in range(n_idx):
            qh = qi_ref[0, :, h * IDX_DIM:(h + 1) * IDX_DIM]
            rel = jnp.maximum(_nt_dot(qh, kt_i), 0.0)
            acc = acc + rel * wi[:, h:h + 1]
        bits = lax.bitcast_convert_type(acc, jnp.int32)
        key = jnp.where(bits < 0, bits ^ INT_MAX, bits)
        kpos = k0 + lax.broadcasted_iota(jnp.int32, (1, tk), 1)
        key_scr[kt] = jnp.where(kpos < adm_len, key, KEY_NONE)
        return carry

    lax.fori_loop(0, n_kt, score_tile, 0)

    lane_pos = lax.broadcasted_iota(jnp.int32, (1, LANES), 1)

    def count(pred):
        def body(kt, cnt):
            keys = key_scr[kt]
            for c in range(tk // LANES):
                hit = pred(keys[:, c * LANES:(c + 1) * LANES], kt * tk + c * LANES + lane_pos)
                cnt = cnt + jnp.where(hit, 1, 0)
            return cnt

        cnt = lax.fori_loop(0, n_kt, body, jnp.zeros((tq, LANES), jnp.int32))
        return jnp.sum(cnt, axis=1, keepdims=True)

    def bisect(_, state):
        lo, hi, c_lo, c_hi = state
        mid = (lo >> 1) + (hi >> 1) + (lo & hi & 1)
        mid_b = jnp.broadcast_to(mid, (tq, LANES))
        c_mid = count(lambda k, _: k >= mid_b)
        take = c_mid >= top_k
        return (jnp.where(take, mid, lo), jnp.where(take, hi, mid),
                jnp.where(take, c_mid, c_lo), jnp.where(take, c_hi, c_mid))

    col = lambda v: jnp.full((tq, 1), v, jnp.int32)
    thr, _, c_ge, c_gt = lax.fori_loop(0, 32, bisect,
                                       (col(INT_MIN), col(INT_MAX), n_kt * tk + col(0), col(0)))
    short = thr <= KEY_NONE
    thr = jnp.maximum(thr, KEY_NONE)
    need = top_k - c_gt
    thr_b = jnp.broadcast_to(thr, (tq, LANES))

    def tie_cut():
        def step(_, lohi):
            lo, hi = lohi
            mid = (lo + hi) >> 1
            mid_b = jnp.broadcast_to(mid, (tq, LANES))
            ok = count(lambda k, pos: (k == thr_b) & (pos <= mid_b)) >= need
            return jnp.where(ok, lo, mid), jnp.where(ok, mid, hi)

        n_steps = int(np.ceil(np.log2(key_scr.shape[0] * tk))) + 1
        return lax.fori_loop(0, n_steps, step, (col(-1), n_kt * tk - 1 + col(0)))[1]

    has_tie = jnp.max(jnp.where((c_ge > top_k) & ~short, 1.0, 0.0)) > 0.5
    cut = lax.cond(has_tie, tie_cut, lambda: col(INT_MAX))
    cut = jnp.where(short, -1, cut)

    def bias_tile(kt, carry):
        keys = key_scr[kt]
        kpos = kt * tk + lax.broadcasted_iota(jnp.int32, (1, tk), 1)
        sel = (keys > thr) | ((keys == thr) & (kpos <= cut))
        key_scr[kt] = lax.bitcast_convert_type(jnp.where(sel, 0.0, NEG_BIG).astype(F32), jnp.int32)
        return carry

    lax.fori_loop(0, n_kt, bias_tile, 0)

    for h in range(n_heads):
        hs = slice(h * HEAD_DIM, (h + 1) * HEAD_DIM)
        qh = qa_ref[0, :, hs]

        def attend(kt, mla):
            m, l, acc = mla
            k0 = pl.multiple_of(kt * tk, tk)
            kh = ka_ref[0, pl.ds(k0, tk), hs]
            vh = va_ref[0, pl.ds(k0, tk), hs]
            z = _nt_dot(qh, kh) + lax.bitcast_convert_type(key_scr[kt], F32)
            m_new = jnp.maximum(m, jnp.max(z, axis=1, keepdims=True))
            alpha = jnp.exp(m - m_new)
            p = jnp.exp(z - m_new)
            l = alpha * l + jnp.sum(p, axis=1, keepdims=True)
            acc = alpha * acc + jnp.dot(p.astype(BF16), vh, preferred_element_type=F32)
            return m_new, l, acc

        m0 = jnp.full((tq, 1), NEG_BIG, F32)
        l0 = jnp.zeros((tq, 1), F32)
        a0 = jnp.zeros((tq, HEAD_DIM), F32)
        m, l, acc = lax.fori_loop(0, n_kt, attend, (m0, l0, a0))
        o_ref[:, hs] = (acc / l).astype(BF16)


def _dsa(main, ki, wi, *, tq, tk, top_k):
    n_seg, s_len, seg = main.shape
    n_kt_max = s_len // tk
    kern = functools.partial(_dsa_kernel, tq=tq, tk=tk, top_k=top_k, n_heads=N_HEADS_A,
                             n_idx=N_IDX_HEADS)
    resident = functools.partial(pl.BlockSpec, pipeline_mode=pl.Buffered(1))
    return pl.pallas_call(
        kern,
        grid=(s_len // tq,),
        in_specs=[
            pl.BlockSpec((1, tq, seg), lambda i: (0, i, 0)),
            pl.BlockSpec((1, tq, seg), lambda i: (3, i, 0)),
            pl.BlockSpec((tq, N_IDX_HEADS), lambda i: (i, 0)),
            resident((s_len, IDX_DIM), lambda i: (0, 0)),
            resident((1, s_len, seg), lambda i: (1, 0, 0)),
            resident((1, s_len, seg), lambda i: (2, 0, 0)),
        ],
        out_specs=pl.BlockSpec((tq, seg), lambda i: (i, 0)),
        out_shape=jax.ShapeDtypeStruct((s_len, seg), BF16),
        scratch_shapes=[pltpu.VMEM((n_kt_max, tq, tk), jnp.int32)],
        compiler_params=_cparams(("arbitrary",)),
        name="dsa",
    )(main, main, wi, ki, main, main)


def _fox_kernel(qt_ref, kt_ref, q_ref, k_ref, v_ref, gate_ref, ctq_ref, ctk_ref, o_ref,
                m_scr, l_scr, acc_scr, *, t, n_heads):
    p = pl.program_id(0)
    qi = qt_ref[p]
    ki = kt_ref[p]

    @pl.when(ki == 0)
    def _():
        m_scr[...] = jnp.full_like(m_scr, NEG_BIG)
        l_scr[...] = jnp.zeros_like(l_scr)
        acc_scr[...] = jnp.zeros_like(acc_scr)

    row = lax.broadcasted_iota(jnp.int32, (t, t), 0)
    col = lax.broadcasted_iota(jnp.int32, (t, t), 1)
    visible = (col <= row) | (ki < qi)

    for h in range(n_heads):
        hs = slice(h * HEAD_DIM, (h + 1) * HEAD_DIM)
        bias = ctq_ref[h:h + 1, 0:1] - ctk_ref[h:h + 1, :]
        z = _nt_dot(q_ref[0, :, hs], k_ref[0, :, hs]) + bias
        z = jnp.where(visible, z, NEG_BIG)
        m_old = m_scr[h]
        m_new = jnp.maximum(m_old, jnp.max(z, axis=1, keepdims=True))
        alpha = jnp.exp(m_old - m_new)
        pr = jnp.exp(z - m_new)
        l_scr[h] = alpha * l_scr[h] + jnp.sum(pr, axis=1, keepdims=True)
        acc_scr[:, hs] = alpha * acc_scr[:, hs] + jnp.dot(pr.astype(BF16), v_ref[0, :, hs],
                                                           preferred_element_type=F32)
        m_scr[h] = m_new

    @pl.when(ki == qi)
    def _():
        for h in range(n_heads):
            hs = slice(h * HEAD_DIM, (h + 1) * HEAD_DIM)
            o = acc_scr[:, hs] / l_scr[h]
            o_ref[:, hs] = (o * gate_ref[0, :, hs].astype(F32)).astype(BF16)


def _fox(main, c_t, *, t):
    n_seg, s_len, seg = main.shape
    nq = s_len // t
    pairs = [(a, b) for a in range(nq) for b in range(a + 1)]
    q_tab = jnp.asarray([a for a, _ in pairs], jnp.int32)
    k_tab = jnp.asarray([b for _, b in pairs], jnp.int32)
    n_hb = c_t.shape[0]
    kern = functools.partial(_fox_kernel, t=t, n_heads=N_HEADS_B)
    grid_spec = pltpu.PrefetchScalarGridSpec(
        num_scalar_prefetch=2,
        grid=(len(pairs),),
        in_specs=[
            pl.BlockSpec((1, t, seg), lambda p, qt, kt: (4, qt[p], 0)),
            pl.BlockSpec((1, t, seg), lambda p, qt, kt: (5, kt[p], 0)),
            pl.BlockSpec((1, t, seg), lambda p, qt, kt: (6, kt[p], 0)),
            pl.BlockSpec((1, t, seg), lambda p, qt, kt: (7, qt[p], 0)),
            pl.BlockSpec((n_hb, t), lambda p, qt, kt: (0, qt[p])),
            pl.BlockSpec((n_hb, t), lambda p, qt, kt: (0, kt[p])),
        ],
        out_specs=pl.BlockSpec((t, seg), lambda p, qt, kt: (qt[p], 0)),
        scratch_shapes=[
            pltpu.VMEM((N_HEADS_B, t, 1), F32),
            pltpu.VMEM((N_HEADS_B, t, 1), F32),
            pltpu.VMEM((t, seg), F32),
        ],
    )
    return pl.pallas_call(
        kern,
        grid_spec=grid_spec,
        out_shape=jax.ShapeDtypeStruct((s_len, seg), BF16),
        compiler_params=_cparams(("arbitrary",)),
        name="fox",
    )(q_tab, k_tab, main, main, main, main, c_t, c_t)


def _transpose_heads(src, dst_scr):
    for s in range(src.shape[1] // LANES):
        sl = slice(s * LANES, (s + 1) * LANES)
        dst_scr[sl, :] = src[:, sl].astype(F32).T.astype(BF16)


def _logits_pass(h, z_t, z_scr, m_scr, mnew_scr):
    z_scr[h] = z_t
    mnew_scr[h:h + 1, :] = jnp.maximum(m_scr[h:h + 1, :], jnp.max(z_t, axis=0, keepdims=True))


def _softmax_pass(h, v_t, z_scr, m_scr, mnew_scr, l_scr, acc_scr):
    hs = slice(h * HEAD_DIM, (h + 1) * HEAD_DIM)
    m_new = mnew_scr[h:h + 1, :]
    alpha = jnp.exp2(m_scr[h:h + 1, :] - m_new)
    p_t = jnp.exp2(z_scr[h] - m_new)
    l_scr[h:h + 1, :] = alpha * l_scr[h:h + 1, :] + jnp.sum(p_t, axis=0, keepdims=True)
    acc_scr[hs, :] = alpha * acc_scr[hs, :] + jnp.dot(v_t, p_t.astype(BF16),
                                                      preferred_element_type=F32)
    m_scr[h:h + 1, :] = m_new


def _init_attn_state(m_scr, l_scr, acc_scr):
    m_scr[...] = jnp.full_like(m_scr, NEG_BIG)
    l_scr[...] = jnp.zeros_like(l_scr)
    acc_scr[...] = jnp.zeros_like(acc_scr)


def _dsa_t_kernel(qa_ref, qi_ref, wit_ref, ki_ref, ka_ref, vat_ref, o_ref,
                  key_scr, half_scr, qat_scr, qit_scr, z_scr, m_scr, mnew_scr, l_scr, acc_scr,
                  *, tq, tk, top_k, n_heads, n_idx):
    i = pl.program_id(0)
    n_kt = ((i + 1) * tq + tk - 1) // tk
    t_glob = i * tq + lax.broadcasted_iota(jnp.int32, (1, tq), 1)
    adm_len = (t_glob // CHUNK + 1) * CHUNK
    key_pos = lax.broadcasted_iota(jnp.int32, (tk, 1), 0)

    _transpose_heads(qa_ref[0], qat_scr)
    _transpose_heads(qi_ref[0], qit_scr)

    def score_tile(kt, carry):
        k0 = pl.multiple_of(kt * tk, tk)
        k_idx = ki_ref[pl.ds(k0, tk), 0:IDX_DIM]
        acc = jnp.zeros((tk, tq), F32)
        for h in range(n_idx):
            q_t = qit_scr[h * IDX_DIM:(h + 1) * IDX_DIM, :]
            rel = jnp.maximum(jnp.dot(k_idx, q_t, preferred_element_type=F32), 0.0)
            acc = acc + rel * wit_ref[h:h + 1, :]
        bits = lax.bitcast_convert_type(acc, jnp.int32)
        key = jnp.where(bits < 0, bits ^ INT_MAX, bits)
        key = jnp.where(k0 + key_pos < adm_len, key, KEY_NONE)
        key_scr[kt] = key
        half_scr[kt] = (key >> 16).astype(jnp.int16)
        return carry

    lax.fori_loop(0, n_kt, score_tile, 0)

    def count(pred):
        def body(kt, cnt):
            hit = jnp.where(pred(key_scr[kt], kt * tk + key_pos), 1, 0)
            return cnt + jnp.sum(hit.reshape(tk // 8, 8, tq), axis=0)

        cnt = lax.fori_loop(0, n_kt, body, jnp.zeros((8, tq), jnp.int32))
        return jnp.sum(cnt, axis=0, keepdims=True)

    def count_half_ge(mid):
        mid16 = mid.astype(jnp.int16)

        def body(kt, cnt):
            hit = jnp.where(half_scr[kt] >= mid16, jnp.int16(1), jnp.int16(0))
            parts = [hit[r * 16:(r + 1) * 16] for r in range(tk // 16)]
            while len(parts) > 1:
                parts = [parts[a] + parts[a + 1] for a in range(0, len(parts), 2)]
            return cnt + parts[0].astype(jnp.int32)

        cnt = lax.fori_loop(0, n_kt, body, jnp.zeros((16, tq), jnp.int32))
        return jnp.sum(cnt, axis=0, keepdims=True)

    def bisect_half(need_k, c_all):
        def step(_, state):
            lo, hi, c_lo, c_hi = state
            mid = (lo + hi) >> 1
            c_mid = count_half_ge(mid)
            take = c_mid >= need_k
            return (jnp.where(take, mid, lo), jnp.where(take, hi, mid),
                    jnp.where(take, c_mid, c_lo), jnp.where(take, c_hi, c_mid))

        x, _, c_x, c_above = lax.fori_loop(0, 16, step,
                                           (row(HALF_MIN), row(HALF_MAX + 1), c_all, row(0)))
        return x, c_x, c_above

    row = lambda v: jnp.full((1, tq), v, jnp.int32)
    thr_hi, c_ge_hi, c_gt_hi = bisect_half(top_k, n_kt * tk + row(0))
    thr_hi16 = thr_hi.astype(jnp.int16)

    def low_half_tile(kt, carry):
        low = ((key_scr[kt] & 0xFFFF) + HALF_MIN).astype(jnp.int16)
        half_scr[kt] = jnp.where(half_scr[kt] == thr_hi16, low, jnp.int16(HALF_MIN))
        return carry

    lax.fori_loop(0, n_kt, low_half_tile, 0)
    thr_lo, c_ge_lo, c_gt_lo = bisect_half(top_k - c_gt_hi, c_ge_hi - c_gt_hi)
    thr = thr_hi * 65536 + (thr_lo - HALF_MIN)
    c_ge = c_gt_hi + c_ge_lo
    c_gt = c_gt_hi + c_gt_lo
    short = thr <= KEY_NONE
    thr = jnp.maximum(thr, KEY_NONE)
    need = top_k - c_gt

    def tie_cut():
        def step(_, lohi):
            lo, hi = lohi
            mid = (lo + hi) >> 1
            ok = count(lambda k, pos: (k == thr) & (pos <= mid)) >= need
            return jnp.where(ok, lo, mid), jnp.where(ok, mid, hi)

        n_steps = int(np.ceil(np.log2(key_scr.shape[0] * tk))) + 1
        return lax.fori_loop(0, n_steps, step, (row(-1), n_kt * tk - 1 + row(0)))[1]

    has_tie = jnp.max(jnp.where((c_ge > top_k) & ~short, 1.0, 0.0)) > 0.5
    cut = lax.cond(has_tie, tie_cut, lambda: row(INT_MAX))
    cut = jnp.where(short, -1, cut)

    _init_attn_state(m_scr, l_scr, acc_scr)

    def attend(kt, carry):
        k0 = pl.multiple_of(kt * tk, tk)
        keys = key_scr[kt]
        sel = (keys > thr) | ((keys == thr) & (kt * tk + key_pos <= cut))
        bias = jnp.where(sel, 0.0, NEG_BIG).astype(F32)
        for h in range(n_heads):
            hs = slice(h * HEAD_DIM, (h + 1) * HEAD_DIM)
            z_t = jnp.dot(ka_ref[0, pl.ds(k0, tk), hs], qat_scr[hs, :],
                          preferred_element_type=F32) + bias
            _logits_pass(h, z_t, z_scr, m_scr, mnew_scr)
        for h in range(n_heads):
            hs = slice(h * HEAD_DIM, (h + 1) * HEAD_DIM)
            _softmax_pass(h, vat_ref[0, kt, hs, :], z_scr, m_scr, mnew_scr, l_scr, acc_scr)
        return carry

    lax.fori_loop(0, n_kt, attend, 0)
    for h in range(n_heads):
        hs = slice(h * HEAD_DIM, (h + 1) * HEAD_DIM)
        o_ref[:, hs] = (acc_scr[hs, :] / l_scr[h:h + 1, :]).T.astype(BF16)


def _dsa_t(main, vt, ki, wi_t, *, tq, tk, top_k):
    n_seg, s_len, seg = main.shape
    assert vt.shape == (2, s_len // tk, seg, tk)
    kern = functools.partial(_dsa_t_kernel, tq=tq, tk=tk, top_k=top_k, n_heads=N_HEADS_A,
                             n_idx=N_IDX_HEADS)
    resident = functools.partial(pl.BlockSpec, pipeline_mode=pl.Buffered(1))
    return pl.pallas_call(
        kern,
        grid=(s_len // tq,),
        in_specs=[
            pl.BlockSpec((1, tq, seg), lambda i: (0, i, 0)),
            pl.BlockSpec((1, tq, seg), lambda i: (3, i, 0)),
            pl.BlockSpec((N_IDX_HEADS, tq), lambda i: (0, i)),
            resident((s_len, LANES), lambda i: (0, 0)),
            resident((1, s_len, seg), lambda i: (1, 0, 0)),
            resident((1, s_len // tk, seg, tk), lambda i: (0, 0, 0, 0)),
        ],
        out_specs=pl.BlockSpec((tq, seg), lambda i: (i, 0)),
        out_shape=jax.ShapeDtypeStruct((s_len, seg), BF16),
        scratch_shapes=[
            pltpu.VMEM((s_len // tk, tk, tq), jnp.int32),
            pltpu.VMEM((s_len // tk, tk, tq), jnp.int16),
            pltpu.VMEM((seg, tq), BF16),
            pltpu.VMEM((seg, tq), BF16),
            pltpu.VMEM((N_HEADS_A, tk, tq), F32),
            pltpu.VMEM((N_HEADS_A, tq), F32),
            pltpu.VMEM((N_HEADS_A, tq), F32),
            pltpu.VMEM((N_HEADS_A, tq), F32),
            pltpu.VMEM((seg, tq), F32),
        ],
        compiler_params=_cparams(("arbitrary",)),
        name="dsa",
    )(main, main, wi_t, ki, main, vt)


def _fox_t_kernel(qt_ref, kt_ref, q_ref, k_ref, vt_ref, gate_ref, cneg_ref, o_ref,
                  qt_scr, z_scr, m_scr, mnew_scr, l_scr, acc_scr, *, t, n_heads):
    p = pl.program_id(0)
    qi = qt_ref[p]
    ki = kt_ref[p]

    @pl.when(ki == 0)
    def _():
        _transpose_heads(q_ref[0], qt_scr)
        _init_attn_state(m_scr, l_scr, acc_scr)

    def update(causal):
        key_i = lax.broadcasted_iota(jnp.int32, (t, LANES), 0)
        qry_i = lax.broadcasted_iota(jnp.int32, (t, LANES), 1)
        for h in range(n_heads):
            hs = slice(h * HEAD_DIM, (h + 1) * HEAD_DIM)
            z_t = jnp.dot(k_ref[0, :, hs], qt_scr[hs, :], preferred_element_type=F32)
            cols = []
            for c in range(t // LANES):
                zc = z_t[:, c * LANES:(c + 1) * LANES] + cneg_ref[h]
                if causal:
                    zc = jnp.where(key_i <= qry_i + c * LANES, zc, NEG_BIG)
                cols.append(zc)
            _logits_pass(h, jnp.concatenate(cols, axis=1), z_scr, m_scr, mnew_scr)
        for h in range(n_heads):
            hs = slice(h * HEAD_DIM, (h + 1) * HEAD_DIM)
            _softmax_pass(h, vt_ref[0, 0, hs, :], z_scr, m_scr, mnew_scr, l_scr, acc_scr)

    @pl.when(ki < qi)
    def _():
        update(causal=False)

    @pl.when(ki == qi)
    def _():
        update(causal=True)
        for h in range(n_heads):
            hs = slice(h * HEAD_DIM, (h + 1) * HEAD_DIM)
            o = (acc_scr[hs, :] / l_scr[h:h + 1, :]).T
            o_ref[:, hs] = (o * gate_ref[0, :, hs].astype(F32)).astype(BF16)


def _fox_t(main, vt, cneg, *, t):
    n_seg, s_len, seg = main.shape
    assert vt.shape == (2, s_len // t, seg, t)
    nq = s_len // t
    pairs = [(a, b) for a in range(nq) for b in range(a + 1)]
    q_tab = jnp.asarray([a for a, _ in pairs], jnp.int32)
    k_tab = jnp.asarray([b for _, b in pairs], jnp.int32)
    kern = functools.partial(_fox_t_kernel, t=t, n_heads=N_HEADS_B)
    grid_spec = pltpu.PrefetchScalarGridSpec(
        num_scalar_prefetch=2,
        grid=(len(pairs),),
        in_specs=[
            pl.BlockSpec((1, t, seg), lambda p, qt, kt: (4, qt[p], 0)),
            pl.BlockSpec((1, t, seg), lambda p, qt, kt: (5, kt[p], 0)),
            pl.BlockSpec((1, 1, seg, t), lambda p, qt, kt: (1, kt[p], 0, 0)),
            pl.BlockSpec((1, t, seg), lambda p, qt, kt: (7, qt[p], 0)),
            pl.BlockSpec((N_HEADS_B, t, LANES), lambda p, qt, kt: (0, kt[p], 0)),
        ],
        out_specs=pl.BlockSpec((t, seg), lambda p, qt, kt: (qt[p], 0)),
        scratch_shapes=[
            pltpu.VMEM((seg, t), BF16),
            pltpu.VMEM((N_HEADS_B, t, t), F32),
            pltpu.VMEM((N_HEADS_B, t), F32),
            pltpu.VMEM((N_HEADS_B, t), F32),
            pltpu.VMEM((N_HEADS_B, t), F32),
            pltpu.VMEM((seg, t), F32),
        ],
    )
    return pl.pallas_call(
        kern,
        grid_spec=grid_spec,
        out_shape=jax.ShapeDtypeStruct((s_len, seg), BF16),
        compiler_params=_cparams(("arbitrary",)),
        name="fox",
    )(q_tab, k_tab, main, main, vt, main, cneg)


def _out_proj_kernel(x_ref, oa_ref, ob_ref, wa_ref, wb_ref, y_ref):
    y = x_ref[...] + jnp.dot(oa_ref[...], wa_ref[...], preferred_element_type=F32)
    y_ref[...] = y + jnp.dot(ob_ref[...], wb_ref[...], preferred_element_type=F32)


def _out_proj(x, o_a, o_b, w_a, w_b, *, tm):
    s_len, d = x.shape
    seg = o_a.shape[1]
    resident = functools.partial(pl.BlockSpec, pipeline_mode=pl.Buffered(1))
    return pl.pallas_call(
        _out_proj_kernel,
        grid=(s_len // tm,),
        in_specs=[
            pl.BlockSpec((tm, d), lambda i: (i, 0)),
            pl.BlockSpec((tm, seg), lambda i: (i, 0)),
            pl.BlockSpec((tm, seg), lambda i: (i, 0)),
            resident((seg, d), lambda i: (0, 0)),
            resident((seg, d), lambda i: (0, 0)),
        ],
        out_specs=pl.BlockSpec((tm, d), lambda i: (i, 0)),
        out_shape=jax.ShapeDtypeStruct((s_len, d), F32),
        compiler_params=_cparams(("arbitrary",)),
        name="out_proj",
    )(x, o_a, o_b, w_a, w_b)


def _swiglu_step(h, wg, wu, wd):
    g = jnp.dot(h, wg.astype(BF16), preferred_element_type=F32)
    u = jnp.dot(h, wu.astype(BF16), preferred_element_type=F32)
    a = (g * jax.nn.sigmoid(g) * u).astype(BF16)
    return jnp.dot(a, wd.astype(BF16), preferred_element_type=F32)


def _ffn_kernel(x_ref, g_ref, wg_ref, wu_ref, wd_ref, y_ref, h_scr):
    @pl.when(pl.program_id(1) == 0)
    def _():
        x = x_ref[...]
        h_scr[...] = _rms(x, g_ref[...]).astype(BF16)
        y_ref[...] = x

    y_ref[...] += _swiglu_step(h_scr[...], wg_ref[...], wu_ref[...], wd_ref[...])


def _ffn(x, g, wg, wu, wd, *, tm, tf):
    s_len, d = x.shape
    f = wg.shape[1]
    return pl.pallas_call(
        _ffn_kernel,
        grid=(s_len // tm, f // tf),
        in_specs=[
            pl.BlockSpec((tm, d), lambda i, j: (i, 0), pipeline_mode=pl.Buffered(1)),
            pl.BlockSpec((1, d), lambda i, j: (0, 0)),
            pl.BlockSpec((d, tf), lambda i, j: (0, j)),
            pl.BlockSpec((d, tf), lambda i, j: (0, j)),
            pl.BlockSpec((tf, d), lambda i, j: (j, 0)),
        ],
        out_specs=pl.BlockSpec((tm, d), lambda i, j: (i, 0)),
        out_shape=jax.ShapeDtypeStruct((s_len, d), F32),
        scratch_shapes=[pltpu.VMEM((tm, d), BF16)],
        compiler_params=_cparams(("arbitrary", "arbitrary")),
        name="ffn",
    )(x, g, wg, wu, wd)


def _pool_kernel(x_ref, g_ref, w_ref, sc_ref, gm_ref, r_ref, y_ref, hp_ref, route_ref, cnt_ref,
                 ext_scr, carry_scr, *, halo, n_exp):
    i = pl.program_id(0)
    tm, d = x_ref.shape
    n_grp = len(POOL_WINDOWS)
    width = d // n_grp

    @pl.when(i == 0)
    def _():
        ext_scr[0:halo, :] = jnp.zeros((halo, d), F32)

    x = x_ref[...]
    ext_scr[halo:halo + tm, :] = _rms(x, g_ref[...])
    t_glob = i * tm + lax.broadcasted_iota(jnp.int32, (tm, 1), 0)
    for gi, win in enumerate(POOL_WINDOWS):
        cs = slice(gi * width, (gi + 1) * width)
        hn = ext_scr[halo:halo + tm, cs]
        tot = hn
        for dlt in range(1, win):
            tot = tot + ext_scr[halo - dlt:halo - dlt + tm, cs]
        count = jnp.minimum(t_glob + 1, win).astype(F32)
        pooled = (tot / count - hn).astype(BF16)
        y = jnp.dot(pooled, w_ref[gi], preferred_element_type=F32)
        y_ref[:, cs] = x[:, cs] + y * sc_ref[:, cs]
    ext_scr[0:halo, :] = ext_scr[tm:tm + halo, :]

    h = _rms(y_ref[...], gm_ref[...])
    hp_ref[...] = _pack_bf16_pairs(h)
    _route_rows(h, r_ref, route_ref, cnt_ref, carry_scr, n_exp=n_exp)


def _pool_route(x, g, w_pool, scale, g_moe, r_pad, *, tm, n_exp):
    s_len, d = x.shape
    halo = 16
    assert max(POOL_WINDOWS) <= halo
    n_grp, width, _ = w_pool.shape
    row_out = lambda w: pl.BlockSpec((tm, w), lambda i: (i, 0))
    return pl.pallas_call(
        functools.partial(_pool_kernel, halo=halo, n_exp=n_exp),
        grid=(s_len // tm,),
        in_specs=[
            pl.BlockSpec((tm, d), lambda i: (i, 0)),
            pl.BlockSpec((1, d), lambda i: (0, 0)),
            pl.BlockSpec((n_grp, width, width), lambda i: (0, 0, 0)),
            pl.BlockSpec((1, d), lambda i: (0, 0)),
            pl.BlockSpec((1, d), lambda i: (0, 0)),
            pl.BlockSpec((2, d, LANES), lambda i: (0, 0, 0)),
        ],
        out_specs=[row_out(d), row_out(d // 2), row_out(LANES),
                   pl.BlockSpec((1, LANES), lambda i: (0, 0))],
        out_shape=[
            jax.ShapeDtypeStruct((s_len, d), F32),
            jax.ShapeDtypeStruct((s_len, d // 2), jnp.uint32),
            jax.ShapeDtypeStruct((s_len, LANES), F32),
            jax.ShapeDtypeStruct((1, LANES), F32),
        ],
        scratch_shapes=[pltpu.VMEM((tm + halo, d), F32), pltpu.VMEM((1, LANES), F32)],
        compiler_params=_cparams(("arbitrary",)),
        name="pool_route",
    )(x, g, w_pool, scale, g_moe, r_pad)


R_E1, R_E2, R_G1, R_G2, R_K1, R_K2 = range(6)


def _route_rows(h, r_ref, route_ref, cnt_ref, carry_scr, *, n_exp):
    tm = h.shape[0]

    @pl.when(pl.program_id(0) == 0)
    def _():
        carry_scr[...] = jnp.zeros_like(carry_scr)

    h_hi = h.astype(BF16)
    h_lo = (h - h_hi.astype(F32)).astype(BF16)
    logits = (jnp.dot(h_hi, r_ref[0], preferred_element_type=F32)
              + jnp.dot(h_hi, r_ref[1], preferred_element_type=F32)
              + jnp.dot(h_lo, r_ref[0], preferred_element_type=F32))
    lane = lax.broadcasted_iota(jnp.int32, logits.shape, 1)
    logits = jnp.where(lane < n_exp, logits, -jnp.inf)
    lane_f = lane.astype(F32)
    v1 = jnp.max(logits, axis=1, keepdims=True)
    e1 = jnp.min(jnp.where(logits == v1, lane_f, float(LANES)), axis=1, keepdims=True)
    rest = jnp.where(lane_f == e1, -jnp.inf, logits)
    v2 = jnp.max(rest, axis=1, keepdims=True)
    e2 = jnp.min(jnp.where(rest == v2, lane_f, float(LANES)), axis=1, keepdims=True)
    ex = jnp.exp(v2 - v1)
    g1 = 1.0 / (1.0 + ex)
    g2 = ex / (1.0 + ex)
    oh1 = (lane_f == e1).astype(F32)
    oh2 = (lane_f == e2).astype(F32)
    both = oh1 + oh2
    row = lax.broadcasted_iota(jnp.int32, (tm, tm), 0)
    col = lax.broadcasted_iota(jnp.int32, (tm, tm), 1)
    tri = (col < row).astype(BF16)
    before = jnp.dot(tri, both.astype(BF16), preferred_element_type=F32) + carry_scr[...]
    k1 = jnp.sum(before * oh1, axis=1, keepdims=True)
    k2 = jnp.sum(before * oh2, axis=1, keepdims=True)
    carry_scr[...] = carry_scr[...] + jnp.sum(both, axis=0, keepdims=True)
    cnt_ref[...] = carry_scr[...]
    vals = (e1, e2, g1, g2, k1, k2)
    out = jnp.zeros(logits.shape, F32)
    for li, v in enumerate(vals):
        out = jnp.where(lane == li, v, out)
    route_ref[...] = out


def _pack_bf16_pairs(h):
    half = h.shape[1] // 2
    lo = lax.bitcast_convert_type(h[:, :half].astype(BF16).astype(F32), jnp.uint32)
    hi = lax.bitcast_convert_type(h[:, half:].astype(BF16).astype(F32), jnp.uint32)
    return (lo >> 16) | (hi & jnp.uint32(0xFFFF0000))


def _unpack_bf16_pairs(u):
    lo = lax.bitcast_convert_type(u << 16, F32).astype(BF16)
    hi = lax.bitcast_convert_type(u & jnp.uint32(0xFFFF0000), F32).astype(BF16)
    return lo, hi


ROW_COPY_UNROLL = 8


def _run_row_copies(copies, n):
    def start(t, c):
        a, b = copies(t)
        a.start(priority=0)
        b.start(priority=1)
        return c

    def wait(t, c):
        a, b = copies(t)
        a.wait()
        b.wait()
        return c

    lax.fori_loop(0, n, start, 0, unroll=ROW_COPY_UNROLL)
    lax.fori_loop(0, n, wait, 0, unroll=ROW_COPY_UNROLL)


def _scatter_kernel(p1_ref, p2_ref, hp_ref, xs_in_ref, xs_ref, sem):
    del xs_in_ref
    i = pl.program_id(0)
    tm = hp_ref.shape[0]

    def copies(t):
        src = hp_ref.at[pl.ds(t, 1), :]
        return (pltpu.make_async_copy(src, xs_ref.at[pl.ds(p1_ref[i * tm + t], 1), :], sem.at[0]),
                pltpu.make_async_copy(src, xs_ref.at[pl.ds(p2_ref[i * tm + t], 1), :], sem.at[1]))

    _run_row_copies(copies, tm)


def _scatter(pos1, pos2, hp, xs_zero, *, tm):
    s_len, half = hp.shape
    grid_spec = pltpu.PrefetchScalarGridSpec(
        num_scalar_prefetch=2,
        grid=(s_len // tm,),
        in_specs=[
            pl.BlockSpec((tm, half), lambda i, p1, p2: (i, 0)),
            pl.BlockSpec(memory_space=pl.ANY),
        ],
        out_specs=pl.BlockSpec(memory_space=pl.ANY),
        scratch_shapes=[pltpu.SemaphoreType.DMA((2,))],
    )
    return pl.pallas_call(
        _scatter_kernel,
        grid_spec=grid_spec,
        out_shape=jax.ShapeDtypeStruct(xs_zero.shape, jnp.uint32),
        input_output_aliases={3: 0},
        compiler_params=_cparams(("arbitrary",)),
        name="moe_scatter",
    )(pos1, pos2, hp, xs_zero)


def _experts_kernel(e_ref, v_ref, xs_ref, wg_ref, wu_ref, wd_ref, o_ref, xb_scr):
    j = pl.program_id(0)
    f = pl.program_id(1)
    half = xs_ref.shape[1]

    @pl.when(f == 0)
    def _():
        o_ref[...] = jnp.zeros_like(o_ref)

    @pl.when(v_ref[j] > 0)
    def _():
        @pl.when(f == 0)
        def _():
            lo, hi = _unpack_bf16_pairs(xs_ref[...])
            xb_scr[:, :half] = lo
            xb_scr[:, half:] = hi

        o_ref[...] += _swiglu_step(xb_scr[...], wg_ref[0], wu_ref[0], wd_ref[0])


def _experts(item_e, item_valid, xs, wg, wu, wd, *, r, tf):
    rows, half = xs.shape
    n_exp, d, f = wg.shape
    n_items = rows // r
    n_f = f // tf

    def f_idx(fi, j, v):
        return jnp.where(v[j] > 0, fi, n_f - 1)

    grid_spec = pltpu.PrefetchScalarGridSpec(
        num_scalar_prefetch=2,
        grid=(n_items, n_f),
        in_specs=[
            pl.BlockSpec((r, half), lambda j, fi, e, v: (j, 0)),
            pl.BlockSpec((1, d, tf), lambda j, fi, e, v: (e[j], 0, f_idx(fi, j, v))),
            pl.BlockSpec((1, d, tf), lambda j, fi, e, v: (e[j], 0, f_idx(fi, j, v))),
            pl.BlockSpec((1, tf, d), lambda j, fi, e, v: (e[j], f_idx(fi, j, v), 0)),
        ],
        out_specs=pl.BlockSpec((r, d), lambda j, fi, e, v: (j, 0)),
        scratch_shapes=[pltpu.VMEM((r, d), BF16)],
    )
    return pl.pallas_call(
        _experts_kernel,
        grid_spec=grid_spec,
        out_shape=jax.ShapeDtypeStruct((rows, d), F32),
        compiler_params=_cparams(("arbitrary", "arbitrary")),
        name="moe_experts",
    )(item_e, item_valid, xs, wg, wu, wd)


def _combine_kernel(p1_ref, p2_ref, x_ref, route_ref, o_hbm, y_ref, a_scr, b_scr, sem):
    i = pl.program_id(0)
    tm = x_ref.shape[0]

    def copies(t):
        return (pltpu.make_async_copy(o_hbm.at[pl.ds(p1_ref[i * tm + t], 1), :],
                                      a_scr.at[pl.ds(t, 1), :], sem.at[0]),
                pltpu.make_async_copy(o_hbm.at[pl.ds(p2_ref[i * tm + t], 1), :],
                                      b_scr.at[pl.ds(t, 1), :], sem.at[1]))

    _run_row_copies(copies, tm)
    route = route_ref[...]
    g1 = route[:, R_G1:R_G1 + 1]
    g2 = route[:, R_G2:R_G2 + 1]
    y_ref[...] = x_ref[...] + g1 * a_scr[...] + g2 * b_scr[...]


def _combine(pos1, pos2, x, route, o_sorted, *, tm):
    s_len, d = x.shape
    grid_spec = pltpu.PrefetchScalarGridSpec(
        num_scalar_prefetch=2,
        grid=(s_len // tm,),
        in_specs=[
            pl.BlockSpec((tm, d), lambda i, p1, p2: (i, 0)),
            pl.BlockSpec((tm, LANES), lambda i, p1, p2: (i, 0)),
            pl.BlockSpec(memory_space=pl.ANY),
        ],
        out_specs=pl.BlockSpec((tm, d), lambda i, p1, p2: (i, 0)),
        scratch_shapes=[
            pltpu.VMEM((tm, d), F32),
            pltpu.VMEM((tm, d), F32),
            pltpu.SemaphoreType.DMA((2,)),
        ],
    )
    return pl.pallas_call(
        _combine_kernel,
        grid_spec=grid_spec,
        out_shape=jax.ShapeDtypeStruct((s_len, d), F32),
        compiler_params=_cparams(("arbitrary",)),
        name="moe_combine",
    )(pos1, pos2, x, route, o_sorted)


def _rotary_tables(s_len, rot_dim, period):
    half = rot_dim // 2
    inv_freq = 1.0 / (ROPE_THETA ** (jnp.arange(half, dtype=F32) * 2.0 / rot_dim))
    ang = jnp.arange(s_len, dtype=F32)[:, None] * inv_freq[None, :]
    cos, sin = jnp.cos(ang), jnp.sin(ang)
    pad = period - rot_dim
    cos_p = jnp.concatenate([cos, cos, jnp.ones((s_len, pad), F32)], axis=1)
    sin_p = jnp.concatenate([-sin, sin, jnp.zeros((s_len, pad), F32)], axis=1)
    reps = LANES // period
    return jnp.tile(cos_p, (1, reps)), jnp.tile(sin_p, (1, reps))


def _split_w_in(w_in):
    seg_a = N_HEADS_A * HEAD_DIM
    seg_i = N_IDX_HEADS * IDX_DIM
    seg_b = N_HEADS_B * HEAD_DIM
    sizes = (seg_a, seg_a, seg_a, seg_i, IDX_DIM, N_IDX_HEADS, seg_b, seg_b, seg_b, N_HEADS_B, seg_b)
    offs = np.cumsum(sizes)[:-1].tolist()
    return jnp.split(w_in, offs, axis=1)


def _moe_tables(counts, e1, e2, k1, k2, *, r, n_items):
    n_exp = counts.shape[0]
    blocks = (counts + r - 1) // r
    bend = jnp.cumsum(blocks)
    bstart = bend - blocks
    j = jnp.arange(n_items, dtype=jnp.int32)
    item_e = jnp.sum((j[:, None] >= bend[None, :]).astype(jnp.int32), axis=1)
    valid = item_e < n_exp
    last_e = jnp.minimum(item_e[jnp.maximum(bend[-1] - 1, 0)], n_exp - 1)
    item_e = jnp.where(valid, item_e, last_e).astype(jnp.int32)
    pos1 = bstart[e1] * r + k1
    pos2 = bstart[e2] * r + k2
    return item_e, valid.astype(jnp.int32), pos1.astype(jnp.int32), pos2.astype(jnp.int32)


def _moe_block_rows(n_assign, n_exp):
    r = -(-n_assign // (2 * n_exp))
    r = -(-(r + r // 16) // 64) * 64
    return max(r, 64)


def kernel(x, attn_norm_e, w_in_e, b_forget_e, q_norm_a_e, k_norm_a_e, q_norm_b_e, k_norm_b_e,
           w_out_e, ffn_norm_e, w_gate_e, w_up_e, w_down_e, pool_norm_o, w_pool_o, pool_scale_o,
           moe_norm_o, router_o, w_gate_o, w_up_o, w_down_o):
    b, s_len, d = x.shape
    assert b == 1
    x0 = x[0]
    seg = N_HEADS_A * HEAD_DIM
    assert seg == N_IDX_HEADS * IDX_DIM == N_HEADS_B * HEAD_DIM
    assert N_IDX_HEADS <= FB_OFF - WI_OFF and N_HEADS_B <= LANES - FB_OFF
    row = lambda v: v.reshape(1, -1)

    qa, ka, va, qi, ki, wi, qb, kb, vb, fb, gb = _split_w_in(w_in_e[0])
    w_main = jnp.concatenate([qa, ka, va, qi, qb, kb, vb, gb], axis=1).astype(BF16)
    w_small = jnp.concatenate(
        [ki, wi, jnp.zeros((d, FB_OFF - WI_OFF - N_IDX_HEADS), F32), fb,
         jnp.zeros((d, LANES - FB_OFF - N_HEADS_B), F32)], axis=1).astype(BF16)
    bf_pad = jnp.zeros((1, LANES), F32).at[0, FB_OFF:FB_OFF + N_HEADS_B].set(b_forget_e[0])
    scale = HEAD_DIM ** -0.5 * LOG2_E
    gains = jnp.stack([q_norm_a_e[0] * scale, k_norm_a_e[0], q_norm_b_e[0] * scale, k_norm_b_e[0]])
    tabs = _rotary_tables(s_len, ROT_DIM, HEAD_DIM) + _rotary_tables(s_len, IDX_ROT_DIM, IDX_DIM)
    t_attn = _tile(s_len, 512)
    main, small, v_t, c_neg, ki_rot = _in_proj(x0, row(attn_norm_e[0]), w_main, w_small, bf_pad, gains,
                                               tabs, n_seg=8, seg=seg, tm=t_attn)
    wi_t = small[:, WI_OFF:WI_OFF + N_IDX_HEADS].T

    top_k = min(TOPK_MAX, s_len // 4)
    o_a = _dsa_t(main, v_t, ki_rot, wi_t, tq=_tile(s_len, 256), tk=t_attn, top_k=top_k)
    o_b = _fox_t(main, v_t, c_neg, t=t_attn)
    w_out = w_out_e[0].astype(BF16)
    x1 = _out_proj(x0, o_a, o_b, w_out[:seg], w_out[seg:], tm=_tile(s_len, 512))

    x2 = _ffn(x1, row(ffn_norm_e[0]), w_gate_e[0], w_up_e[0], w_down_e[0],
              tm=_tile(s_len, 1024), tf=_tile(w_gate_e.shape[2], 256))

    n_exp = router_o.shape[2]
    r_f32 = jnp.zeros((d, LANES), F32).at[:, :n_exp].set(router_o[0])
    r_hi = r_f32.astype(BF16)
    r_pad = jnp.stack([r_hi, (r_f32 - r_hi.astype(F32)).astype(BF16)])
    x3, h3_packed, route, counts = _pool_route(
        x2, row(pool_norm_o[0]), w_pool_o[0].astype(BF16), row(pool_scale_o[0]),
        row(moe_norm_o[0]), r_pad, tm=_tile(s_len, 512), n_exp=n_exp)

    r_rows = _moe_block_rows(2 * s_len, n_exp)
    n_items = (2 * s_len) // r_rows + n_exp
    as_int = lambda c: route[:, c].astype(jnp.int32)
    item_e, item_valid, pos1, pos2 = _moe_tables(
        counts[0, :n_exp].astype(jnp.int32), as_int(R_E1), as_int(R_E2), as_int(R_K1), as_int(R_K2),
        r=r_rows, n_items=n_items)
    xs_zero = jnp.zeros((n_items * r_rows, d // 2), jnp.uint32)
    xs = _scatter(pos1, pos2, h3_packed, xs_zero, tm=_tile(s_len, 256))
    o_sorted = _experts(item_e, item_valid, xs, w_gate_o[0], w_up_o[0], w_down_o[0],
                        r=r_rows, tf=_tile(w_gate_o.shape[3], 256))
    y = _combine(pos1, pos2, x3, route, o_sorted, tm=_tile(s_len, 256))
    return y[None]
```

```python
import functools
from typing import NamedTuple

import jax
import jax.numpy as jnp
import numpy as np
from jax import lax
from jax.experimental import pallas as pl
from jax.experimental.pallas import tpu as pltpu

HEAD_DIM = 128
N_HEADS_A = 8
N_HEADS_B = 8
N_IDX_HEADS = 16
IDX_DIM = 64
ROT_DIM = HEAD_DIM // 4
IDX_ROT_DIM = IDX_DIM // 4
ROPE_THETA = 500000.0
CHUNK = 64
TOPK_MAX = 256
POOL_WINDOWS = (2, 4, 8, 16)
RMS_EPS = 1e-6

LANES = 128
MXU_COLS = 256
VMEM_LIMIT_BYTES = 56 * 1024 * 1024

WI_OFF = IDX_DIM
FB_OFF = 80

LOG2_E = 1.4426950408889634
NEG_BIG = -1e30
INT_MIN = -(2 ** 31)
INT_MAX = 2 ** 31 - 1
KEY_NONE = INT_MIN + 1
HALF_MIN = -(2 ** 15)
HALF_MAX = 2 ** 15 - 1

F32 = jnp.float32
BF16 = jnp.bfloat16


def _cparams(sem):
    return pltpu.CompilerParams(dimension_semantics=sem, vmem_limit_bytes=VMEM_LIMIT_BYTES)


def _tile(n, pref):
    t = min(n, pref)
    while n % t:
        t //= 2
    return t


def _rms(x, g):
    ms = jnp.mean(x * x, axis=-1, keepdims=True)
    return x * lax.rsqrt(ms + RMS_EPS) * g


def _rotate(y, cos_t, sin_t, half, period):
    lane = lax.broadcasted_iota(jnp.int32, y.shape, 1)
    up = pltpu.roll(y, LANES - half, 1)
    dn = pltpu.roll(y, half, 1)
    rot = jnp.where((lane & (period - 1)) < half, up, dn)
    return y * cos_t + rot * sin_t


def _in_proj_kernel(x_ref, g_ref, wm_ref, ws_ref, bf_ref, gains_ref, cosa_ref, sina_ref,
                    cosi_ref, sini_ref, main_ref, small_ref, vt_ref, cneg_ref, ki_ref, h_scr,
                    carry_scr, *, n_seg, idx_scale, n_fb):
    i = pl.program_id(0)
    j = pl.program_id(1)
    tm = h_scr.shape[0]
    seg = wm_ref.shape[1]

    @pl.when(j == 0)
    def _():
        h_scr[...] = _rms(x_ref[...], g_ref[...]).astype(BF16)

    def head_norm(slab, gain):
        ms = jnp.mean(slab * slab, axis=-1, keepdims=True)
        return slab * lax.rsqrt(ms + RMS_EPS) * gain

    def project(epilogue):
        for c0 in range(0, seg, MXU_COLS):
            acc = jnp.dot(h_scr[...], wm_ref[:, c0:c0 + MXU_COLS], preferred_element_type=F32)
            epilogue(acc, c0)

    def per_slab(fn):
        def epilogue(acc, c0):
            for s0 in range(0, MXU_COLS, LANES):
                y = fn(acc[:, s0:s0 + LANES])
                main_ref[0, :, c0 + s0:c0 + s0 + LANES] = y.astype(BF16)
        return epilogue

    @pl.when(j < 2)
    def _():
        gain = gains_ref[pl.ds(j, 1), :]
        project(per_slab(lambda y: _rotate(head_norm(y, gain), cosa_ref[...], sina_ref[...],
                                           ROT_DIM // 2, LANES)))

    @pl.when((j == 4) | (j == 5))
    def _():
        gain = gains_ref[pl.ds(j - 2, 1), :]
        project(per_slab(lambda y: head_norm(y, gain)))

    @pl.when((j == 2) | (j == 6))
    def _():
        def epilogue(acc, c0):
            main_ref[0, :, c0:c0 + MXU_COLS] = acc.astype(BF16)
            vt_ref[0, 0, c0:c0 + MXU_COLS, :] = acc.T.astype(BF16)
        project(epilogue)

    @pl.when(j == 3)
    def _():
        project(per_slab(lambda y: _rotate(y, cosi_ref[...], sini_ref[...], IDX_ROT_DIM // 2, IDX_DIM)))

    @pl.when(j == 7)
    def _():
        project(per_slab(jax.nn.sigmoid))

    @pl.when(j == n_seg)
    def _():
        acc = jnp.dot(h_scr[...], ws_ref[...], preferred_element_type=F32)
        lane = lax.broadcasted_iota(jnp.int32, acc.shape, 1)
        is_ki = lane < WI_OFF
        cos_k = jnp.where(is_ki, cosi_ref[...], 1.0)
        sin_k = jnp.where(is_ki, sini_ref[...], 0.0)
        ki = _rotate(acc, cos_k, sin_k, IDX_ROT_DIM // 2, IDX_DIM)
        wi = acc * idx_scale
        z = acc + bf_ref[...]
        ls = jnp.minimum(z, 0.0) - jnp.log1p(jnp.exp(-jnp.abs(z)))
        is_fb = (lane >= FB_OFF) & (lane < FB_OFF + n_fb)
        ls = jnp.where(is_fb, ls, 0.0)
        row = lax.broadcasted_iota(jnp.int32, (tm, tm), 0)
        col = lax.broadcasted_iota(jnp.int32, (tm, tm), 1)
        tri = (col <= row).astype(BF16)
        p1 = ls.astype(BF16)
        r1 = ls - p1.astype(F32)
        p2 = r1.astype(BF16)
        p3 = (r1 - p2.astype(F32)).astype(BF16)
        c = (jnp.dot(tri, p1, preferred_element_type=F32)
             + jnp.dot(tri, p2, preferred_element_type=F32)
             + jnp.dot(tri, p3, preferred_element_type=F32))

        @pl.when(i == 0)
        def _():
            carry_scr[...] = jnp.zeros_like(carry_scr)

        c = c + carry_scr[...]
        carry_scr[...] = c[tm - 1:tm, :]
        small_ref[...] = jnp.where(is_ki, ki, jnp.where(lane < FB_OFF, wi, c))
        ki_ref[...] = jnp.where(is_ki, ki, 0.0).astype(BF16)
        for hb in range(n_fb):
            cneg_ref[hb] = jnp.broadcast_to(-LOG2_E * c[:, FB_OFF + hb:FB_OFF + hb + 1], (tm, LANES))


def _in_proj(x, g, w_main, w_small, bf_pad, gains, tabs, *, n_seg, seg, tm):
    s_len, d = x.shape
    cosa, sina, cosi, sini = tabs
    grid = (s_len // tm, n_seg + 1)
    row_spec = lambda w: pl.BlockSpec((tm, w), lambda i, j: (i, 0))
    kern = functools.partial(_in_proj_kernel, n_seg=n_seg,
                             idx_scale=float((N_IDX_HEADS * IDX_DIM) ** -0.5), n_fb=N_HEADS_B)
    return pl.pallas_call(
        kern,
        grid=grid,
        in_specs=[
            pl.BlockSpec((tm, d), lambda i, j: (i, 0), pipeline_mode=pl.Buffered(1)),
            pl.BlockSpec((1, d), lambda i, j: (0, 0)),
            pl.BlockSpec((d, seg), lambda i, j: (0, jnp.minimum(j, n_seg - 1))),
            pl.BlockSpec((d, LANES), lambda i, j: (0, 0)),
            pl.BlockSpec((1, LANES), lambda i, j: (0, 0)),
            pl.BlockSpec((4, LANES), lambda i, j: (0, 0)),
            row_spec(LANES), row_spec(LANES), row_spec(LANES), row_spec(LANES),
        ],
        out_specs=[
            pl.BlockSpec((1, tm, seg), lambda i, j: (jnp.minimum(j, n_seg - 1), i, 0)),
            pl.BlockSpec((tm, LANES), lambda i, j: (i, 0)),
            pl.BlockSpec((1, 1, seg, tm), lambda i, j: (jnp.where(j <= 2, 0, 1), i, 0, 0)),
            pl.BlockSpec((N_HEADS_B, tm, LANES), lambda i, j: (0, i, 0)),
            pl.BlockSpec((tm, LANES), lambda i, j: (i, 0)),
        ],
        out_shape=[
            jax.ShapeDtypeStruct((n_seg, s_len, seg), BF16),
            jax.ShapeDtypeStruct((s_len, LANES), F32),
            jax.ShapeDtypeStruct((2, s_len // tm, seg, tm), BF16),
            jax.ShapeDtypeStruct((N_HEADS_B, s_len, LANES), F32),
            jax.ShapeDtypeStruct((s_len, LANES), BF16),
        ],
        scratch_shapes=[
            pltpu.VMEM((tm, d), BF16),
            pltpu.VMEM((1, LANES), F32),
        ],
        compiler_params=_cparams(("arbitrary", "arbitrary")),
        name="in_proj",
    )(x, g, w_main, w_small, bf_pad, gains, cosa, sina, cosi, sini)


def _transpose_heads(src, dst_scr):
    for s in range(src.shape[1] // LANES):
        sl = slice(s * LANES, (s + 1) * LANES)
        dst_scr[sl, :] = src[:, sl].astype(F32).T.astype(BF16)


def _logits_pass(h, z_t, z_scr, m_scr, mnew_scr):
    z_scr[h] = z_t
    mnew_scr[h:h + 1, :] = jnp.maximum(m_scr[h:h + 1, :], jnp.max(z_t, axis=0, keepdims=True))


def _softmax_pass(h, v_t, z_scr, m_scr, mnew_scr, l_scr, acc_scr):
    hs = slice(h * HEAD_DIM, (h + 1) * HEAD_DIM)
    m_new = mnew_scr[h:h + 1, :]
    alpha = jnp.exp2(m_scr[h:h + 1, :] - m_new)
    p_t = jnp.exp2(z_scr[h] - m_new)
    l_scr[h:h + 1, :] = alpha * l_scr[h:h + 1, :] + jnp.sum(p_t, axis=0, keepdims=True)
    acc_scr[hs, :] = alpha * acc_scr[hs, :] + jnp.dot(v_t, p_t.astype(BF16),
                                                      preferred_element_type=F32)
    m_scr[h:h + 1, :] = m_new


def _init_attn_state(m_scr, l_scr, acc_scr):
    m_scr[...] = jnp.full_like(m_scr, NEG_BIG)
    l_scr[...] = jnp.zeros_like(l_scr)
    acc_scr[...] = jnp.zeros_like(acc_scr)


def _dsa_t_kernel(qa_ref, qi_ref, wit_ref, ki_ref, ka_ref, vat_ref, o_ref,
                  key_scr, half_scr, qat_scr, qit_scr, z_scr, m_scr, mnew_scr, l_scr, acc_scr,
                  *, tq, tk, top_k, n_heads, n_idx):
    i = pl.program_id(0)
    n_kt = ((i + 1) * tq + tk - 1) // tk
    t_glob = i * tq + lax.broadcasted_iota(jnp.int32, (1, tq), 1)
    adm_len = (t_glob // CHUNK + 1) * CHUNK
    key_pos = lax.broadcasted_iota(jnp.int32, (tk, 1), 0)

    _transpose_heads(qa_ref[0], qat_scr)
    _transpose_heads(qi_ref[0], qit_scr)

    def score_tile(kt, carry):
        k0 = pl.multiple_of(kt * tk, tk)
        k_idx = ki_ref[pl.ds(k0, tk), 0:IDX_DIM]
        acc = jnp.zeros((tk, tq), F32)
        for h in range(n_idx):
            q_t = qit_scr[h * IDX_DIM:(h + 1) * IDX_DIM, :]
            rel = jnp.maximum(jnp.dot(k_idx, q_t, preferred_element_type=F32), 0.0)
            acc = acc + rel * wit_ref[h:h + 1, :]
        bits = lax.bitcast_convert_type(acc, jnp.int32)
        key = jnp.where(bits < 0, bits ^ INT_MAX, bits)
        key = jnp.where(k0 + key_pos < adm_len, key, KEY_NONE)
        key_scr[kt] = key
        half_scr[kt] = (key >> 16).astype(jnp.int16)
        return carry

    lax.fori_loop(0, n_kt, score_tile, 0)

    def count(pred):
        def body(kt, cnt):
            hit = jnp.where(pred(key_scr[kt], kt * tk + key_pos), 1, 0)
            return cnt + jnp.sum(hit.reshape(tk // 8, 8, tq), axis=0)

        cnt = lax.fori_loop(0, n_kt, body, jnp.zeros((8, tq), jnp.int32))
        return jnp.sum(cnt, axis=0, keepdims=True)

    def count_half_ge(mid):
        mid16 = mid.astype(jnp.int16)

        def body(kt, cnt):
            hit = jnp.where(half_scr[kt] >= mid16, jnp.int16(1), jnp.int16(0))
            parts = [hit[r * 16:(r + 1) * 16] for r in range(tk // 16)]
            while len(parts) > 1:
                parts = [parts[a] + parts[a + 1] for a in range(0, len(parts), 2)]
            return cnt + parts[0].astype(jnp.int32)

        cnt = lax.fori_loop(0, n_kt, body, jnp.zeros((16, tq), jnp.int32))
        return jnp.sum(cnt, axis=0, keepdims=True)

    def bisect_half(need_k, c_all):
        def step(_, state):
            lo, hi, c_lo, c_hi = state
            mid = (lo + hi) >> 1
            c_mid = count_half_ge(mid)
            take = c_mid >= need_k
            return (jnp.where(take, mid, lo), jnp.where(take, hi, mid),
                    jnp.where(take, c_mid, c_lo), jnp.where(take, c_hi, c_mid))

        x, _, c_x, c_above = lax.fori_loop(0, 16, step,
                                           (row(HALF_MIN), row(HALF_MAX + 1), c_all, row(0)))
        return x, c_x, c_above

    row = lambda v: jnp.full((1, tq), v, jnp.int32)
    thr_hi, c_ge_hi, c_gt_hi = bisect_half(top_k, n_kt * tk + row(0))
    thr_hi16 = thr_hi.astype(jnp.int16)

    def low_half_tile(kt, carry):
        low = ((key_scr[kt] & 0xFFFF) + HALF_MIN).astype(jnp.int16)
        half_scr[kt] = jnp.where(half_scr[kt] == thr_hi16, low, jnp.int16(HALF_MIN))
        return carry

    lax.fori_loop(0, n_kt, low_half_tile, 0)
    thr_lo, c_ge_lo, c_gt_lo = bisect_half(top_k - c_gt_hi, c_ge_hi - c_gt_hi)
    thr = thr_hi * 65536 + (thr_lo - HALF_MIN)
    c_ge = c_gt_hi + c_ge_lo
    c_gt = c_gt_hi + c_gt_lo
    short = thr <= KEY_NONE
    thr = jnp.maximum(thr, KEY_NONE)
    need = top_k - c_gt

    def tie_cut():
        def step(_, lohi):
            lo, hi = lohi
            mid = (lo + hi) >> 1
            ok = count(lambda k, pos: (k == thr) & (pos <= mid)) >= need
            return jnp.where(ok, lo, mid), jnp.where(ok, mid, hi)

        n_steps = int(np.ceil(np.log2(key_scr.shape[0] * tk))) + 1
        return lax.fori_loop(0, n_steps, step, (row(-1), n_kt * tk - 1 + row(0)))[1]

    has_tie = jnp.max(jnp.where((c_ge > top_k) & ~short, 1.0, 0.0)) > 0.5
    cut = lax.cond(has_tie, tie_cut, lambda: row(INT_MAX))
    cut = jnp.where(short, -1, cut)

    _init_attn_state(m_scr, l_scr, acc_scr)

    def attend(kt, carry):
        k0 = pl.multiple_of(kt * tk, tk)
        keys = key_scr[kt]
        sel = (keys > thr) | ((keys == thr) & (kt * tk + key_pos <= cut))
        bias = jnp.where(sel, 0.0, NEG_BIG).astype(F32)
        for h in range(n_heads):
            hs = slice(h * HEAD_DIM, (h + 1) * HEAD_DIM)
            z_t = jnp.dot(ka_ref[0, pl.ds(k0, tk), hs], qat_scr[hs, :],
                          preferred_element_type=F32) + bias
            _logits_pass(h, z_t, z_scr, m_scr, mnew_scr)
        for h in range(n_heads):
            hs = slice(h * HEAD_DIM, (h + 1) * HEAD_DIM)
            _softmax_pass(h, vat_ref[0, kt, hs, :], z_scr, m_scr, mnew_scr, l_scr, acc_scr)
        return carry

    lax.fori_loop(0, n_kt, attend, 0)
    for h in range(n_heads):
        hs = slice(h * HEAD_DIM, (h + 1) * HEAD_DIM)
        o_ref[:, hs] = (acc_scr[hs, :] / l_scr[h:h + 1, :]).T.astype(BF16)


def _dsa_t(main, vt, ki, wi_t, *, tq, tk, top_k):
    n_seg, s_len, seg = main.shape
    assert vt.shape == (2, s_len // tk, seg, tk)
    kern = functools.partial(_dsa_t_kernel, tq=tq, tk=tk, top_k=top_k, n_heads=N_HEADS_A,
                             n_idx=N_IDX_HEADS)
    resident = functools.partial(pl.BlockSpec, pipeline_mode=pl.Buffered(1))
    return pl.pallas_call(
        kern,
        grid=(s_len // tq,),
        in_specs=[
            pl.BlockSpec((1, tq, seg), lambda i: (0, i, 0)),
            pl.BlockSpec((1, tq, seg), lambda i: (3, i, 0)),
            pl.BlockSpec((N_IDX_HEADS, tq), lambda i: (0, i)),
            resident((s_len, LANES), lambda i: (0, 0)),
            resident((1, s_len, seg), lambda i: (1, 0, 0)),
            resident((1, s_len // tk, seg, tk), lambda i: (0, 0, 0, 0)),
        ],
        out_specs=pl.BlockSpec((tq, seg), lambda i: (i, 0)),
        out_shape=jax.ShapeDtypeStruct((s_len, seg), BF16),
        scratch_shapes=[
            pltpu.VMEM((s_len // tk, tk, tq), jnp.int32),
            pltpu.VMEM((s_len // tk, tk, tq), jnp.int16),
            pltpu.VMEM((seg, tq), BF16),
            pltpu.VMEM((seg, tq), BF16),
            pltpu.VMEM((N_HEADS_A, tk, tq), F32),
            pltpu.VMEM((N_HEADS_A, tq), F32),
            pltpu.VMEM((N_HEADS_A, tq), F32),
            pltpu.VMEM((N_HEADS_A, tq), F32),
            pltpu.VMEM((seg, tq), F32),
        ],
        compiler_params=_cparams(("arbitrary",)),
        name="dsa",
    )(main, main, wi_t, ki, main, vt)


def _fox_t_kernel(qt_ref, kt_ref, q_ref, k_ref, vt_ref, gate_ref, cneg_ref, o_ref,
                  qt_scr, z_scr, m_scr, mnew_scr, l_scr, acc_scr, *, t, n_heads):
    p = pl.program_id(0)
    qi = qt_ref[p]
    ki = kt_ref[p]

    @pl.when(ki == 0)
    def _():
        _transpose_heads(q_ref[0], qt_scr)
        _init_attn_state(m_scr, l_scr, acc_scr)

    def update(causal):
        key_i = lax.broadcasted_iota(jnp.int32, (t, LANES), 0)
        qry_i = lax.broadcasted_iota(jnp.int32, (t, LANES), 1)
        for h in range(n_heads):
            hs = slice(h * HEAD_DIM, (h + 1) * HEAD_DIM)
            z_t = jnp.dot(k_ref[0, :, hs], qt_scr[hs, :], preferred_element_type=F32)
            cols = []
            for c in range(t // LANES):
                zc = z_t[:, c * LANES:(c + 1) * LANES] + cneg_ref[h]
                if causal:
                    zc = jnp.where(key_i <= qry_i + c * LANES, zc, NEG_BIG)
                cols.append(zc)
            _logits_pass(h, jnp.concatenate(cols, axis=1), z_scr, m_scr, mnew_scr)
        for h in range(n_heads):
            hs = slice(h * HEAD_DIM, (h + 1) * HEAD_DIM)
            _softmax_pass(h, vt_ref[0, 0, hs, :], z_scr, m_scr, mnew_scr, l_scr, acc_scr)

    @pl.when(ki < qi)
    def _():
        update(causal=False)

    @pl.when(ki == qi)
    def _():
        update(causal=True)
        for h in range(n_heads):
            hs = slice(h * HEAD_DIM, (h + 1) * HEAD_DIM)
            o = (acc_scr[hs, :] / l_scr[h:h + 1, :]).T
            o_ref[:, hs] = (o * gate_ref[0, :, hs].astype(F32)).astype(BF16)


def _fox_t(main, vt, cneg, *, t):
    n_seg, s_len, seg = main.shape
    assert vt.shape == (2, s_len // t, seg, t)
    nq = s_len // t
    pairs = [(a, b) for a in range(nq) for b in range(a + 1)]
    q_tab = jnp.asarray([a for a, _ in pairs], jnp.int32)
    k_tab = jnp.asarray([b for _, b in pairs], jnp.int32)
    kern = functools.partial(_fox_t_kernel, t=t, n_heads=N_HEADS_B)
    grid_spec = pltpu.PrefetchScalarGridSpec(
        num_scalar_prefetch=2,
        grid=(len(pairs),),
        in_specs=[
            pl.BlockSpec((1, t, seg), lambda p, qt, kt: (4, qt[p], 0)),
            pl.BlockSpec((1, t, seg), lambda p, qt, kt: (5, kt[p], 0)),
            pl.BlockSpec((1, 1, seg, t), lambda p, qt, kt: (1, kt[p], 0, 0)),
            pl.BlockSpec((1, t, seg), lambda p, qt, kt: (7, qt[p], 0)),
            pl.BlockSpec((N_HEADS_B, t, LANES), lambda p, qt, kt: (0, kt[p], 0)),
        ],
        out_specs=pl.BlockSpec((t, seg), lambda p, qt, kt: (qt[p], 0)),
        scratch_shapes=[
            pltpu.VMEM((seg, t), BF16),
            pltpu.VMEM((N_HEADS_B, t, t), F32),
            pltpu.VMEM((N_HEADS_B, t), F32),
            pltpu.VMEM((N_HEADS_B, t), F32),
            pltpu.VMEM((N_HEADS_B, t), F32),
            pltpu.VMEM((seg, t), F32),
        ],
    )
    return pl.pallas_call(
        kern,
        grid_spec=grid_spec,
        out_shape=jax.ShapeDtypeStruct((s_len, seg), BF16),
        compiler_params=_cparams(("arbitrary",)),
        name="fox",
    )(q_tab, k_tab, main, main, vt, main, cneg)


def _out_proj_kernel(x_ref, oa_ref, ob_ref, wa_ref, wb_ref, y_ref):
    y = x_ref[...] + jnp.dot(oa_ref[...], wa_ref[...], preferred_element_type=F32)
    y_ref[...] = y + jnp.dot(ob_ref[...], wb_ref[...], preferred_element_type=F32)


def _out_proj(x, o_a, o_b, w_a, w_b, *, tm):
    s_len, d = x.shape
    seg = o_a.shape[1]
    resident = functools.partial(pl.BlockSpec, pipeline_mode=pl.Buffered(1))
    return pl.pallas_call(
        _out_proj_kernel,
        grid=(s_len // tm,),
        in_specs=[
            pl.BlockSpec((tm, d), lambda i: (i, 0)),
            pl.BlockSpec((tm, seg), lambda i: (i, 0)),
            pl.BlockSpec((tm, seg), lambda i: (i, 0)),
            resident((seg, d), lambda i: (0, 0)),
            resident((seg, d), lambda i: (0, 0)),
        ],
        out_specs=pl.BlockSpec((tm, d), lambda i: (i, 0)),
        out_shape=jax.ShapeDtypeStruct((s_len, d), F32),
        compiler_params=_cparams(("arbitrary",)),
        name="out_proj",
    )(x, o_a, o_b, w_a, w_b)


def _swiglu_step(h, wg, wu, wd):
    g = jnp.dot(h, wg.astype(BF16), preferred_element_type=F32)
    u = jnp.dot(h, wu.astype(BF16), preferred_element_type=F32)
    a = (g * jax.nn.sigmoid(g) * u).astype(BF16)
    return jnp.dot(a, wd.astype(BF16), preferred_element_type=F32)


def _ffn_kernel(x_ref, g_ref, wg_ref, wu_ref, wd_ref, y_ref, h_scr):
    @pl.when(pl.program_id(1) == 0)
    def _():
        x = x_ref[...]
        h_scr[...] = _rms(x, g_ref[...]).astype(BF16)
        y_ref[...] = x

    y_ref[...] += _swiglu_step(h_scr[...], wg_ref[...], wu_ref[...], wd_ref[...])


def _ffn(x, g, wg, wu, wd, *, tm, tf):
    s_len, d = x.shape
    f = wg.shape[1]
    return pl.pallas_call(
        _ffn_kernel,
        grid=(s_len // tm, f // tf),
        in_specs=[
            pl.BlockSpec((tm, d), lambda i, j: (i, 0), pipeline_mode=pl.Buffered(1)),
            pl.BlockSpec((1, d), lambda i, j: (0, 0)),
            pl.BlockSpec((d, tf), lambda i, j: (0, j)),
            pl.BlockSpec((d, tf), lambda i, j: (0, j)),
            pl.BlockSpec((tf, d), lambda i, j: (j, 0)),
        ],
        out_specs=pl.BlockSpec((tm, d), lambda i, j: (i, 0)),
        out_shape=jax.ShapeDtypeStruct((s_len, d), F32),
        scratch_shapes=[pltpu.VMEM((tm, d), BF16)],
        compiler_params=_cparams(("arbitrary", "arbitrary")),
        name="ffn",
    )(x, g, wg, wu, wd)


def _pool_kernel(x_ref, g_ref, w_ref, sc_ref, gm_ref, r_ref, y_ref, hp_ref, route_ref, cnt_ref,
                 ext_scr, carry_scr, *, halo, n_exp):
    i = pl.program_id(0)
    tm, d = x_ref.shape
    n_grp = len(POOL_WINDOWS)
    width = d // n_grp

    @pl.when(i == 0)
    def _():
        ext_scr[0:halo, :] = jnp.zeros((halo, d), F32)

    x = x_ref[...]
    ext_scr[halo:halo + tm, :] = _rms(x, g_ref[...])
    t_glob = i * tm + lax.broadcasted_iota(jnp.int32, (tm, 1), 0)
    for gi, win in enumerate(POOL_WINDOWS):
        cs = slice(gi * width, (gi + 1) * width)
        hn = ext_scr[halo:halo + tm, cs]
        tot = hn
        for dlt in range(1, win):
            tot = tot + ext_scr[halo - dlt:halo - dlt + tm, cs]
        count = jnp.minimum(t_glob + 1, win).astype(F32)
        pooled = (tot / count - hn).astype(BF16)
        y = jnp.dot(pooled, w_ref[gi], preferred_element_type=F32)
        y_ref[:, cs] = x[:, cs] + y * sc_ref[:, cs]
    ext_scr[0:halo, :] = ext_scr[tm:tm + halo, :]

    h = _rms(y_ref[...], gm_ref[...])
    hp_ref[...] = _pack_bf16_pairs(h)
    _route_rows(h, r_ref, route_ref, cnt_ref, carry_scr, n_exp=n_exp)


def _pool_route(x, g, w_pool, scale, g_moe, r_pad, *, tm, n_exp):
    s_len, d = x.shape
    halo = 16
    assert max(POOL_WINDOWS) <= halo
    n_grp, width, _ = w_pool.shape
    row_out = lambda w: pl.BlockSpec((tm, w), lambda i: (i, 0))
    return pl.pallas_call(
        functools.partial(_pool_kernel, halo=halo, n_exp=n_exp),
        grid=(s_len // tm,),
        in_specs=[
            pl.BlockSpec((tm, d), lambda i: (i, 0)),
            pl.BlockSpec((1, d), lambda i: (0, 0)),
            pl.BlockSpec((n_grp, width, width), lambda i: (0, 0, 0)),
            pl.BlockSpec((1, d), lambda i: (0, 0)),
            pl.BlockSpec((1, d), lambda i: (0, 0)),
            pl.BlockSpec((2, d, LANES), lambda i: (0, 0, 0)),
        ],
        out_specs=[row_out(d), row_out(d // 2), row_out(LANES),
                   pl.BlockSpec((1, LANES), lambda i: (0, 0))],
        out_shape=[
            jax.ShapeDtypeStruct((s_len, d), F32),
            jax.ShapeDtypeStruct((s_len, d // 2), jnp.uint32),
            jax.ShapeDtypeStruct((s_len, LANES), F32),
            jax.ShapeDtypeStruct((1, LANES), F32),
        ],
        scratch_shapes=[pltpu.VMEM((tm + halo, d), F32), pltpu.VMEM((1, LANES), F32)],
        compiler_params=_cparams(("arbitrary",)),
        name="pool_route",
    )(x, g, w_pool, scale, g_moe, r_pad)


R_E1, R_E2, R_G1, R_G2, R_K1, R_K2 = range(6)


def _route_rows(h, r_ref, route_ref, cnt_ref, carry_scr, *, n_exp):
    tm = h.shape[0]

    @pl.when(pl.program_id(0) == 0)
    def _():
        carry_scr[...] = jnp.zeros_like(carry_scr)

    h_hi = h.astype(BF16)
    h_lo = (h - h_hi.astype(F32)).astype(BF16)
    logits = (jnp.dot(h_hi, r_ref[0], preferred_element_type=F32)
              + jnp.dot(h_hi, r_ref[1], preferred_element_type=F32)
              + jnp.dot(h_lo, r_ref[0], preferred_element_type=F32))
    lane = lax.broadcasted_iota(jnp.int32, logits.shape, 1)
    logits = jnp.where(lane < n_exp, logits, -jnp.inf)
    lane_f = lane.astype(F32)
    v1 = jnp.max(logits, axis=1, keepdims=True)
    e1 = jnp.min(jnp.where(logits == v1, lane_f, float(LANES)), axis=1, keepdims=True)
    rest = jnp.where(lane_f == e1, -jnp.inf, logits)
    v2 = jnp.max(rest, axis=1, keepdims=True)
    e2 = jnp.min(jnp.where(rest == v2, lane_f, float(LANES)), axis=1, keepdims=True)
    ex = jnp.exp(v2 - v1)
    g1 = 1.0 / (1.0 + ex)
    g2 = ex / (1.0 + ex)
    oh1 = (lane_f == e1).astype(F32)
    oh2 = (lane_f == e2).astype(F32)
    both = oh1 + oh2
    row = lax.broadcasted_iota(jnp.int32, (tm, tm), 0)
    col = lax.broadcasted_iota(jnp.int32, (tm, tm), 1)
    tri = (col < row).astype(BF16)
    before = jnp.dot(tri, both.astype(BF16), preferred_element_type=F32) + carry_scr[...]
    k1 = jnp.sum(before * oh1, axis=1, keepdims=True)
    k2 = jnp.sum(before * oh2, axis=1, keepdims=True)
    carry_scr[...] = carry_scr[...] + jnp.sum(both, axis=0, keepdims=True)
    cnt_ref[...] = carry_scr[...]
    vals = (e1, e2, g1, g2, k1, k2)
    out = jnp.zeros(logits.shape, F32)
    for li, v in enumerate(vals):
        out = jnp.where(lane == li, v, out)
    route_ref[...] = out


def _pack_bf16_pairs(h):
    half = h.shape[1] // 2
    lo = lax.bitcast_convert_type(h[:, :half].astype(BF16).astype(F32), jnp.uint32)
    hi = lax.bitcast_convert_type(h[:, half:].astype(BF16).astype(F32), jnp.uint32)
    return (lo >> 16) | (hi & jnp.uint32(0xFFFF0000))


def _unpack_bf16_pairs(u):
    lo = lax.bitcast_convert_type(u << 16, F32).astype(BF16)
    hi = lax.bitcast_convert_type(u & jnp.uint32(0xFFFF0000), F32).astype(BF16)
    return lo, hi


ROW_COPY_UNROLL = 8


def _run_row_copies(copies, n):
    def start(t, c):
        a, b = copies(t)
        a.start(priority=0)
        b.start(priority=1)
        return c

    def wait(t, c):
        a, b = copies(t)
        a.wait()
        b.wait()
        return c

    lax.fori_loop(0, n, start, 0, unroll=ROW_COPY_UNROLL)
    lax.fori_loop(0, n, wait, 0, unroll=ROW_COPY_UNROLL)


def _scatter_kernel(p1_ref, p2_ref, hp_ref, xs_in_ref, xs_ref, sem):
    del xs_in_ref
    i = pl.program_id(0)
    tm = hp_ref.shape[0]

    def copies(t):
        src = hp_ref.at[pl.ds(t, 1), :]
        return (pltpu.make_async_copy(src, xs_ref.at[pl.ds(p1_ref[i * tm + t], 1), :], sem.at[0]),
                pltpu.make_async_copy(src, xs_ref.at[pl.ds(p2_ref[i * tm + t], 1), :], sem.at[1]))

    _run_row_copies(copies, tm)


def _scatter(pos1, pos2, hp, xs_zero, *, tm):
    s_len, half = hp.shape
    grid_spec = pltpu.PrefetchScalarGridSpec(
        num_scalar_prefetch=2,
        grid=(s_len // tm,),
        in_specs=[
            pl.BlockSpec((tm, half), lambda i, p1, p2: (i, 0)),
            pl.BlockSpec(memory_space=pl.ANY),
        ],
        out_specs=pl.BlockSpec(memory_space=pl.ANY),
        scratch_shapes=[pltpu.SemaphoreType.DMA((2,))],
    )
    return pl.pallas_call(
        _scatter_kernel,
        grid_spec=grid_spec,
        out_shape=jax.ShapeDtypeStruct(xs_zero.shape, jnp.uint32),
        input_output_aliases={3: 0},
        compiler_params=_cparams(("arbitrary",)),
        name="moe_scatter",
    )(pos1, pos2, hp, xs_zero)


def _experts_kernel(e_ref, v_ref, xs_ref, wg_ref, wu_ref, wd_ref, o_ref, xb_scr):
    j = pl.program_id(0)
    f = pl.program_id(1)
    half = xs_ref.shape[1]

    @pl.when(f == 0)
    def _():
        o_ref[...] = jnp.zeros_like(o_ref)

    @pl.when(v_ref[j] > 0)
    def _():
        @pl.when(f == 0)
        def _():
            lo, hi = _unpack_bf16_pairs(xs_ref[...])
            xb_scr[:, :half] = lo
            xb_scr[:, half:] = hi

        o_ref[...] += _swiglu_step(xb_scr[...], wg_ref[0], wu_ref[0], wd_ref[0])


def _experts(item_e, item_valid, xs, wg, wu, wd, *, r, tf):
    rows, half = xs.shape
    n_exp, d, f = wg.shape
    n_items = rows // r
    n_f = f // tf

    def f_idx(fi, j, v):
        return jnp.where(v[j] > 0, fi, n_f - 1)

    grid_spec = pltpu.PrefetchScalarGridSpec(
        num_scalar_prefetch=2,
        grid=(n_items, n_f),
        in_specs=[
            pl.BlockSpec((r, half), lambda j, fi, e, v: (j, 0), pipeline_mode=pl.Buffered(1)),
            pl.BlockSpec((1, d, tf), lambda j, fi, e, v: (e[j], 0, f_idx(fi, j, v))),
            pl.BlockSpec((1, d, tf), lambda j, fi, e, v: (e[j], 0, f_idx(fi, j, v))),
            pl.BlockSpec((1, tf, d), lambda j, fi, e, v: (e[j], f_idx(fi, j, v), 0)),
        ],
        out_specs=pl.BlockSpec((r, d), lambda j, fi, e, v: (j, 0)),
        scratch_shapes=[pltpu.VMEM((r, d), BF16)],
    )
    return pl.pallas_call(
        _experts_kernel,
        grid_spec=grid_spec,
        out_shape=jax.ShapeDtypeStruct((rows, d), F32),
        compiler_params=_cparams(("arbitrary", "arbitrary")),
        name="moe_experts",
    )(item_e, item_valid, xs, wg, wu, wd)


def _combine_kernel(p1_ref, p2_ref, x_ref, route_ref, o_hbm, y_ref, a_scr, b_scr, sem):
    i = pl.program_id(0)
    tm = x_ref.shape[0]

    def copies(t):
        return (pltpu.make_async_copy(o_hbm.at[pl.ds(p1_ref[i * tm + t], 1), :],
                                      a_scr.at[pl.ds(t, 1), :], sem.at[0]),
                pltpu.make_async_copy(o_hbm.at[pl.ds(p2_ref[i * tm + t], 1), :],
                                      b_scr.at[pl.ds(t, 1), :], sem.at[1]))

    _run_row_copies(copies, tm)
    route = route_ref[...]
    g1 = route[:, R_G1:R_G1 + 1]
    g2 = route[:, R_G2:R_G2 + 1]
    y_ref[...] = x_ref[...] + g1 * a_scr[...] + g2 * b_scr[...]


def _combine(pos1, pos2, x, route, o_sorted, *, tm):
    s_len, d = x.shape
    grid_spec = pltpu.PrefetchScalarGridSpec(
        num_scalar_prefetch=2,
        grid=(s_len // tm,),
        in_specs=[
            pl.BlockSpec((tm, d), lambda i, p1, p2: (i, 0)),
            pl.BlockSpec((tm, LANES), lambda i, p1, p2: (i, 0)),
            pl.BlockSpec(memory_space=pl.ANY),
        ],
        out_specs=pl.BlockSpec((tm, d), lambda i, p1, p2: (i, 0)),
        scratch_shapes=[
            pltpu.VMEM((tm, d), F32),
            pltpu.VMEM((tm, d), F32),
            pltpu.SemaphoreType.DMA((2,)),
        ],
    )
    return pl.pallas_call(
        _combine_kernel,
        grid_spec=grid_spec,
        out_shape=jax.ShapeDtypeStruct((s_len, d), F32),
        compiler_params=_cparams(("arbitrary",)),
        name="moe_combine",
    )(pos1, pos2, x, route, o_sorted)


def _rotary_tables(s_len, rot_dim, period):
    half = rot_dim // 2
    inv_freq = 1.0 / (ROPE_THETA ** (jnp.arange(half, dtype=F32) * 2.0 / rot_dim))
    ang = jnp.arange(s_len, dtype=F32)[:, None] * inv_freq[None, :]
    cos, sin = jnp.cos(ang), jnp.sin(ang)
    pad = period - rot_dim
    cos_p = jnp.concatenate([cos, cos, jnp.ones((s_len, pad), F32)], axis=1)
    sin_p = jnp.concatenate([-sin, sin, jnp.zeros((s_len, pad), F32)], axis=1)
    reps = LANES // period
    return jnp.tile(cos_p, (1, reps)), jnp.tile(sin_p, (1, reps))


def _split_w_in(w_in):
    seg_a = N_HEADS_A * HEAD_DIM
    seg_i = N_IDX_HEADS * IDX_DIM
    seg_b = N_HEADS_B * HEAD_DIM
    sizes = (seg_a, seg_a, seg_a, seg_i, IDX_DIM, N_IDX_HEADS, seg_b, seg_b, seg_b, N_HEADS_B, seg_b)
    offs = np.cumsum(sizes)[:-1].tolist()
    return jnp.split(w_in, offs, axis=1)


def _moe_tables(counts, e1, e2, k1, k2, *, r, n_items):
    n_exp = counts.shape[0]
    blocks = (counts + r - 1) // r
    bend = jnp.cumsum(blocks)
    bstart = bend - blocks
    j = jnp.arange(n_items, dtype=jnp.int32)
    item_e = jnp.sum((j[:, None] >= bend[None, :]).astype(jnp.int32), axis=1)
    valid = item_e < n_exp
    last_e = jnp.minimum(item_e[jnp.maximum(bend[-1] - 1, 0)], n_exp - 1)
    item_e = jnp.where(valid, item_e, last_e).astype(jnp.int32)
    pos1 = bstart[e1] * r + k1
    pos2 = bstart[e2] * r + k2
    return item_e, valid.astype(jnp.int32), pos1.astype(jnp.int32), pos2.astype(jnp.int32)


def _moe_block_rows(n_assign, n_exp):
    r = -(-n_assign // (2 * n_exp))
    r = -(-(r + r // 16) // 64) * 64
    return max(r, 64)


class _Tiles(NamedTuple):
    attn: int
    dsa_q: int
    out_rows: int
    ffn_rows: int
    ffn_cols: int
    pool_rows: int
    moe_rows: int
    expert_cols: int


def _tiles(s_len, d_ff, d_ff_expert):
    return _Tiles(attn=_tile(s_len, 512), dsa_q=_tile(s_len, 256), out_rows=_tile(s_len, 512),
                  ffn_rows=_tile(s_len, 1024), ffn_cols=_tile(d_ff, 256),
                  pool_rows=_tile(s_len, 512), moe_rows=_tile(s_len, 256),
                  expert_cols=_tile(d_ff_expert, 512))


def kernel(x, attn_norm_e, w_in_e, b_forget_e, q_norm_a_e, k_norm_a_e, q_norm_b_e, k_norm_b_e,
           w_out_e, ffn_norm_e, w_gate_e, w_up_e, w_down_e, pool_norm_o, w_pool_o, pool_scale_o,
           moe_norm_o, router_o, w_gate_o, w_up_o, w_down_o):
    b, s_len, d = x.shape
    assert b == 1
    x0 = x[0]
    tiles = _tiles(s_len, w_gate_e.shape[2], w_gate_o.shape[3])
    seg = N_HEADS_A * HEAD_DIM
    assert seg == N_IDX_HEADS * IDX_DIM == N_HEADS_B * HEAD_DIM
    assert N_IDX_HEADS <= FB_OFF - WI_OFF and N_HEADS_B <= LANES - FB_OFF
    row = lambda v: v.reshape(1, -1)

    qa, ka, va, qi, ki, wi, qb, kb, vb, fb, gb = _split_w_in(w_in_e[0])
    w_main = jnp.concatenate([qa, ka, va, qi, qb, kb, vb, gb], axis=1).astype(BF16)
    w_small = jnp.concatenate(
        [ki, wi, jnp.zeros((d, FB_OFF - WI_OFF - N_IDX_HEADS), F32), fb,
         jnp.zeros((d, LANES - FB_OFF - N_HEADS_B), F32)], axis=1).astype(BF16)
    bf_pad = jnp.zeros((1, LANES), F32).at[0, FB_OFF:FB_OFF + N_HEADS_B].set(b_forget_e[0])
    scale = HEAD_DIM ** -0.5 * LOG2_E
    gains = jnp.stack([q_norm_a_e[0] * scale, k_norm_a_e[0], q_norm_b_e[0] * scale, k_norm_b_e[0]])
    tabs = _rotary_tables(s_len, ROT_DIM, HEAD_DIM) + _rotary_tables(s_len, IDX_ROT_DIM, IDX_DIM)
    main, small, v_t, c_neg, ki_rot = _in_proj(x0, row(attn_norm_e[0]), w_main, w_small, bf_pad, gains,
                                               tabs, n_seg=8, seg=seg, tm=tiles.attn)
    wi_t = small[:, WI_OFF:WI_OFF + N_IDX_HEADS].T

    top_k = min(TOPK_MAX, s_len // 4)
    o_a = _dsa_t(main, v_t, ki_rot, wi_t, tq=tiles.dsa_q, tk=tiles.attn, top_k=top_k)
    o_b = _fox_t(main, v_t, c_neg, t=tiles.attn)
    w_out = w_out_e[0].astype(BF16)
    x1 = _out_proj(x0, o_a, o_b, w_out[:seg], w_out[seg:], tm=tiles.out_rows)

    x2 = _ffn(x1, row(ffn_norm_e[0]), w_gate_e[0], w_up_e[0], w_down_e[0],
              tm=tiles.ffn_rows, tf=tiles.ffn_cols)

    n_exp = router_o.shape[2]
    r_f32 = jnp.zeros((d, LANES), F32).at[:, :n_exp].set(router_o[0])
    r_hi = r_f32.astype(BF16)
    r_pad = jnp.stack([r_hi, (r_f32 - r_hi.astype(F32)).astype(BF16)])
    x3, h3_packed, route, counts = _pool_route(
        x2, row(pool_norm_o[0]), w_pool_o[0].astype(BF16), row(pool_scale_o[0]),
        row(moe_norm_o[0]), r_pad, tm=tiles.pool_rows, n_exp=n_exp)

    r_rows = _moe_block_rows(2 * s_len, n_exp)
    n_items = (2 * s_len) // r_rows + n_exp
    as_int = lambda c: route[:, c].astype(jnp.int32)
    item_e, item_valid, pos1, pos2 = _moe_tables(
        counts[0, :n_exp].astype(jnp.int32), as_int(R_E1), as_int(R_E2), as_int(R_K1), as_int(R_K2),
        r=r_rows, n_items=n_items)
    xs_zero = jnp.zeros((n_items * r_rows, d // 2), jnp.uint32)
    xs = _scatter(pos1, pos2, h3_packed, xs_zero, tm=tiles.moe_rows)
    o_sorted = _experts(item_e, item_valid, xs, w_gate_o[0], w_up_o[0], w_down_o[0],
                        r=r_rows, tf=tiles.expert_cols)
    y = _combine(pos1, pos2, x3, route, o_sorted, tm=tiles.moe_rows)
    return y[None]
```

```python
import functools
from typing import NamedTuple

import jax
import jax.numpy as jnp
import numpy as np
from jax import lax
from jax.experimental import pallas as pl
from jax.experimental.pallas import tpu as pltpu

HEAD_DIM = 128
N_HEADS_A = 8
N_HEADS_B = 8
N_IDX_HEADS = 16
IDX_DIM = 64
ROT_DIM = HEAD_DIM // 4
IDX_ROT_DIM = IDX_DIM // 4
ROPE_THETA = 500000.0
CHUNK = 64
TOPK_MAX = 256
POOL_WINDOWS = (2, 4, 8, 16)
RMS_EPS = 1e-6

LANES = 128
BF16_SUBLANES = 16
MXU_COLS = 256
VMEM_LIMIT_BYTES = 56 * 1024 * 1024

WI_OFF = IDX_DIM
FB_OFF = 80

LOG2_E = 1.4426950408889634
NEG_BIG = -1e30
INT_MIN = -(2 ** 31)
INT_MAX = 2 ** 31 - 1
KEY_NONE = INT_MIN + 1
HALF_MIN = -(2 ** 15)
HALF_MAX = 2 ** 15 - 1

F32 = jnp.float32
BF16 = jnp.bfloat16


def _cparams(sem):
    return pltpu.CompilerParams(dimension_semantics=sem, vmem_limit_bytes=VMEM_LIMIT_BYTES)


def _tile(n, pref):
    t = min(n, pref)
    while n % t:
        t //= 2
    return t


def _rms(x, g):
    ms = jnp.mean(x * x, axis=-1, keepdims=True)
    return x * lax.rsqrt(ms + RMS_EPS) * g


def _rotate(y, cos_t, sin_t, half, period):
    lane = lax.broadcasted_iota(jnp.int32, y.shape, 1)
    up = pltpu.roll(y, LANES - half, 1)
    dn = pltpu.roll(y, half, 1)
    rot = jnp.where((lane & (period - 1)) < half, up, dn)
    return y * cos_t + rot * sin_t


def _in_proj_kernel(x_ref, g_ref, wm_ref, ws_ref, bf_ref, gains_ref, cosa_ref, sina_ref,
                    cosi_ref, sini_ref, main_ref, small_ref, vt_ref, cneg_ref, ki_ref, h_scr,
                    carry_scr, *, n_seg, idx_scale, n_fb):
    i = pl.program_id(0)
    j = pl.program_id(1)
    tm = h_scr.shape[0]
    seg = wm_ref.shape[1]

    @pl.when(j == 0)
    def _():
        h_scr[...] = _rms(x_ref[...], g_ref[...]).astype(BF16)

    def head_norm(slab, gain):
        ms = jnp.mean(slab * slab, axis=-1, keepdims=True)
        return slab * lax.rsqrt(ms + RMS_EPS) * gain

    def project(epilogue):
        for c0 in range(0, seg, MXU_COLS):
            acc = jnp.dot(h_scr[...], wm_ref[:, c0:c0 + MXU_COLS], preferred_element_type=F32)
            epilogue(acc, c0)

    def per_slab(fn):
        def epilogue(acc, c0):
            for s0 in range(0, MXU_COLS, LANES):
                y = fn(acc[:, s0:s0 + LANES])
                main_ref[0, :, c0 + s0:c0 + s0 + LANES] = y.astype(BF16)
        return epilogue

    @pl.when(j < 2)
    def _():
        gain = gains_ref[pl.ds(j, 1), :]
        project(per_slab(lambda y: _rotate(head_norm(y, gain), cosa_ref[...], sina_ref[...],
                                           ROT_DIM // 2, LANES)))

    @pl.when((j == 4) | (j == 5))
    def _():
        gain = gains_ref[pl.ds(j - 2, 1), :]
        project(per_slab(lambda y: head_norm(y, gain)))

    @pl.when((j == 2) | (j == 6))
    def _():
        def epilogue(acc, c0):
            main_ref[0, :, c0:c0 + MXU_COLS] = acc.astype(BF16)
            vt_ref[0, 0, c0:c0 + MXU_COLS, :] = acc.T.astype(BF16)
        project(epilogue)

    @pl.when(j == 3)
    def _():
        project(per_slab(lambda y: _rotate(y, cosi_ref[...], sini_ref[...], IDX_ROT_DIM // 2, IDX_DIM)))

    @pl.when(j == 7)
    def _():
        project(per_slab(jax.nn.sigmoid))

    @pl.when(j == n_seg)
    def _():
        acc = jnp.dot(h_scr[...], ws_ref[...], preferred_element_type=F32)
        lane = lax.broadcasted_iota(jnp.int32, acc.shape, 1)
        is_ki = lane < WI_OFF
        cos_k = jnp.where(is_ki, cosi_ref[...], 1.0)
        sin_k = jnp.where(is_ki, sini_ref[...], 0.0)
        ki = _rotate(acc, cos_k, sin_k, IDX_ROT_DIM // 2, IDX_DIM)
        wi = acc * idx_scale
        z = acc + bf_ref[...]
        ls = jnp.minimum(z, 0.0) - jnp.log1p(jnp.exp(-jnp.abs(z)))
        is_fb = (lane >= FB_OFF) & (lane < FB_OFF + n_fb)
        ls = jnp.where(is_fb, ls, 0.0)
        row = lax.broadcasted_iota(jnp.int32, (tm, tm), 0)
        col = lax.broadcasted_iota(jnp.int32, (tm, tm), 1)
        tri = (col <= row).astype(BF16)
        p1 = ls.astype(BF16)
        r1 = ls - p1.astype(F32)
        p2 = r1.astype(BF16)
        p3 = (r1 - p2.astype(F32)).astype(BF16)
        c = (jnp.dot(tri, p1, preferred_element_type=F32)
             + jnp.dot(tri, p2, preferred_element_type=F32)
             + jnp.dot(tri, p3, preferred_element_type=F32))

        @pl.when(i == 0)
        def _():
            carry_scr[...] = jnp.zeros_like(carry_scr)

        c = c + carry_scr[...]
        carry_scr[...] = c[tm - 1:tm, :]
        small_ref[...] = jnp.where(is_ki, ki, jnp.where(lane < FB_OFF, wi, c))
        ki_ref[...] = jnp.where(is_ki, ki, 0.0).astype(BF16)
        for hb in range(n_fb):
            cneg_ref[hb] = jnp.broadcast_to(-LOG2_E * c[:, FB_OFF + hb:FB_OFF + hb + 1], (tm, LANES))


def _in_proj(x, g, w_main, w_small, bf_pad, gains, tabs, *, n_seg, seg, tm):
    s_len, d = x.shape
    cosa, sina, cosi, sini = tabs
    grid = (s_len // tm, n_seg + 1)
    row_spec = lambda w: pl.BlockSpec((tm, w), lambda i, j: (i, 0))
    kern = functools.partial(_in_proj_kernel, n_seg=n_seg,
                             idx_scale=float((N_IDX_HEADS * IDX_DIM) ** -0.5), n_fb=N_HEADS_B)
    return pl.pallas_call(
        kern,
        grid=grid,
        in_specs=[
            pl.BlockSpec((tm, d), lambda i, j: (i, 0), pipeline_mode=pl.Buffered(1)),
            pl.BlockSpec((1, d), lambda i, j: (0, 0)),
            pl.BlockSpec((d, seg), lambda i, j: (0, jnp.minimum(j, n_seg - 1))),
            pl.BlockSpec((d, LANES), lambda i, j: (0, 0)),
            pl.BlockSpec((1, LANES), lambda i, j: (0, 0)),
            pl.BlockSpec((4, LANES), lambda i, j: (0, 0)),
            row_spec(LANES), row_spec(LANES), row_spec(LANES), row_spec(LANES),
        ],
        out_specs=[
            pl.BlockSpec((1, tm, seg), lambda i, j: (jnp.minimum(j, n_seg - 1), i, 0)),
            pl.BlockSpec((tm, LANES), lambda i, j: (i, 0)),
            pl.BlockSpec((1, 1, seg, tm), lambda i, j: (jnp.where(j <= 2, 0, 1), i, 0, 0)),
            pl.BlockSpec((N_HEADS_B, tm, LANES), lambda i, j: (0, i, 0)),
            pl.BlockSpec((tm, LANES), lambda i, j: (i, 0)),
        ],
        out_shape=[
            jax.ShapeDtypeStruct((n_seg, s_len, seg), BF16),
            jax.ShapeDtypeStruct((s_len, LANES), F32),
            jax.ShapeDtypeStruct((2, s_len // tm, seg, tm), BF16),
            jax.ShapeDtypeStruct((N_HEADS_B, s_len, LANES), F32),
            jax.ShapeDtypeStruct((s_len, LANES), BF16),
        ],
        scratch_shapes=[
            pltpu.VMEM((tm, d), BF16),
            pltpu.VMEM((1, LANES), F32),
        ],
        compiler_params=_cparams(("arbitrary", "arbitrary")),
        name="in_proj",
    )(x, g, w_main, w_small, bf_pad, gains, cosa, sina, cosi, sini)


def _transpose_heads(src, dst_scr):
    for s in range(src.shape[1] // LANES):
        sl = slice(s * LANES, (s + 1) * LANES)
        dst_scr[sl, :] = src[:, sl].astype(F32).T.astype(BF16)


def _logits_pass(h, z_t, z_scr, m_scr, mnew_scr):
    z_scr[h] = z_t
    mnew_scr[h:h + 1, :] = jnp.maximum(m_scr[h:h + 1, :], jnp.max(z_t, axis=0, keepdims=True))


def _softmax_pass(h, v_t, z_scr, m_scr, mnew_scr, l_scr, acc_scr):
    hs = slice(h * HEAD_DIM, (h + 1) * HEAD_DIM)
    m_new = mnew_scr[h:h + 1, :]
    alpha = jnp.exp2(m_scr[h:h + 1, :] - m_new)
    p_t = jnp.exp2(z_scr[h] - m_new)
    l_scr[h:h + 1, :] = alpha * l_scr[h:h + 1, :] + jnp.sum(p_t, axis=0, keepdims=True)
    acc_scr[hs, :] = alpha * acc_scr[hs, :] + jnp.dot(v_t, p_t.astype(BF16),
                                                      preferred_element_type=F32)
    m_scr[h:h + 1, :] = m_new


def _init_attn_state(m_scr, l_scr, acc_scr):
    m_scr[...] = jnp.full_like(m_scr, NEG_BIG)
    l_scr[...] = jnp.zeros_like(l_scr)
    acc_scr[...] = jnp.zeros_like(acc_scr)


def _dsa_t_kernel(qa_ref, qi_ref, wit_ref, ki_ref, ka_ref, vat_ref, o_ref,
                  key_scr, half_scr, qat_scr, qit_scr, z_scr, m_scr, mnew_scr, l_scr, acc_scr,
                  *, tq, tk, top_k, n_heads, n_idx):
    i = pl.program_id(0)
    n_kt = ((i + 1) * tq + tk - 1) // tk
    t_glob = i * tq + lax.broadcasted_iota(jnp.int32, (1, tq), 1)
    adm_len = (t_glob // CHUNK + 1) * CHUNK
    key_pos = lax.broadcasted_iota(jnp.int32, (tk, 1), 0)

    _transpose_heads(qa_ref[0], qat_scr)
    _transpose_heads(qi_ref[0], qit_scr)

    def score_tile(kt, carry):
        k0 = pl.multiple_of(kt * tk, tk)
        k_idx = ki_ref[pl.ds(k0, tk), 0:IDX_DIM]
        acc = jnp.zeros((tk, tq), F32)
        for h in range(n_idx):
            q_t = qit_scr[h * IDX_DIM:(h + 1) * IDX_DIM, :]
            rel = jnp.maximum(jnp.dot(k_idx, q_t, preferred_element_type=F32), 0.0)
            acc = acc + rel * wit_ref[h:h + 1, :]
        bits = lax.bitcast_convert_type(acc, jnp.int32)
        key = jnp.where(bits < 0, bits ^ INT_MAX, bits)
        key = jnp.where(k0 + key_pos < adm_len, key, KEY_NONE)
        key_scr[kt] = key
        half_scr[kt] = (key >> 16).astype(jnp.int16)
        return carry

    lax.fori_loop(0, n_kt, score_tile, 0)

    def count(pred):
        def body(kt, cnt):
            hit = jnp.where(pred(key_scr[kt], kt * tk + key_pos), 1, 0)
            return cnt + jnp.sum(hit.reshape(tk // 8, 8, tq), axis=0)

        cnt = lax.fori_loop(0, n_kt, body, jnp.zeros((8, tq), jnp.int32))
        return jnp.sum(cnt, axis=0, keepdims=True)

    def count_half_ge(mid):
        mid16 = mid.astype(jnp.int16)

        def body(kt, cnt):
            hit = jnp.where(half_scr[kt] >= mid16, jnp.int16(1), jnp.int16(0))
            parts = [hit[r * 16:(r + 1) * 16] for r in range(tk // 16)]
            while len(parts) > 1:
                parts = [parts[a] + parts[a + 1] for a in range(0, len(parts), 2)]
            return cnt + parts[0].astype(jnp.int32)

        cnt = lax.fori_loop(0, n_kt, body, jnp.zeros((16, tq), jnp.int32))
        return jnp.sum(cnt, axis=0, keepdims=True)

    def bisect_half(need_k, c_all):
        def step(_, state):
            lo, hi, c_lo, c_hi = state
            mid = (lo + hi) >> 1
            c_mid = count_half_ge(mid)
            take = c_mid >= need_k
            return (jnp.where(take, mid, lo), jnp.where(take, hi, mid),
                    jnp.where(take, c_mid, c_lo), jnp.where(take, c_hi, c_mid))

        x, _, c_x, c_above = lax.fori_loop(0, 16, step,
                                           (row(HALF_MIN), row(HALF_MAX + 1), c_all, row(0)))
        return x, c_x, c_above

    row = lambda v: jnp.full((1, tq), v, jnp.int32)
    thr_hi, c_ge_hi, c_gt_hi = bisect_half(top_k, n_kt * tk + row(0))
    thr_hi16 = thr_hi.astype(jnp.int16)

    def low_half_tile(kt, carry):
        low = ((key_scr[kt] & 0xFFFF) + HALF_MIN).astype(jnp.int16)
        half_scr[kt] = jnp.where(half_scr[kt] == thr_hi16, low, jnp.int16(HALF_MIN))
        return carry

    lax.fori_loop(0, n_kt, low_half_tile, 0)
    thr_lo, c_ge_lo, c_gt_lo = bisect_half(top_k - c_gt_hi, c_ge_hi - c_gt_hi)
    thr = thr_hi * 65536 + (thr_lo - HALF_MIN)
    c_ge = c_gt_hi + c_ge_lo
    c_gt = c_gt_hi + c_gt_lo
    short = thr <= KEY_NONE
    thr = jnp.maximum(thr, KEY_NONE)
    need = top_k - c_gt

    def tie_cut():
        def step(_, lohi):
            lo, hi = lohi
            mid = (lo + hi) >> 1
            ok = count(lambda k, pos: (k == thr) & (pos <= mid)) >= need
            return jnp.where(ok, lo, mid), jnp.where(ok, mid, hi)

        n_steps = int(np.ceil(np.log2(key_scr.shape[0] * tk))) + 1
        return lax.fori_loop(0, n_steps, step, (row(-1), n_kt * tk - 1 + row(0)))[1]

    has_tie = jnp.max(jnp.where((c_ge > top_k) & ~short, 1.0, 0.0)) > 0.5
    cut = lax.cond(has_tie, tie_cut, lambda: row(INT_MAX))
    cut = jnp.where(short, -1, cut)

    _init_attn_state(m_scr, l_scr, acc_scr)

    def attend(kt, carry):
        k0 = pl.multiple_of(kt * tk, tk)
        keys = key_scr[kt]
        sel = (keys > thr) | ((keys == thr) & (kt * tk + key_pos <= cut))
        bias = jnp.where(sel, 0.0, NEG_BIG).astype(F32)
        for h in range(n_heads):
            hs = slice(h * HEAD_DIM, (h + 1) * HEAD_DIM)
            z_t = jnp.dot(ka_ref[0, pl.ds(k0, tk), hs], qat_scr[hs, :],
                          preferred_element_type=F32) + bias
            _logits_pass(h, z_t, z_scr, m_scr, mnew_scr)
        for h in range(n_heads):
            hs = slice(h * HEAD_DIM, (h + 1) * HEAD_DIM)
            _softmax_pass(h, vat_ref[0, kt, hs, :], z_scr, m_scr, mnew_scr, l_scr, acc_scr)
        return carry

    lax.fori_loop(0, n_kt, attend, 0)
    for h in range(n_heads):
        hs = slice(h * HEAD_DIM, (h + 1) * HEAD_DIM)
        o_ref[:, hs] = (acc_scr[hs, :] / l_scr[h:h + 1, :]).T.astype(BF16)


def _dsa_t(main, vt, ki, wi_t, *, tq, tk, top_k):
    n_seg, s_len, seg = main.shape
    assert vt.shape == (2, s_len // tk, seg, tk)
    kern = functools.partial(_dsa_t_kernel, tq=tq, tk=tk, top_k=top_k, n_heads=N_HEADS_A,
                             n_idx=N_IDX_HEADS)
    resident = functools.partial(pl.BlockSpec, pipeline_mode=pl.Buffered(1))
    return pl.pallas_call(
        kern,
        grid=(s_len // tq,),
        in_specs=[
            pl.BlockSpec((1, tq, seg), lambda i: (0, i, 0)),
            pl.BlockSpec((1, tq, seg), lambda i: (3, i, 0)),
            pl.BlockSpec((N_IDX_HEADS, tq), lambda i: (0, i)),
            resident((s_len, LANES), lambda i: (0, 0)),
            resident((1, s_len, seg), lambda i: (1, 0, 0)),
            resident((1, s_len // tk, seg, tk), lambda i: (0, 0, 0, 0)),
        ],
        out_specs=pl.BlockSpec((tq, seg), lambda i: (i, 0)),
        out_shape=jax.ShapeDtypeStruct((s_len, seg), BF16),
        scratch_shapes=[
            pltpu.VMEM((s_len // tk, tk, tq), jnp.int32),
            pltpu.VMEM((s_len // tk, tk, tq), jnp.int16),
            pltpu.VMEM((seg, tq), BF16),
            pltpu.VMEM((seg, tq), BF16),
            pltpu.VMEM((N_HEADS_A, tk, tq), F32),
            pltpu.VMEM((N_HEADS_A, tq), F32),
            pltpu.VMEM((N_HEADS_A, tq), F32),
            pltpu.VMEM((N_HEADS_A, tq), F32),
            pltpu.VMEM((seg, tq), F32),
        ],
        compiler_params=_cparams(("arbitrary",)),
        name="dsa",
    )(main, main, wi_t, ki, main, vt)


def _fox_t_kernel(qt_ref, kt_ref, q_ref, k_ref, vt_ref, gate_ref, cneg_ref, o_ref,
                  qt_scr, z_scr, m_scr, mnew_scr, l_scr, acc_scr, *, t, n_heads):
    p = pl.program_id(0)
    qi = qt_ref[p]
    ki = kt_ref[p]

    @pl.when(ki == 0)
    def _():
        _transpose_heads(q_ref[0], qt_scr)
        _init_attn_state(m_scr, l_scr, acc_scr)

    def update(causal):
        key_i = lax.broadcasted_iota(jnp.int32, (t, LANES), 0)
        qry_i = lax.broadcasted_iota(jnp.int32, (t, LANES), 1)
        for h in range(n_heads):
            hs = slice(h * HEAD_DIM, (h + 1) * HEAD_DIM)
            z_t = jnp.dot(k_ref[0, :, hs], qt_scr[hs, :], preferred_element_type=F32)
            cols = []
            for c in range(t // LANES):
                zc = z_t[:, c * LANES:(c + 1) * LANES] + cneg_ref[h]
                if causal:
                    zc = jnp.where(key_i <= qry_i + c * LANES, zc, NEG_BIG)
                cols.append(zc)
            _logits_pass(h, jnp.concatenate(cols, axis=1), z_scr, m_scr, mnew_scr)
        for h in range(n_heads):
            hs = slice(h * HEAD_DIM, (h + 1) * HEAD_DIM)
            _softmax_pass(h, vt_ref[0, 0, hs, :], z_scr, m_scr, mnew_scr, l_scr, acc_scr)

    @pl.when(ki < qi)
    def _():
        update(causal=False)

    @pl.when(ki == qi)
    def _():
        update(causal=True)
        for h in range(n_heads):
            hs = slice(h * HEAD_DIM, (h + 1) * HEAD_DIM)
            o = (acc_scr[hs, :] / l_scr[h:h + 1, :]).T
            o_ref[:, hs] = (o * gate_ref[0, :, hs].astype(F32)).astype(BF16)


def _fox_t(main, vt, cneg, *, t):
    n_seg, s_len, seg = main.shape
    assert vt.shape == (2, s_len // t, seg, t)
    nq = s_len // t
    pairs = [(a, b) for a in range(nq) for b in range(a + 1)]
    q_tab = jnp.asarray([a for a, _ in pairs], jnp.int32)
    k_tab = jnp.asarray([b for _, b in pairs], jnp.int32)
    kern = functools.partial(_fox_t_kernel, t=t, n_heads=N_HEADS_B)
    grid_spec = pltpu.PrefetchScalarGridSpec(
        num_scalar_prefetch=2,
        grid=(len(pairs),),
        in_specs=[
            pl.BlockSpec((1, t, seg), lambda p, qt, kt: (4, qt[p], 0)),
            pl.BlockSpec((1, t, seg), lambda p, qt, kt: (5, kt[p], 0)),
            pl.BlockSpec((1, 1, seg, t), lambda p, qt, kt: (1, kt[p], 0, 0)),
            pl.BlockSpec((1, t, seg), lambda p, qt, kt: (7, qt[p], 0)),
            pl.BlockSpec((N_HEADS_B, t, LANES), lambda p, qt, kt: (0, kt[p], 0)),
        ],
        out_specs=pl.BlockSpec((t, seg), lambda p, qt, kt: (qt[p], 0)),
        scratch_shapes=[
            pltpu.VMEM((seg, t), BF16),
            pltpu.VMEM((N_HEADS_B, t, t), F32),
            pltpu.VMEM((N_HEADS_B, t), F32),
            pltpu.VMEM((N_HEADS_B, t), F32),
            pltpu.VMEM((N_HEADS_B, t), F32),
            pltpu.VMEM((seg, t), F32),
        ],
    )
    return pl.pallas_call(
        kern,
        grid_spec=grid_spec,
        out_shape=jax.ShapeDtypeStruct((s_len, seg), BF16),
        compiler_params=_cparams(("arbitrary",)),
        name="fox",
    )(q_tab, k_tab, main, main, vt, main, cneg)


def _out_proj_kernel(x_ref, oa_ref, ob_ref, wa_ref, wb_ref, y_ref):
    y = x_ref[...] + jnp.dot(oa_ref[...], wa_ref[...], preferred_element_type=F32)
    y_ref[...] = y + jnp.dot(ob_ref[...], wb_ref[...], preferred_element_type=F32)


def _out_proj(x, o_a, o_b, w_a, w_b, *, tm):
    s_len, d = x.shape
    seg = o_a.shape[1]
    resident = functools.partial(pl.BlockSpec, pipeline_mode=pl.Buffered(1))
    return pl.pallas_call(
        _out_proj_kernel,
        grid=(s_len // tm,),
        in_specs=[
            pl.BlockSpec((tm, d), lambda i: (i, 0)),
            pl.BlockSpec((tm, seg), lambda i: (i, 0)),
            pl.BlockSpec((tm, seg), lambda i: (i, 0)),
            resident((seg, d), lambda i: (0, 0)),
            resident((seg, d), lambda i: (0, 0)),
        ],
        out_specs=pl.BlockSpec((tm, d), lambda i: (i, 0)),
        out_shape=jax.ShapeDtypeStruct((s_len, d), F32),
        compiler_params=_cparams(("arbitrary",)),
        name="out_proj",
    )(x, o_a, o_b, w_a, w_b)


def _swiglu_step(h, wg, wu, wd):
    g = jnp.dot(h, wg.astype(BF16), preferred_element_type=F32)
    u = jnp.dot(h, wu.astype(BF16), preferred_element_type=F32)
    a = (g * jax.nn.sigmoid(g) * u).astype(BF16)
    return jnp.dot(a, wd.astype(BF16), preferred_element_type=F32)


def _ffn_kernel(x_ref, g_ref, wg_ref, wu_ref, wd_ref, y_ref, h_scr):
    @pl.when(pl.program_id(1) == 0)
    def _():
        x = x_ref[...]
        h_scr[...] = _rms(x, g_ref[...]).astype(BF16)
        y_ref[...] = x

    y_ref[...] += _swiglu_step(h_scr[...], wg_ref[...], wu_ref[...], wd_ref[...])


def _ffn(x, g, wg, wu, wd, *, tm, tf):
    s_len, d = x.shape
    f = wg.shape[1]
    return pl.pallas_call(
        _ffn_kernel,
        grid=(s_len // tm, f // tf),
        in_specs=[
            pl.BlockSpec((tm, d), lambda i, j: (i, 0), pipeline_mode=pl.Buffered(1)),
            pl.BlockSpec((1, d), lambda i, j: (0, 0)),
            pl.BlockSpec((d, tf), lambda i, j: (0, j)),
            pl.BlockSpec((d, tf), lambda i, j: (0, j)),
            pl.BlockSpec((tf, d), lambda i, j: (j, 0)),
        ],
        out_specs=pl.BlockSpec((tm, d), lambda i, j: (i, 0)),
        out_shape=jax.ShapeDtypeStruct((s_len, d), F32),
        scratch_shapes=[pltpu.VMEM((tm, d), BF16)],
        compiler_params=_cparams(("arbitrary", "arbitrary")),
        name="ffn",
    )(x, g, wg, wu, wd)


def _pool_kernel(x_ref, g_ref, w_ref, sc_ref, gm_ref, r_ref, y_ref, hp_ref, route_ref, cnt_ref,
                 ext_scr, carry_scr, *, halo, n_exp):
    i = pl.program_id(0)
    tm, d = x_ref.shape
    n_grp = len(POOL_WINDOWS)
    width = d // n_grp

    @pl.when(i == 0)
    def _():
        ext_scr[0:halo, :] = jnp.zeros((halo, d), F32)

    x = x_ref[...]
    ext_scr[halo:halo + tm, :] = _rms(x, g_ref[...])
    t_glob = i * tm + lax.broadcasted_iota(jnp.int32, (tm, 1), 0)
    for gi, win in enumerate(POOL_WINDOWS):
        cs = slice(gi * width, (gi + 1) * width)
        hn = ext_scr[halo:halo + tm, cs]
        tot = hn
        for dlt in range(1, win):
            tot = tot + ext_scr[halo - dlt:halo - dlt + tm, cs]
        count = jnp.minimum(t_glob + 1, win).astype(F32)
        pooled = (tot / count - hn).astype(BF16)
        y = jnp.dot(pooled, w_ref[gi], preferred_element_type=F32)
        y_ref[:, cs] = x[:, cs] + y * sc_ref[:, cs]
    ext_scr[0:halo, :] = ext_scr[tm:tm + halo, :]

    h = _rms(y_ref[...], gm_ref[...])
    hp_ref[...] = _pack_bf16_pairs(h)
    _route_rows(h, r_ref, route_ref, cnt_ref, carry_scr, n_exp=n_exp)


def _pool_route(x, g, w_pool, scale, g_moe, r_pad, *, tm, n_exp):
    s_len, d = x.shape
    halo = 16
    assert max(POOL_WINDOWS) <= halo
    n_grp, width, _ = w_pool.shape
    row_out = lambda w: pl.BlockSpec((tm, w), lambda i: (i, 0))
    return pl.pallas_call(
        functools.partial(_pool_kernel, halo=halo, n_exp=n_exp),
        grid=(s_len // tm,),
        in_specs=[
            pl.BlockSpec((tm, d), lambda i: (i, 0)),
            pl.BlockSpec((1, d), lambda i: (0, 0)),
            pl.BlockSpec((n_grp, width, width), lambda i: (0, 0, 0)),
            pl.BlockSpec((1, d), lambda i: (0, 0)),
            pl.BlockSpec((1, d), lambda i: (0, 0)),
            pl.BlockSpec((2, d, LANES), lambda i: (0, 0, 0)),
        ],
        out_specs=[row_out(d), row_out(d // 2), row_out(LANES),
                   pl.BlockSpec((1, LANES), lambda i: (0, 0))],
        out_shape=[
            jax.ShapeDtypeStruct((s_len, d), F32),
            jax.ShapeDtypeStruct((s_len, d // 2), jnp.uint32),
            jax.ShapeDtypeStruct((s_len, LANES), F32),
            jax.ShapeDtypeStruct((1, LANES), F32),
        ],
        scratch_shapes=[pltpu.VMEM((tm + halo, d), F32), pltpu.VMEM((1, LANES), F32)],
        compiler_params=_cparams(("arbitrary",)),
        name="pool_route",
    )(x, g, w_pool, scale, g_moe, r_pad)


R_E1, R_E2, R_G1, R_G2, R_K1, R_K2 = range(6)


def _route_rows(h, r_ref, route_ref, cnt_ref, carry_scr, *, n_exp):
    tm = h.shape[0]

    @pl.when(pl.program_id(0) == 0)
    def _():
        carry_scr[...] = jnp.zeros_like(carry_scr)

    h_hi = h.astype(BF16)
    h_lo = (h - h_hi.astype(F32)).astype(BF16)
    logits = (jnp.dot(h_hi, r_ref[0], preferred_element_type=F32)
              + jnp.dot(h_hi, r_ref[1], preferred_element_type=F32)
              + jnp.dot(h_lo, r_ref[0], preferred_element_type=F32))
    lane = lax.broadcasted_iota(jnp.int32, logits.shape, 1)
    logits = jnp.where(lane < n_exp, logits, -jnp.inf)
    lane_f = lane.astype(F32)
    v1 = jnp.max(logits, axis=1, keepdims=True)
    e1 = jnp.min(jnp.where(logits == v1, lane_f, float(LANES)), axis=1, keepdims=True)
    rest = jnp.where(lane_f == e1, -jnp.inf, logits)
    v2 = jnp.max(rest, axis=1, keepdims=True)
    e2 = jnp.min(jnp.where(rest == v2, lane_f, float(LANES)), axis=1, keepdims=True)
    ex = jnp.exp(v2 - v1)
    g1 = 1.0 / (1.0 + ex)
    g2 = ex / (1.0 + ex)
    oh1 = (lane_f == e1).astype(F32)
    oh2 = (lane_f == e2).astype(F32)
    both = oh1 + oh2
    row = lax.broadcasted_iota(jnp.int32, (tm, tm), 0)
    col = lax.broadcasted_iota(jnp.int32, (tm, tm), 1)
    tri = (col < row).astype(BF16)
    before = jnp.dot(tri, both.astype(BF16), preferred_element_type=F32) + carry_scr[...]
    k1 = jnp.sum(before * oh1, axis=1, keepdims=True)
    k2 = jnp.sum(before * oh2, axis=1, keepdims=True)
    carry_scr[...] = carry_scr[...] + jnp.sum(both, axis=0, keepdims=True)
    cnt_ref[...] = carry_scr[...]
    vals = (e1, e2, g1, g2, k1, k2)
    out = jnp.zeros(logits.shape, F32)
    for li, v in enumerate(vals):
        out = jnp.where(lane == li, v, out)
    route_ref[...] = out


def _pack_bf16_pairs(h):
    half = h.shape[1] // 2
    lo = lax.bitcast_convert_type(h[:, :half].astype(BF16).astype(F32), jnp.uint32)
    hi = lax.bitcast_convert_type(h[:, half:].astype(BF16).astype(F32), jnp.uint32)
    return (lo >> 16) | (hi & jnp.uint32(0xFFFF0000))


def _unpack_bf16_pairs(u):
    lo = lax.bitcast_convert_type(u << 16, F32).astype(BF16)
    hi = lax.bitcast_convert_type(u & jnp.uint32(0xFFFF0000), F32).astype(BF16)
    return lo, hi


ROW_COPY_UNROLL = 8


def _run_row_copies(copies, n):
    def start(t, c):
        a, b = copies(t)
        a.start(priority=0)
        b.start(priority=1)
        return c

    def wait(t, c):
        a, b = copies(t)
        a.wait()
        b.wait()
        return c

    lax.fori_loop(0, n, start, 0, unroll=ROW_COPY_UNROLL)
    lax.fori_loop(0, n, wait, 0, unroll=ROW_COPY_UNROLL)


def _scatter_kernel(p1_ref, p2_ref, hp_ref, xs_in_ref, xs_ref, sem):
    del xs_in_ref
    i = pl.program_id(0)
    tm = hp_ref.shape[0]

    def copies(t):
        src = hp_ref.at[pl.ds(t, 1), :]
        return (pltpu.make_async_copy(src, xs_ref.at[pl.ds(p1_ref[i * tm + t], 1), :], sem.at[0]),
                pltpu.make_async_copy(src, xs_ref.at[pl.ds(p2_ref[i * tm + t], 1), :], sem.at[1]))

    _run_row_copies(copies, tm)


def _scatter(pos1, pos2, hp, xs_zero, *, tm):
    s_len, half = hp.shape
    grid_spec = pltpu.PrefetchScalarGridSpec(
        num_scalar_prefetch=2,
        grid=(s_len // tm,),
        in_specs=[
            pl.BlockSpec((tm, half), lambda i, p1, p2: (i, 0)),
            pl.BlockSpec(memory_space=pl.ANY),
        ],
        out_specs=pl.BlockSpec(memory_space=pl.ANY),
        scratch_shapes=[pltpu.SemaphoreType.DMA((2,))],
    )
    return pl.pallas_call(
        _scatter_kernel,
        grid_spec=grid_spec,
        out_shape=jax.ShapeDtypeStruct(xs_zero.shape, jnp.uint32),
        input_output_aliases={3: 0},
        compiler_params=_cparams(("arbitrary",)),
        name="moe_scatter",
    )(pos1, pos2, hp, xs_zero)


def _experts_kernel(e_ref, v_ref, xs_ref, wg_ref, wu_ref, wd_ref, o_ref, xb_scr):
    j = pl.program_id(0)
    f = pl.program_id(1)
    half = xs_ref.shape[1]

    @pl.when(f == 0)
    def _():
        o_ref[...] = jnp.zeros_like(o_ref)

    @pl.when(v_ref[j] > 0)
    def _():
        @pl.when(f == 0)
        def _():
            lo, hi = _unpack_bf16_pairs(xs_ref[...])
            xb_scr[:, :half] = lo
            xb_scr[:, half:] = hi

        o_ref[...] += _swiglu_step(xb_scr[...], wg_ref[0], wu_ref[0], wd_ref[0])


def _experts(item_e, item_rows, xs, wg, wu, wd, *, r, tf):
    rows, half = xs.shape
    n_exp, d, f = wg.shape
    n_items = rows // r
    n_f = f // tf

    def f_idx(fi, j, v):
        return jnp.where(v[j] > 0, fi, n_f - 1)

    grid_spec = pltpu.PrefetchScalarGridSpec(
        num_scalar_prefetch=2,
        grid=(n_items, n_f),
        in_specs=[
            pl.BlockSpec((r, half), lambda j, fi, e, v: (j, 0)),
            pl.BlockSpec((1, d, tf), lambda j, fi, e, v: (e[j], 0, f_idx(fi, j, v))),
            pl.BlockSpec((1, d, tf), lambda j, fi, e, v: (e[j], 0, f_idx(fi, j, v))),
            pl.BlockSpec((1, tf, d), lambda j, fi, e, v: (e[j], f_idx(fi, j, v), 0)),
        ],
        out_specs=pl.BlockSpec((r, d), lambda j, fi, e, v: (j, 0)),
        scratch_shapes=[pltpu.VMEM((r, d), BF16)],
    )
    return pl.pallas_call(
        _experts_kernel,
        grid_spec=grid_spec,
        out_shape=jax.ShapeDtypeStruct((rows, d), F32),
        compiler_params=_cparams(("arbitrary", "arbitrary")),
        name="moe_experts",
    )(item_e, item_rows, xs, wg, wu, wd)


def _combine_kernel(p1_ref, p2_ref, x_ref, route_ref, o_hbm, y_ref, a_scr, b_scr, sem):
    i = pl.program_id(0)
    tm = x_ref.shape[0]

    def copies(t):
        return (pltpu.make_async_copy(o_hbm.at[pl.ds(p1_ref[i * tm + t], 1), :],
                                      a_scr.at[pl.ds(t, 1), :], sem.at[0]),
                pltpu.make_async_copy(o_hbm.at[pl.ds(p2_ref[i * tm + t], 1), :],
                                      b_scr.at[pl.ds(t, 1), :], sem.at[1]))

    _run_row_copies(copies, tm)
    route = route_ref[...]
    g1 = route[:, R_G1:R_G1 + 1]
    g2 = route[:, R_G2:R_G2 + 1]
    y_ref[...] = x_ref[...] + g1 * a_scr[...] + g2 * b_scr[...]


def _combine(pos1, pos2, x, route, o_sorted, *, tm):
    s_len, d = x.shape
    grid_spec = pltpu.PrefetchScalarGridSpec(
        num_scalar_prefetch=2,
        grid=(s_len // tm,),
        in_specs=[
            pl.BlockSpec((tm, d), lambda i, p1, p2: (i, 0)),
            pl.BlockSpec((tm, LANES), lambda i, p1, p2: (i, 0)),
            pl.BlockSpec(memory_space=pl.ANY),
        ],
        out_specs=pl.BlockSpec((tm, d), lambda i, p1, p2: (i, 0)),
        scratch_shapes=[
            pltpu.VMEM((tm, d), F32),
            pltpu.VMEM((tm, d), F32),
            pltpu.SemaphoreType.DMA((2,)),
        ],
    )
    return pl.pallas_call(
        _combine_kernel,
        grid_spec=grid_spec,
        out_shape=jax.ShapeDtypeStruct((s_len, d), F32),
        compiler_params=_cparams(("arbitrary",)),
        name="moe_combine",
    )(pos1, pos2, x, route, o_sorted)


def _rotary_tables(s_len, rot_dim, period):
    half = rot_dim // 2
    inv_freq = 1.0 / (ROPE_THETA ** (jnp.arange(half, dtype=F32) * 2.0 / rot_dim))
    ang = jnp.arange(s_len, dtype=F32)[:, None] * inv_freq[None, :]
    cos, sin = jnp.cos(ang), jnp.sin(ang)
    pad = period - rot_dim
    cos_p = jnp.concatenate([cos, cos, jnp.ones((s_len, pad), F32)], axis=1)
    sin_p = jnp.concatenate([-sin, sin, jnp.zeros((s_len, pad), F32)], axis=1)
    reps = LANES // period
    return jnp.tile(cos_p, (1, reps)), jnp.tile(sin_p, (1, reps))


def _split_w_in(w_in):
    seg_a = N_HEADS_A * HEAD_DIM
    seg_i = N_IDX_HEADS * IDX_DIM
    seg_b = N_HEADS_B * HEAD_DIM
    sizes = (seg_a, seg_a, seg_a, seg_i, IDX_DIM, N_IDX_HEADS, seg_b, seg_b, seg_b, N_HEADS_B, seg_b)
    offs = np.cumsum(sizes)[:-1].tolist()
    return jnp.split(w_in, offs, axis=1)


def _moe_tables(counts, e1, e2, k1, k2, *, r, n_items):
    n_exp = counts.shape[0]
    blocks = (counts + r - 1) // r
    bend = jnp.cumsum(blocks)
    bstart = bend - blocks
    j = jnp.arange(n_items, dtype=jnp.int32)
    item_e = jnp.sum((j[:, None] >= bend[None, :]).astype(jnp.int32), axis=1)
    valid = item_e < n_exp
    e_c = jnp.minimum(item_e, n_exp - 1)
    item_rows = jnp.where(valid, jnp.clip(counts[e_c] - (j - bstart[e_c]) * r, 0, r), 0)
    last_e = jnp.minimum(item_e[jnp.maximum(bend[-1] - 1, 0)], n_exp - 1)
    item_e = jnp.where(valid, item_e, last_e).astype(jnp.int32)
    pos1 = bstart[e1] * r + k1
    pos2 = bstart[e2] * r + k2
    return item_e, item_rows.astype(jnp.int32), pos1.astype(jnp.int32), pos2.astype(jnp.int32)


def _moe_block_rows(n_assign, n_exp):
    share = -(-n_assign // (2 * n_exp))
    return -(-(share + 3 * share // 64) // BF16_SUBLANES) * BF16_SUBLANES


class _Tiles(NamedTuple):
    attn: int
    dsa_q: int
    out_rows: int
    ffn_rows: int
    ffn_cols: int
    pool_rows: int
    moe_rows: int
    expert_cols: int


def _tiles(s_len, d_ff, d_ff_expert):
    return _Tiles(attn=_tile(s_len, 512), dsa_q=_tile(s_len, 256), out_rows=_tile(s_len, 512),
                  ffn_rows=_tile(s_len, 1024), ffn_cols=_tile(d_ff, 256),
                  pool_rows=_tile(s_len, 512), moe_rows=_tile(s_len, 256),
                  expert_cols=_tile(d_ff_expert, 256))


def kernel(x, attn_norm_e, w_in_e, b_forget_e, q_norm_a_e, k_norm_a_e, q_norm_b_e, k_norm_b_e,
           w_out_e, ffn_norm_e, w_gate_e, w_up_e, w_down_e, pool_norm_o, w_pool_o, pool_scale_o,
           moe_norm_o, router_o, w_gate_o, w_up_o, w_down_o):
    b, s_len, d = x.shape
    assert b == 1
    x0 = x[0]
    tiles = _tiles(s_len, w_gate_e.shape[2], w_gate_o.shape[3])
    seg = N_HEADS_A * HEAD_DIM
    assert seg == N_IDX_HEADS * IDX_DIM == N_HEADS_B * HEAD_DIM
    assert N_IDX_HEADS <= FB_OFF - WI_OFF and N_HEADS_B <= LANES - FB_OFF
    row = lambda v: v.reshape(1, -1)

    qa, ka, va, qi, ki, wi, qb, kb, vb, fb, gb = _split_w_in(w_in_e[0])
    w_main = jnp.concatenate([qa, ka, va, qi, qb, kb, vb, gb], axis=1).astype(BF16)
    w_small = jnp.concatenate(
        [ki, wi, jnp.zeros((d, FB_OFF - WI_OFF - N_IDX_HEADS), F32), fb,
         jnp.zeros((d, LANES - FB_OFF - N_HEADS_B), F32)], axis=1).astype(BF16)
    bf_pad = jnp.zeros((1, LANES), F32).at[0, FB_OFF:FB_OFF + N_HEADS_B].set(b_forget_e[0])
    scale = HEAD_DIM ** -0.5 * LOG2_E
    gains = jnp.stack([q_norm_a_e[0] * scale, k_norm_a_e[0], q_norm_b_e[0] * scale, k_norm_b_e[0]])
    tabs = _rotary_tables(s_len, ROT_DIM, HEAD_DIM) + _rotary_tables(s_len, IDX_ROT_DIM, IDX_DIM)
    main, small, v_t, c_neg, ki_rot = _in_proj(x0, row(attn_norm_e[0]), w_main, w_small, bf_pad, gains,
                                               tabs, n_seg=8, seg=seg, tm=tiles.attn)
    wi_t = small[:, WI_OFF:WI_OFF + N_IDX_HEADS].T

    top_k = min(TOPK_MAX, s_len // 4)
    o_a = _dsa_t(main, v_t, ki_rot, wi_t, tq=tiles.dsa_q, tk=tiles.attn, top_k=top_k)
    o_b = _fox_t(main, v_t, c_neg, t=tiles.attn)
    w_out = w_out_e[0].astype(BF16)
    x1 = _out_proj(x0, o_a, o_b, w_out[:seg], w_out[seg:], tm=tiles.out_rows)

    x2 = _ffn(x1, row(ffn_norm_e[0]), w_gate_e[0], w_up_e[0], w_down_e[0],
              tm=tiles.ffn_rows, tf=tiles.ffn_cols)

    n_exp = router_o.shape[2]
    r_f32 = jnp.zeros((d, LANES), F32).at[:, :n_exp].set(router_o[0])
    r_hi = r_f32.astype(BF16)
    r_pad = jnp.stack([r_hi, (r_f32 - r_hi.astype(F32)).astype(BF16)])
    x3, h3_packed, route, counts = _pool_route(
        x2, row(pool_norm_o[0]), w_pool_o[0].astype(BF16), row(pool_scale_o[0]),
        row(moe_norm_o[0]), r_pad, tm=tiles.pool_rows, n_exp=n_exp)

    r_rows = _moe_block_rows(2 * s_len, n_exp)
    n_items = (2 * s_len) // r_rows + n_exp
    as_int = lambda c: route[:, c].astype(jnp.int32)
    item_e, item_rows, pos1, pos2 = _moe_tables(
        counts[0, :n_exp].astype(jnp.int32), as_int(R_E1), as_int(R_E2), as_int(R_K1), as_int(R_K2),
        r=r_rows, n_items=n_items)
    xs_zero = jnp.zeros((n_items * r_rows, d // 2), jnp.uint32)
    xs = _scatter(pos1, pos2, h3_packed, xs_zero, tm=tiles.moe_rows)
    o_sorted = _experts(item_e, item_rows, xs, w_gate_o[0], w_up_o[0], w_down_o[0],
                        r=r_rows, tf=tiles.expert_cols)
    y = _combine(pos1, pos2, x3, route, o_sorted, tm=tiles.moe_rows)
    return y[None]
```

```python
import functools
from typing import NamedTuple

import jax
import jax.numpy as jnp
import numpy as np
from jax import lax
from jax.experimental import pallas as pl
from jax.experimental.pallas import tpu as pltpu

HEAD_DIM = 128
N_HEADS_A = 8
N_HEADS_B = 8
N_IDX_HEADS = 16
IDX_DIM = 64
ROT_DIM = HEAD_DIM // 4
IDX_ROT_DIM = IDX_DIM // 4
ROPE_THETA = 500000.0
CHUNK = 64
TOPK_MAX = 256
POOL_WINDOWS = (2, 4, 8, 16)
RMS_EPS = 1e-6

LANES = 128
BF16_SUBLANES = 16
MXU_COLS = 256
VMEM_LIMIT_BYTES = 56 * 1024 * 1024

WI_OFF = IDX_DIM
FB_OFF = 80

LOG2_E = 1.4426950408889634
NEG_BIG = -1e30
INT_MIN = -(2 ** 31)
INT_MAX = 2 ** 31 - 1
KEY_NONE = INT_MIN + 1
HALF_MIN = -(2 ** 15)
HALF_MAX = 2 ** 15 - 1

F32 = jnp.float32
BF16 = jnp.bfloat16


def _cparams(sem):
    return pltpu.CompilerParams(dimension_semantics=sem, vmem_limit_bytes=VMEM_LIMIT_BYTES)


def _tile(n, pref):
    t = min(n, pref)
    while n % t:
        t //= 2
    return t


def _rms(x, g):
    ms = jnp.mean(x * x, axis=-1, keepdims=True)
    return x * lax.rsqrt(ms + RMS_EPS) * g


def _rotate(y, cos_t, sin_t, half, period):
    lane = lax.broadcasted_iota(jnp.int32, y.shape, 1)
    up = pltpu.roll(y, LANES - half, 1)
    dn = pltpu.roll(y, half, 1)
    rot = jnp.where((lane & (period - 1)) < half, up, dn)
    return y * cos_t + rot * sin_t


def _in_proj_kernel(x_ref, g_ref, wm_ref, ws_ref, bf_ref, gains_ref, cosa_ref, sina_ref,
                    cosi_ref, sini_ref, main_ref, small_ref, vt_ref, cneg_ref, ki_ref, h_scr,
                    raw_scr, carry_scr, *, n_seg, idx_scale, n_fb):
    i = pl.program_id(0)
    j = pl.program_id(1)
    tm = h_scr.shape[0]
    seg = wm_ref.shape[1]

    @pl.when(j == 0)
    def _():
        h_scr[...] = _rms(x_ref[...], g_ref[...]).astype(BF16)

    def head_norm(slab, gain):
        ms = jnp.mean(slab * slab, axis=-1, keepdims=True)
        return slab * lax.rsqrt(ms + RMS_EPS) * gain

    def produce():
        for c0 in range(0, seg, MXU_COLS):
            raw_scr[j % 2, :, c0:c0 + MXU_COLS] = jnp.dot(
                h_scr[...], wm_ref[:, c0:c0 + MXU_COLS], preferred_element_type=F32)

    def finish(epilogue):
        for c0 in range(0, seg, MXU_COLS):
            epilogue(raw_scr[(j + 1) % 2, :, c0:c0 + MXU_COLS], c0)

    def per_slab(fn):
        def epilogue(acc, c0):
            for s0 in range(0, MXU_COLS, LANES):
                y = fn(acc[:, s0:s0 + LANES])
                main_ref[0, :, c0 + s0:c0 + s0 + LANES] = y.astype(BF16)
        return epilogue

    def store_values(acc, c0):
        main_ref[0, :, c0:c0 + MXU_COLS] = acc.astype(BF16)
        vt_ref[0, 0, c0:c0 + MXU_COLS, :] = acc.T.astype(BF16)

    @pl.when(j == 0)
    def _():
        produce()

    @pl.when((j == 1) | (j == 2))
    def _():
        gain = gains_ref[pl.ds(j - 1, 1), :]
        finish(per_slab(lambda y: _rotate(head_norm(y, gain), cosa_ref[...], sina_ref[...],
                                          ROT_DIM // 2, LANES)))
        produce()

    @pl.when((j == 5) | (j == 6))
    def _():
        gain = gains_ref[pl.ds(j - 3, 1), :]
        finish(per_slab(lambda y: head_norm(y, gain)))
        produce()

    @pl.when((j == 3) | (j == 7))
    def _():
        finish(store_values)
        produce()

    @pl.when(j == 4)
    def _():
        finish(per_slab(lambda y: _rotate(y, cosi_ref[...], sini_ref[...], IDX_ROT_DIM // 2, IDX_DIM)))
        produce()

    @pl.when(j == n_seg)
    def _():
        finish(per_slab(jax.nn.sigmoid))
        raw_scr[j % 2, :, 0:LANES] = jnp.dot(h_scr[...], ws_ref[...], preferred_element_type=F32)

    @pl.when(j == n_seg + 1)
    def _():
        acc = raw_scr[(j + 1) % 2, :, 0:LANES]
        lane = lax.broadcasted_iota(jnp.int32, acc.shape, 1)
        is_ki = lane < WI_OFF
        cos_k = jnp.where(is_ki, cosi_ref[...], 1.0)
        sin_k = jnp.where(is_ki, sini_ref[...], 0.0)
        ki = _rotate(acc, cos_k, sin_k, IDX_ROT_DIM // 2, IDX_DIM)
        wi = acc * idx_scale
        z = acc + bf_ref[...]
        ls = jnp.minimum(z, 0.0) - jnp.log1p(jnp.exp(-jnp.abs(z)))
        is_fb = (lane >= FB_OFF) & (lane < FB_OFF + n_fb)
        ls = jnp.where(is_fb, ls, 0.0)
        row = lax.broadcasted_iota(jnp.int32, (tm, tm), 0)
        col = lax.broadcasted_iota(jnp.int32, (tm, tm), 1)
        tri = (col <= row).astype(BF16)
        p1 = ls.astype(BF16)
        r1 = ls - p1.astype(F32)
        p2 = r1.astype(BF16)
        p3 = (r1 - p2.astype(F32)).astype(BF16)
        c = (jnp.dot(tri, p1, preferred_element_type=F32)
             + jnp.dot(tri, p2, preferred_element_type=F32)
             + jnp.dot(tri, p3, preferred_element_type=F32))

        @pl.when(i == 0)
        def _():
            carry_scr[...] = jnp.zeros_like(carry_scr)

        c = c + carry_scr[...]
        carry_scr[...] = c[tm - 1:tm, :]
        small_ref[...] = jnp.where(is_ki, ki, jnp.where(lane < FB_OFF, wi, c))
        ki_ref[...] = jnp.where(is_ki, ki, 0.0).astype(BF16)
        for hb in range(n_fb):
            cneg_ref[hb] = jnp.broadcast_to(-LOG2_E * c[:, FB_OFF + hb:FB_OFF + hb + 1], (tm, LANES))


def _in_proj(x, g, w_main, w_small, bf_pad, gains, tabs, *, n_seg, seg, tm):
    s_len, d = x.shape
    cosa, sina, cosi, sini = tabs
    grid = (s_len // tm, n_seg + 2)
    row_spec = lambda w: pl.BlockSpec((tm, w), lambda i, j: (i, 0))
    finished = lambda j: jnp.clip(j - 1, 0, n_seg - 1)
    kern = functools.partial(_in_proj_kernel, n_seg=n_seg,
                             idx_scale=float((N_IDX_HEADS * IDX_DIM) ** -0.5), n_fb=N_HEADS_B)
    return pl.pallas_call(
        kern,
        grid=grid,
        in_specs=[
            pl.BlockSpec((tm, d), lambda i, j: (i, 0), pipeline_mode=pl.Buffered(1)),
            pl.BlockSpec((1, d), lambda i, j: (0, 0)),
            pl.BlockSpec((d, seg), lambda i, j: (0, jnp.minimum(j, n_seg - 1))),
            pl.BlockSpec((d, LANES), lambda i, j: (0, 0)),
            pl.BlockSpec((1, LANES), lambda i, j: (0, 0)),
            pl.BlockSpec((4, LANES), lambda i, j: (0, 0)),
            row_spec(LANES), row_spec(LANES), row_spec(LANES), row_spec(LANES),
        ],
        out_specs=[
            pl.BlockSpec((1, tm, seg), lambda i, j: (finished(j), i, 0)),
            pl.BlockSpec((tm, LANES), lambda i, j: (i, 0)),
            pl.BlockSpec((1, 1, seg, tm), lambda i, j: (jnp.where(finished(j) <= 2, 0, 1), i, 0, 0)),
            pl.BlockSpec((N_HEADS_B, tm, LANES), lambda i, j: (0, i, 0)),
            pl.BlockSpec((tm, LANES), lambda i, j: (i, 0)),
        ],
        out_shape=[
            jax.ShapeDtypeStruct((n_seg, s_len, seg), BF16),
            jax.ShapeDtypeStruct((s_len, LANES), F32),
            jax.ShapeDtypeStruct((2, s_len // tm, seg, tm), BF16),
            jax.ShapeDtypeStruct((N_HEADS_B, s_len, LANES), F32),
            jax.ShapeDtypeStruct((s_len, LANES), BF16),
        ],
        scratch_shapes=[
            pltpu.VMEM((tm, d), BF16),
            pltpu.VMEM((2, tm, seg), F32),
            pltpu.VMEM((1, LANES), F32),
        ],
        compiler_params=_cparams(("arbitrary", "arbitrary")),
        name="in_proj",
    )(x, g, w_main, w_small, bf_pad, gains, cosa, sina, cosi, sini)


def _transpose_heads(src, dst_scr):
    for s in range(src.shape[1] // LANES):
        sl = slice(s * LANES, (s + 1) * LANES)
        dst_scr[sl, :] = src[:, sl].astype(F32).T.astype(BF16)


def _logits_pass(h, z_t, z_scr, m_scr, mnew_scr):
    z_scr[h] = z_t
    mnew_scr[h:h + 1, :] = jnp.maximum(m_scr[h:h + 1, :], jnp.max(z_t, axis=0, keepdims=True))


def _softmax_pass(h, v_t, z_scr, m_scr, mnew_scr, l_scr, acc_scr):
    hs = slice(h * HEAD_DIM, (h + 1) * HEAD_DIM)
    m_new = mnew_scr[h:h + 1, :]
    alpha = jnp.exp2(m_scr[h:h + 1, :] - m_new)
    p_t = jnp.exp2(z_scr[h] - m_new)
    l_scr[h:h + 1, :] = alpha * l_scr[h:h + 1, :] + jnp.sum(p_t, axis=0, keepdims=True)
    acc_scr[hs, :] = alpha * acc_scr[hs, :] + jnp.dot(v_t, p_t.astype(BF16),
                                                      preferred_element_type=F32)
    m_scr[h:h + 1, :] = m_new


def _init_attn_state(m_scr, l_scr, acc_scr):
    m_scr[...] = jnp.full_like(m_scr, NEG_BIG)
    l_scr[...] = jnp.zeros_like(l_scr)
    acc_scr[...] = jnp.zeros_like(acc_scr)


def _dsa_t_kernel(qa_ref, qi_ref, wit_ref, ki_ref, ka_ref, vat_ref, o_ref,
                  key_scr, half_scr, qat_scr, qit_scr, z_scr, m_scr, mnew_scr, l_scr, acc_scr,
                  *, tq, tk, top_k, n_heads, n_idx):
    i = pl.program_id(0)
    n_kt = ((i + 1) * tq + tk - 1) // tk
    t_glob = i * tq + lax.broadcasted_iota(jnp.int32, (1, tq), 1)
    adm_len = (t_glob // CHUNK + 1) * CHUNK
    key_pos = lax.broadcasted_iota(jnp.int32, (tk, 1), 0)

    _transpose_heads(qa_ref[0], qat_scr)
    _transpose_heads(qi_ref[0], qit_scr)

    def score_tile(kt, carry):
        k0 = pl.multiple_of(kt * tk, tk)
        k_idx = ki_ref[pl.ds(k0, tk), 0:IDX_DIM]
        acc = jnp.zeros((tk, tq), F32)
        for h in range(n_idx):
            q_t = qit_scr[h * IDX_DIM:(h + 1) * IDX_DIM, :]
            rel = jnp.maximum(jnp.dot(k_idx, q_t, preferred_element_type=F32), 0.0)
            acc = acc + rel * wit_ref[h:h + 1, :]
        bits = lax.bitcast_convert_type(acc, jnp.int32)
        key = jnp.where(bits < 0, bits ^ INT_MAX, bits)
        key = jnp.where(k0 + key_pos < adm_len, key, KEY_NONE)
        key_scr[kt] = key
        half_scr[kt] = (key >> 16).astype(jnp.int16)
        return carry

    lax.fori_loop(0, n_kt, score_tile, 0)

    def count(pred):
        def body(kt, cnt):
            hit = jnp.where(pred(key_scr[kt], kt * tk + key_pos), 1, 0)
            return cnt + jnp.sum(hit.reshape(tk // 8, 8, tq), axis=0)

        cnt = lax.fori_loop(0, n_kt, body, jnp.zeros((8, tq), jnp.int32))
        return jnp.sum(cnt, axis=0, keepdims=True)

    def count_half_ge(mid):
        mid16 = mid.astype(jnp.int16)

        def body(kt, cnt):
            hit = jnp.where(half_scr[kt] >= mid16, jnp.int16(1), jnp.int16(0))
            parts = [hit[r * 16:(r + 1) * 16] for r in range(tk // 16)]
            while len(parts) > 1:
                parts = [parts[a] + parts[a + 1] for a in range(0, len(parts), 2)]
            return cnt + parts[0].astype(jnp.int32)

        cnt = lax.fori_loop(0, n_kt, body, jnp.zeros((16, tq), jnp.int32))
        return jnp.sum(cnt, axis=0, keepdims=True)

    def bisect_half(need_k, c_all):
        def step(_, state):
            lo, hi, c_lo, c_hi = state
            mid = (lo + hi) >> 1
            c_mid = count_half_ge(mid)
            take = c_mid >= need_k
            return (jnp.where(take, mid, lo), jnp.where(take, hi, mid),
                    jnp.where(take, c_mid, c_lo), jnp.where(take, c_hi, c_mid))

        x, _, c_x, c_above = lax.fori_loop(0, 16, step,
                                           (row(HALF_MIN), row(HALF_MAX + 1), c_all, row(0)))
        return x, c_x, c_above

    row = lambda v: jnp.full((1, tq), v, jnp.int32)
    thr_hi, c_ge_hi, c_gt_hi = bisect_half(top_k, n_kt * tk + row(0))
    thr_hi16 = thr_hi.astype(jnp.int16)

    def low_half_tile(kt, carry):
        low = ((key_scr[kt] & 0xFFFF) + HALF_MIN).astype(jnp.int16)
        half_scr[kt] = jnp.where(half_scr[kt] == thr_hi16, low, jnp.int16(HALF_MIN))
        return carry

    lax.fori_loop(0, n_kt, low_half_tile, 0)
    thr_lo, c_ge_lo, c_gt_lo = bisect_half(top_k - c_gt_hi, c_ge_hi - c_gt_hi)
    thr = thr_hi * 65536 + (thr_lo - HALF_MIN)
    c_ge = c_gt_hi + c_ge_lo
    c_gt = c_gt_hi + c_gt_lo
    short = thr <= KEY_NONE
    thr = jnp.maximum(thr, KEY_NONE)
    need = top_k - c_gt

    def tie_cut():
        def step(_, lohi):
            lo, hi = lohi
            mid = (lo + hi) >> 1
            ok = count(lambda k, pos: (k == thr) & (pos <= mid)) >= need
            return jnp.where(ok, lo, mid), jnp.where(ok, mid, hi)

        n_steps = int(np.ceil(np.log2(key_scr.shape[0] * tk))) + 1
        return lax.fori_loop(0, n_steps, step, (row(-1), n_kt * tk - 1 + row(0)))[1]

    has_tie = jnp.max(jnp.where((c_ge > top_k) & ~short, 1.0, 0.0)) > 0.5
    cut = lax.cond(has_tie, tie_cut, lambda: row(INT_MAX))
    cut = jnp.where(short, -1, cut)

    _init_attn_state(m_scr, l_scr, acc_scr)

    def attend(kt, carry):
        k0 = pl.multiple_of(kt * tk, tk)
        keys = key_scr[kt]
        sel = (keys > thr) | ((keys == thr) & (kt * tk + key_pos <= cut))
        bias = jnp.where(sel, 0.0, NEG_BIG).astype(F32)
        for h in range(n_heads):
            hs = slice(h * HEAD_DIM, (h + 1) * HEAD_DIM)
            z_t = jnp.dot(ka_ref[0, pl.ds(k0, tk), hs], qat_scr[hs, :],
                          preferred_element_type=F32) + bias
            _logits_pass(h, z_t, z_scr, m_scr, mnew_scr)
        for h in range(n_heads):
            hs = slice(h * HEAD_DIM, (h + 1) * HEAD_DIM)
            _softmax_pass(h, vat_ref[0, kt, hs, :], z_scr, m_scr, mnew_scr, l_scr, acc_scr)
        return carry

    lax.fori_loop(0, n_kt, attend, 0)
    for h in range(n_heads):
        hs = slice(h * HEAD_DIM, (h + 1) * HEAD_DIM)
        o_ref[:, hs] = (acc_scr[hs, :] / l_scr[h:h + 1, :]).T.astype(BF16)


def _dsa_t(main, vt, ki, wi_t, *, tq, tk, top_k):
    n_seg, s_len, seg = main.shape
    assert vt.shape == (2, s_len // tk, seg, tk)
    kern = functools.partial(_dsa_t_kernel, tq=tq, tk=tk, top_k=top_k, n_heads=N_HEADS_A,
                             n_idx=N_IDX_HEADS)
    resident = functools.partial(pl.BlockSpec, pipeline_mode=pl.Buffered(1))
    return pl.pallas_call(
        kern,
        grid=(s_len // tq,),
        in_specs=[
            pl.BlockSpec((1, tq, seg), lambda i: (0, i, 0)),
            pl.BlockSpec((1, tq, seg), lambda i: (3, i, 0)),
            pl.BlockSpec((N_IDX_HEADS, tq), lambda i: (0, i)),
            resident((s_len, LANES), lambda i: (0, 0)),
            resident((1, s_len, seg), lambda i: (1, 0, 0)),
            resident((1, s_len // tk, seg, tk), lambda i: (0, 0, 0, 0)),
        ],
        out_specs=pl.BlockSpec((tq, seg), lambda i: (i, 0)),
        out_shape=jax.ShapeDtypeStruct((s_len, seg), BF16),
        scratch_shapes=[
            pltpu.VMEM((s_len // tk, tk, tq), jnp.int32),
            pltpu.VMEM((s_len // tk, tk, tq), jnp.int16),
            pltpu.VMEM((seg, tq), BF16),
            pltpu.VMEM((seg, tq), BF16),
            pltpu.VMEM((N_HEADS_A, tk, tq), F32),
            pltpu.VMEM((N_HEADS_A, tq), F32),
            pltpu.VMEM((N_HEADS_A, tq), F32),
            pltpu.VMEM((N_HEADS_A, tq), F32),
            pltpu.VMEM((seg, tq), F32),
        ],
        compiler_params=_cparams(("arbitrary",)),
        name="dsa",
    )(main, main, wi_t, ki, main, vt)


def _fox_t_kernel(qt_ref, kt_ref, q_ref, k_ref, vt_ref, gate_ref, cneg_ref, o_ref,
                  qt_scr, z_scr, m_scr, mnew_scr, l_scr, acc_scr, *, t, n_heads):
    p = pl.program_id(0)
    qi = qt_ref[p]
    ki = kt_ref[p]

    @pl.when(ki == 0)
    def _():
        _transpose_heads(q_ref[0], qt_scr)
        _init_attn_state(m_scr, l_scr, acc_scr)

    def update(causal):
        key_i = lax.broadcasted_iota(jnp.int32, (t, LANES), 0)
        qry_i = lax.broadcasted_iota(jnp.int32, (t, LANES), 1)
        for h in range(n_heads):
            hs = slice(h * HEAD_DIM, (h + 1) * HEAD_DIM)
            z_t = jnp.dot(k_ref[0, :, hs], qt_scr[hs, :], preferred_element_type=F32)
            cols = []
            for c in range(t // LANES):
                zc = z_t[:, c * LANES:(c + 1) * LANES] + cneg_ref[h]
                if causal:
                    zc = jnp.where(key_i <= qry_i + c * LANES, zc, NEG_BIG)
                cols.append(zc)
            _logits_pass(h, jnp.concatenate(cols, axis=1), z_scr, m_scr, mnew_scr)
        for h in range(n_heads):
            hs = slice(h * HEAD_DIM, (h + 1) * HEAD_DIM)
            _softmax_pass(h, vt_ref[0, 0, hs, :], z_scr, m_scr, mnew_scr, l_scr, acc_scr)

    @pl.when(ki < qi)
    def _():
        update(causal=False)

    @pl.when(ki == qi)
    def _():
        update(causal=True)
        for h in range(n_heads):
            hs = slice(h * HEAD_DIM, (h + 1) * HEAD_DIM)
            o = (acc_scr[hs, :] / l_scr[h:h + 1, :]).T
            o_ref[:, hs] = (o * gate_ref[0, :, hs].astype(F32)).astype(BF16)


def _fox_t(main, vt, cneg, *, t):
    n_seg, s_len, seg = main.shape
    assert vt.shape == (2, s_len // t, seg, t)
    nq = s_len // t
    pairs = [(a, b) for a in range(nq) for b in range(a + 1)]
    q_tab = jnp.asarray([a for a, _ in pairs], jnp.int32)
    k_tab = jnp.asarray([b for _, b in pairs], jnp.int32)
    kern = functools.partial(_fox_t_kernel, t=t, n_heads=N_HEADS_B)
    grid_spec = pltpu.PrefetchScalarGridSpec(
        num_scalar_prefetch=2,
        grid=(len(pairs),),
        in_specs=[
            pl.BlockSpec((1, t, seg), lambda p, qt, kt: (4, qt[p], 0)),
            pl.BlockSpec((1, t, seg), lambda p, qt, kt: (5, kt[p], 0)),
            pl.BlockSpec((1, 1, seg, t), lambda p, qt, kt: (1, kt[p], 0, 0)),
            pl.BlockSpec((1, t, seg), lambda p, qt, kt: (7, qt[p], 0)),
            pl.BlockSpec((N_HEADS_B, t, LANES), lambda p, qt, kt: (0, kt[p], 0)),
        ],
        out_specs=pl.BlockSpec((t, seg), lambda p, qt, kt: (qt[p], 0)),
        scratch_shapes=[
            pltpu.VMEM((seg, t), BF16),
            pltpu.VMEM((N_HEADS_B, t, t), F32),
            pltpu.VMEM((N_HEADS_B, t), F32),
            pltpu.VMEM((N_HEADS_B, t), F32),
            pltpu.VMEM((N_HEADS_B, t), F32),
            pltpu.VMEM((seg, t), F32),
        ],
    )
    return pl.pallas_call(
        kern,
        grid_spec=grid_spec,
        out_shape=jax.ShapeDtypeStruct((s_len, seg), BF16),
        compiler_params=_cparams(("arbitrary",)),
        name="fox",
    )(q_tab, k_tab, main, main, vt, main, cneg)


def _out_proj_kernel(x_ref, oa_ref, ob_ref, wa_ref, wb_ref, y_ref):
    y = x_ref[...] + jnp.dot(oa_ref[...], wa_ref[...], preferred_element_type=F32)
    y_ref[...] = y + jnp.dot(ob_ref[...], wb_ref[...], preferred_element_type=F32)


def _out_proj(x, o_a, o_b, w_a, w_b, *, tm):
    s_len, d = x.shape
    seg = o_a.shape[1]
    resident = functools.partial(pl.BlockSpec, pipeline_mode=pl.Buffered(1))
    return pl.pallas_call(
        _out_proj_kernel,
        grid=(s_len // tm,),
        in_specs=[
            pl.BlockSpec((tm, d), lambda i: (i, 0)),
            pl.BlockSpec((tm, seg), lambda i: (i, 0)),
            pl.BlockSpec((tm, seg), lambda i: (i, 0)),
            resident((seg, d), lambda i: (0, 0)),
            resident((seg, d), lambda i: (0, 0)),
        ],
        out_specs=pl.BlockSpec((tm, d), lambda i: (i, 0)),
        out_shape=jax.ShapeDtypeStruct((s_len, d), F32),
        compiler_params=_cparams(("arbitrary",)),
        name="out_proj",
    )(x, o_a, o_b, w_a, w_b)


def _swiglu_step(h, wg, wu, wd):
    g = jnp.dot(h, wg.astype(BF16), preferred_element_type=F32)
    u = jnp.dot(h, wu.astype(BF16), preferred_element_type=F32)
    a = (g * jax.nn.sigmoid(g) * u).astype(BF16)
    return jnp.dot(a, wd.astype(BF16), preferred_element_type=F32)


def _ffn_kernel(x_ref, g_ref, wg_ref, wu_ref, wd_ref, y_ref, h_scr):
    @pl.when(pl.program_id(1) == 0)
    def _():
        x = x_ref[...]
        h_scr[...] = _rms(x, g_ref[...]).astype(BF16)
        y_ref[...] = x

    y_ref[...] += _swiglu_step(h_scr[...], wg_ref[...], wu_ref[...], wd_ref[...])


def _ffn(x, g, wg, wu, wd, *, tm, tf):
    s_len, d = x.shape
    f = wg.shape[1]
    return pl.pallas_call(
        _ffn_kernel,
        grid=(s_len // tm, f // tf),
        in_specs=[
            pl.BlockSpec((tm, d), lambda i, j: (i, 0), pipeline_mode=pl.Buffered(1)),
            pl.BlockSpec((1, d), lambda i, j: (0, 0)),
            pl.BlockSpec((d, tf), lambda i, j: (0, j)),
            pl.BlockSpec((d, tf), lambda i, j: (0, j)),
            pl.BlockSpec((tf, d), lambda i, j: (j, 0)),
        ],
        out_specs=pl.BlockSpec((tm, d), lambda i, j: (i, 0)),
        out_shape=jax.ShapeDtypeStruct((s_len, d), F32),
        scratch_shapes=[pltpu.VMEM((tm, d), BF16)],
        compiler_params=_cparams(("arbitrary", "arbitrary")),
        name="ffn",
    )(x, g, wg, wu, wd)


def _pool_kernel(x_ref, g_ref, w_ref, sc_ref, gm_ref, r_ref, y_ref, hp_ref, route_ref, cnt_ref,
                 ext_scr, carry_scr, *, halo, n_exp):
    i = pl.program_id(0)
    tm, d = x_ref.shape
    n_grp = len(POOL_WINDOWS)
    width = d // n_grp

    @pl.when(i == 0)
    def _():
        ext_scr[0:halo, :] = jnp.zeros((halo, d), F32)

    x = x_ref[...]
    ext_scr[halo:halo + tm, :] = _rms(x, g_ref[...])
    t_glob = i * tm + lax.broadcasted_iota(jnp.int32, (tm, 1), 0)
    for gi, win in enumerate(POOL_WINDOWS):
        cs = slice(gi * width, (gi + 1) * width)
        hn = ext_scr[halo:halo + tm, cs]
        tot = hn
        for dlt in range(1, win):
            tot = tot + ext_scr[halo - dlt:halo - dlt + tm, cs]
        count = jnp.minimum(t_glob + 1, win).astype(F32)
        pooled = (tot / count - hn).astype(BF16)
        y = jnp.dot(pooled, w_ref[gi], preferred_element_type=F32)
        y_ref[:, cs] = x[:, cs] + y * sc_ref[:, cs]
    ext_scr[0:halo, :] = ext_scr[tm:tm + halo, :]

    h = _rms(y_ref[...], gm_ref[...])
    hp_ref[...] = _pack_bf16_pairs(h)
    _route_rows(h, r_ref, route_ref, cnt_ref, carry_scr, n_exp=n_exp)


def _pool_route(x, g, w_pool, scale, g_moe, r_pad, *, tm, n_exp):
    s_len, d = x.shape
    halo = 16
    assert max(POOL_WINDOWS) <= halo
    n_grp, width, _ = w_pool.shape
    row_out = lambda w: pl.BlockSpec((tm, w), lambda i: (i, 0))
    return pl.pallas_call(
        functools.partial(_pool_kernel, halo=halo, n_exp=n_exp),
        grid=(s_len // tm,),
        in_specs=[
            pl.BlockSpec((tm, d), lambda i: (i, 0)),
            pl.BlockSpec((1, d), lambda i: (0, 0)),
            pl.BlockSpec((n_grp, width, width), lambda i: (0, 0, 0)),
            pl.BlockSpec((1, d), lambda i: (0, 0)),
            pl.BlockSpec((1, d), lambda i: (0, 0)),
            pl.BlockSpec((2, d, LANES), lambda i: (0, 0, 0)),
        ],
        out_specs=[row_out(d), row_out(d // 2), row_out(LANES),
                   pl.BlockSpec((1, LANES), lambda i: (0, 0))],
        out_shape=[
            jax.ShapeDtypeStruct((s_len, d), F32),
            jax.ShapeDtypeStruct((s_len, d // 2), jnp.uint32),
            jax.ShapeDtypeStruct((s_len, LANES), F32),
            jax.ShapeDtypeStruct((1, LANES), F32),
        ],
        scratch_shapes=[pltpu.VMEM((tm + halo, d), F32), pltpu.VMEM((1, LANES), F32)],
        compiler_params=_cparams(("arbitrary",)),
        name="pool_route",
    )(x, g, w_pool, scale, g_moe, r_pad)


R_E1, R_E2, R_G1, R_G2, R_K1, R_K2 = range(6)


def _route_rows(h, r_ref, route_ref, cnt_ref, carry_scr, *, n_exp):
    tm = h.shape[0]

    @pl.when(pl.program_id(0) == 0)
    def _():
        carry_scr[...] = jnp.zeros_like(carry_scr)

    h_hi = h.astype(BF16)
    h_lo = (h - h_hi.astype(F32)).astype(BF16)
    logits = (jnp.dot(h_hi, r_ref[0], preferred_element_type=F32)
              + jnp.dot(h_hi, r_ref[1], preferred_element_type=F32)
              + jnp.dot(h_lo, r_ref[0], preferred_element_type=F32))
    lane = lax.broadcasted_iota(jnp.int32, logits.shape, 1)
    logits = jnp.where(lane < n_exp, logits, -jnp.inf)
    lane_f = lane.astype(F32)
    v1 = jnp.max(logits, axis=1, keepdims=True)
    e1 = jnp.min(jnp.where(logits == v1, lane_f, float(LANES)), axis=1, keepdims=True)
    rest = jnp.where(lane_f == e1, -jnp.inf, logits)
    v2 = jnp.max(rest, axis=1, keepdims=True)
    e2 = jnp.min(jnp.where(rest == v2, lane_f, float(LANES)), axis=1, keepdims=True)
    ex = jnp.exp(v2 - v1)
    g1 = 1.0 / (1.0 + ex)
    g2 = ex / (1.0 + ex)
    oh1 = (lane_f == e1).astype(F32)
    oh2 = (lane_f == e2).astype(F32)
    both = oh1 + oh2
    row = lax.broadcasted_iota(jnp.int32, (tm, tm), 0)
    col = lax.broadcasted_iota(jnp.int32, (tm, tm), 1)
    tri = (col < row).astype(BF16)
    before = jnp.dot(tri, both.astype(BF16), preferred_element_type=F32) + carry_scr[...]
    k1 = jnp.sum(before * oh1, axis=1, keepdims=True)
    k2 = jnp.sum(before * oh2, axis=1, keepdims=True)
    carry_scr[...] = carry_scr[...] + jnp.sum(both, axis=0, keepdims=True)
    cnt_ref[...] = carry_scr[...]
    vals = (e1, e2, g1, g2, k1, k2)
    out = jnp.zeros(logits.shape, F32)
    for li, v in enumerate(vals):
        out = jnp.where(lane == li, v, out)
    route_ref[...] = out


def _pack_bf16_pairs(h):
    half = h.shape[1] // 2
    lo = lax.bitcast_convert_type(h[:, :half].astype(BF16).astype(F32), jnp.uint32)
    hi = lax.bitcast_convert_type(h[:, half:].astype(BF16).astype(F32), jnp.uint32)
    return (lo >> 16) | (hi & jnp.uint32(0xFFFF0000))


def _unpack_bf16_pairs(u):
    lo = lax.bitcast_convert_type(u << 16, F32).astype(BF16)
    hi = lax.bitcast_convert_type(u & jnp.uint32(0xFFFF0000), F32).astype(BF16)
    return lo, hi


ROW_COPY_UNROLL = 8


def _run_row_copies(copies, n):
    def start(t, c):
        a, b = copies(t)
        a.start(priority=0)
        b.start(priority=1)
        return c

    def wait(t, c):
        a, b = copies(t)
        a.wait()
        b.wait()
        return c

    lax.fori_loop(0, n, start, 0, unroll=ROW_COPY_UNROLL)
    lax.fori_loop(0, n, wait, 0, unroll=ROW_COPY_UNROLL)


def _scatter_kernel(p1_ref, p2_ref, hp_ref, xs_in_ref, xs_ref, sem):
    del xs_in_ref
    i = pl.program_id(0)
    tm = hp_ref.shape[0]

    def copies(t):
        src = hp_ref.at[pl.ds(t, 1), :]
        return (pltpu.make_async_copy(src, xs_ref.at[pl.ds(p1_ref[i * tm + t], 1), :], sem.at[0]),
                pltpu.make_async_copy(src, xs_ref.at[pl.ds(p2_ref[i * tm + t], 1), :], sem.at[1]))

    _run_row_copies(copies, tm)


def _scatter(pos1, pos2, hp, xs_zero, *, tm):
    s_len, half = hp.shape
    grid_spec = pltpu.PrefetchScalarGridSpec(
        num_scalar_prefetch=2,
        grid=(s_len // tm,),
        in_specs=[
            pl.BlockSpec((tm, half), lambda i, p1, p2: (i, 0)),
            pl.BlockSpec(memory_space=pl.ANY),
        ],
        out_specs=pl.BlockSpec(memory_space=pl.ANY),
        scratch_shapes=[pltpu.SemaphoreType.DMA((2,))],
    )
    return pl.pallas_call(
        _scatter_kernel,
        grid_spec=grid_spec,
        out_shape=jax.ShapeDtypeStruct(xs_zero.shape, jnp.uint32),
        input_output_aliases={3: 0},
        compiler_params=_cparams(("arbitrary",)),
        name="moe_scatter",
    )(pos1, pos2, hp, xs_zero)


def _experts_kernel(e_ref, v_ref, xs_ref, wg_ref, wu_ref, wd_ref, o_ref, xb_scr):
    j = pl.program_id(0)
    f = pl.program_id(1)
    half = xs_ref.shape[1]

    @pl.when(f == 0)
    def _():
        o_ref[...] = jnp.zeros_like(o_ref)

    @pl.when(v_ref[j] > 0)
    def _():
        @pl.when(f == 0)
        def _():
            lo, hi = _unpack_bf16_pairs(xs_ref[...])
            xb_scr[:, :half] = lo
            xb_scr[:, half:] = hi

        o_ref[...] += _swiglu_step(xb_scr[...], wg_ref[0], wu_ref[0], wd_ref[0])


def _experts(item_e, item_rows, xs, wg, wu, wd, *, r, tf):
    rows, half = xs.shape
    n_exp, d, f = wg.shape
    n_items = rows // r
    n_f = f // tf

    def f_idx(fi, j, v):
        return jnp.where(v[j] > 0, fi, n_f - 1)

    grid_spec = pltpu.PrefetchScalarGridSpec(
        num_scalar_prefetch=2,
        grid=(n_items, n_f),
        in_specs=[
            pl.BlockSpec((r, half), lambda j, fi, e, v: (j, 0)),
            pl.BlockSpec((1, d, tf), lambda j, fi, e, v: (e[j], 0, f_idx(fi, j, v))),
            pl.BlockSpec((1, d, tf), lambda j, fi, e, v: (e[j], 0, f_idx(fi, j, v))),
            pl.BlockSpec((1, tf, d), lambda j, fi, e, v: (e[j], f_idx(fi, j, v), 0)),
        ],
        out_specs=pl.BlockSpec((r, d), lambda j, fi, e, v: (j, 0)),
        scratch_shapes=[pltpu.VMEM((r, d), BF16)],
    )
    return pl.pallas_call(
        _experts_kernel,
        grid_spec=grid_spec,
        out_shape=jax.ShapeDtypeStruct((rows, d), F32),
        compiler_params=_cparams(("arbitrary", "arbitrary")),
        name="moe_experts",
    )(item_e, item_rows, xs, wg, wu, wd)


def _combine_kernel(p1_ref, p2_ref, x_ref, route_ref, o_hbm, y_ref, a_scr, b_scr, sem):
    i = pl.program_id(0)
    tm = x_ref.shape[0]

    def copies(t):
        return (pltpu.make_async_copy(o_hbm.at[pl.ds(p1_ref[i * tm + t], 1), :],
                                      a_scr.at[pl.ds(t, 1), :], sem.at[0]),
                pltpu.make_async_copy(o_hbm.at[pl.ds(p2_ref[i * tm + t], 1), :],
                                      b_scr.at[pl.ds(t, 1), :], sem.at[1]))

    _run_row_copies(copies, tm)
    route = route_ref[...]
    g1 = route[:, R_G1:R_G1 + 1]
    g2 = route[:, R_G2:R_G2 + 1]
    y_ref[...] = x_ref[...] + g1 * a_scr[...] + g2 * b_scr[...]


def _combine(pos1, pos2, x, route, o_sorted, *, tm):
    s_len, d = x.shape
    grid_spec = pltpu.PrefetchScalarGridSpec(
        num_scalar_prefetch=2,
        grid=(s_len // tm,),
        in_specs=[
            pl.BlockSpec((tm, d), lambda i, p1, p2: (i, 0)),
            pl.BlockSpec((tm, LANES), lambda i, p1, p2: (i, 0)),
            pl.BlockSpec(memory_space=pl.ANY),
        ],
        out_specs=pl.BlockSpec((tm, d), lambda i, p1, p2: (i, 0)),
        scratch_shapes=[
            pltpu.VMEM((tm, d), F32),
            pltpu.VMEM((tm, d), F32),
            pltpu.SemaphoreType.DMA((2,)),
        ],
    )
    return pl.pallas_call(
        _combine_kernel,
        grid_spec=grid_spec,
        out_shape=jax.ShapeDtypeStruct((s_len, d), F32),
        compiler_params=_cparams(("arbitrary",)),
        name="moe_combine",
    )(pos1, pos2, x, route, o_sorted)


def _rotary_tables(s_len, rot_dim, period):
    half = rot_dim // 2
    inv_freq = 1.0 / (ROPE_THETA ** (jnp.arange(half, dtype=F32) * 2.0 / rot_dim))
    ang = jnp.arange(s_len, dtype=F32)[:, None] * inv_freq[None, :]
    cos, sin = jnp.cos(ang), jnp.sin(ang)
    pad = period - rot_dim
    cos_p = jnp.concatenate([cos, cos, jnp.ones((s_len, pad), F32)], axis=1)
    sin_p = jnp.concatenate([-sin, sin, jnp.zeros((s_len, pad), F32)], axis=1)
    reps = LANES // period
    return jnp.tile(cos_p, (1, reps)), jnp.tile(sin_p, (1, reps))


def _split_w_in(w_in):
    seg_a = N_HEADS_A * HEAD_DIM
    seg_i = N_IDX_HEADS * IDX_DIM
    seg_b = N_HEADS_B * HEAD_DIM
    sizes = (seg_a, seg_a, seg_a, seg_i, IDX_DIM, N_IDX_HEADS, seg_b, seg_b, seg_b, N_HEADS_B, seg_b)
    offs = np.cumsum(sizes)[:-1].tolist()
    return jnp.split(w_in, offs, axis=1)


def _moe_tables(counts, e1, e2, k1, k2, *, r, n_items):
    n_exp = counts.shape[0]
    blocks = (counts + r - 1) // r
    bend = jnp.cumsum(blocks)
    bstart = bend - blocks
    j = jnp.arange(n_items, dtype=jnp.int32)
    item_e = jnp.sum((j[:, None] >= bend[None, :]).astype(jnp.int32), axis=1)
    valid = item_e < n_exp
    e_c = jnp.minimum(item_e, n_exp - 1)
    item_rows = jnp.where(valid, jnp.clip(counts[e_c] - (j - bstart[e_c]) * r, 0, r), 0)
    last_e = jnp.minimum(item_e[jnp.maximum(bend[-1] - 1, 0)], n_exp - 1)
    item_e = jnp.where(valid, item_e, last_e).astype(jnp.int32)
    pos1 = bstart[e1] * r + k1
    pos2 = bstart[e2] * r + k2
    return item_e, item_rows.astype(jnp.int32), pos1.astype(jnp.int32), pos2.astype(jnp.int32)


def _moe_block_rows(n_assign, n_exp):
    share = -(-n_assign // (2 * n_exp))
    return -(-(share + 3 * share // 64) // BF16_SUBLANES) * BF16_SUBLANES


class _Tiles(NamedTuple):
    attn: int
    dsa_q: int
    out_rows: int
    ffn_rows: int
    ffn_cols: int
    pool_rows: int
    moe_rows: int
    expert_cols: int


def _tiles(s_len, d_ff, d_ff_expert):
    return _Tiles(attn=_tile(s_len, 512), dsa_q=_tile(s_len, 256), out_rows=_tile(s_len, 512),
                  ffn_rows=_tile(s_len, 1024), ffn_cols=_tile(d_ff, 256),
                  pool_rows=_tile(s_len, 512), moe_rows=_tile(s_len, 256),
                  expert_cols=_tile(d_ff_expert, 256))


def kernel(x, attn_norm_e, w_in_e, b_forget_e, q_norm_a_e, k_norm_a_e, q_norm_b_e, k_norm_b_e,
           w_out_e, ffn_norm_e, w_gate_e, w_up_e, w_down_e, pool_norm_o, w_pool_o, pool_scale_o,
           moe_norm_o, router_o, w_gate_o, w_up_o, w_down_o):
    b, s_len, d = x.shape
    assert b == 1
    x0 = x[0]
    tiles = _tiles(s_len, w_gate_e.shape[2], w_gate_o.shape[3])
    seg = N_HEADS_A * HEAD_DIM
    assert seg == N_IDX_HEADS * IDX_DIM == N_HEADS_B * HEAD_DIM
    assert N_IDX_HEADS <= FB_OFF - WI_OFF and N_HEADS_B <= LANES - FB_OFF
    row = lambda v: v.reshape(1, -1)

    qa, ka, va, qi, ki, wi, qb, kb, vb, fb, gb = _split_w_in(w_in_e[0])
    w_main = jnp.concatenate([qa, ka, va, qi, qb, kb, vb, gb], axis=1).astype(BF16)
    w_small = jnp.concatenate(
        [ki, wi, jnp.zeros((d, FB_OFF - WI_OFF - N_IDX_HEADS), F32), fb,
         jnp.zeros((d, LANES - FB_OFF - N_HEADS_B), F32)], axis=1).astype(BF16)
    bf_pad = jnp.zeros((1, LANES), F32).at[0, FB_OFF:FB_OFF + N_HEADS_B].set(b_forget_e[0])
    scale = HEAD_DIM ** -0.5 * LOG2_E
    gains = jnp.stack([q_norm_a_e[0] * scale, k_norm_a_e[0], q_norm_b_e[0] * scale, k_norm_b_e[0]])
    tabs = _rotary_tables(s_len, ROT_DIM, HEAD_DIM) + _rotary_tables(s_len, IDX_ROT_DIM, IDX_DIM)
    main, small, v_t, c_neg, ki_rot = _in_proj(x0, row(attn_norm_e[0]), w_main, w_small, bf_pad, gains,
                                               tabs, n_seg=8, seg=seg, tm=tiles.attn)
    wi_t = small[:, WI_OFF:WI_OFF + N_IDX_HEADS].T

    top_k = min(TOPK_MAX, s_len // 4)
    o_a = _dsa_t(main, v_t, ki_rot, wi_t, tq=tiles.dsa_q, tk=tiles.attn, top_k=top_k)
    o_b = _fox_t(main, v_t, c_neg, t=tiles.attn)
    w_out = w_out_e[0].astype(BF16)
    x1 = _out_proj(x0, o_a, o_b, w_out[:seg], w_out[seg:], tm=tiles.out_rows)

    x2 = _ffn(x1, row(ffn_norm_e[0]), w_gate_e[0], w_up_e[0], w_down_e[0],
              tm=tiles.ffn_rows, tf=tiles.ffn_cols)

    n_exp = router_o.shape[2]
    r_f32 = jnp.zeros((d, LANES), F32).at[:, :n_exp].set(router_o[0])
    r_hi = r_f32.astype(BF16)
    r_pad = jnp.stack([r_hi, (r_f32 - r_hi.astype(F32)).astype(BF16)])
    x3, h3_packed, route, counts = _pool_route(
        x2, row(pool_norm_o[0]), w_pool_o[0].astype(BF16), row(pool_scale_o[0]),
        row(moe_norm_o[0]), r_pad, tm=tiles.pool_rows, n_exp=n_exp)

    r_rows = _moe_block_rows(2 * s_len, n_exp)
    n_items = (2 * s_len) // r_rows + n_exp
    as_int = lambda c: route[:, c].astype(jnp.int32)
    item_e, item_rows, pos1, pos2 = _moe_tables(
        counts[0, :n_exp].astype(jnp.int32), as_int(R_E1), as_int(R_E2), as_int(R_K1), as_int(R_K2),
        r=r_rows, n_items=n_items)
    xs_zero = jnp.zeros((n_items * r_rows, d // 2), jnp.uint32)
    xs = _scatter(pos1, pos2, h3_packed, xs_zero, tm=tiles.moe_rows)
    o_sorted = _experts(item_e, item_rows, xs, w_gate_o[0], w_up_o[0], w_down_o[0],
                        r=r_rows, tf=tiles.expert_cols)
    y = _combine(pos1, pos2, x3, route, o_sorted, tm=tiles.moe_rows)
    return y[None]
```

```python
import functools
from typing import NamedTuple

import jax
import jax.numpy as jnp
import numpy as np
from jax import lax
from jax.experimental import pallas as pl
from jax.experimental.pallas import tpu as pltpu

HEAD_DIM = 128
N_HEADS_A = 8
N_HEADS_B = 8
N_IDX_HEADS = 16
IDX_DIM = 64
ROT_DIM = HEAD_DIM // 4
IDX_ROT_DIM = IDX_DIM // 4
ROPE_THETA = 500000.0
CHUNK = 64
TOPK_MAX = 256
POOL_WINDOWS = (2, 4, 8, 16)
RMS_EPS = 1e-6

LANES = 128
BF16_SUBLANES = 16
MXU_COLS = 256
VMEM_LIMIT_BYTES = 56 * 1024 * 1024

WI_OFF = IDX_DIM
FB_OFF = 80

LOG2_E = 1.4426950408889634
NEG_BIG = -1e30
INT_MIN = -(2 ** 31)
INT_MAX = 2 ** 31 - 1
KEY_NONE = INT_MIN + 1
HALF_MIN = -(2 ** 15)
HALF_MAX = 2 ** 15 - 1

F32 = jnp.float32
BF16 = jnp.bfloat16


def _cparams(sem):
    return pltpu.CompilerParams(dimension_semantics=sem, vmem_limit_bytes=VMEM_LIMIT_BYTES)


def _tile(n, pref):
    t = min(n, pref)
    while n % t:
        t //= 2
    return t


def _rms(x, g):
    ms = jnp.mean(x * x, axis=-1, keepdims=True)
    return x * lax.rsqrt(ms + RMS_EPS) * g


def _rotate(y, cos_t, sin_t, half, period):
    lane = lax.broadcasted_iota(jnp.int32, y.shape, 1)
    up = pltpu.roll(y, LANES - half, 1)
    dn = pltpu.roll(y, half, 1)
    rot = jnp.where((lane & (period - 1)) < half, up, dn)
    return y * cos_t + rot * sin_t


def _in_proj_kernel(x_ref, g_ref, wm_ref, ws_ref, bf_ref, gains_ref, cosa_ref, sina_ref,
                    cosi_ref, sini_ref, main_ref, small_ref, vt_ref, cneg_ref, ki_ref, h_scr,
                    raw_scr, carry_scr, *, n_seg, idx_scale, n_fb):
    i = pl.program_id(0)
    j = pl.program_id(1)
    tm = h_scr.shape[0]
    seg = wm_ref.shape[1]

    @pl.when(j == 0)
    def _():
        h_scr[...] = _rms(x_ref[...], g_ref[...]).astype(BF16)

    def head_norm(slab, gain):
        ms = jnp.mean(slab * slab, axis=-1, keepdims=True)
        return slab * lax.rsqrt(ms + RMS_EPS) * gain

    def produce():
        for c0 in range(0, seg, MXU_COLS):
            raw_scr[j % 2, :, c0:c0 + MXU_COLS] = jnp.dot(
                h_scr[...], wm_ref[:, c0:c0 + MXU_COLS], preferred_element_type=F32)

    def finish(epilogue):
        for c0 in range(0, seg, MXU_COLS):
            epilogue(raw_scr[(j + 1) % 2, :, c0:c0 + MXU_COLS], c0)

    def per_slab(fn):
        def epilogue(acc, c0):
            for s0 in range(0, MXU_COLS, LANES):
                y = fn(acc[:, s0:s0 + LANES])
                main_ref[0, :, c0 + s0:c0 + s0 + LANES] = y.astype(BF16)
        return epilogue

    def store_values(acc, c0):
        main_ref[0, :, c0:c0 + MXU_COLS] = acc.astype(BF16)
        vt_ref[0, 0, c0:c0 + MXU_COLS, :] = acc.T.astype(BF16)

    @pl.when(j == 0)
    def _():
        produce()

    @pl.when((j == 1) | (j == 2))
    def _():
        gain = gains_ref[pl.ds(j - 1, 1), :]
        finish(per_slab(lambda y: _rotate(head_norm(y, gain), cosa_ref[...], sina_ref[...],
                                          ROT_DIM // 2, LANES)))
        produce()

    @pl.when((j == 5) | (j == 6))
    def _():
        gain = gains_ref[pl.ds(j - 3, 1), :]
        finish(per_slab(lambda y: head_norm(y, gain)))
        produce()

    @pl.when((j == 3) | (j == 7))
    def _():
        finish(store_values)
        produce()

    @pl.when(j == 4)
    def _():
        finish(per_slab(lambda y: _rotate(y, cosi_ref[...], sini_ref[...], IDX_ROT_DIM // 2, IDX_DIM)))
        produce()

    @pl.when(j == n_seg)
    def _():
        finish(per_slab(jax.nn.sigmoid))
        raw_scr[j % 2, :, 0:LANES] = jnp.dot(h_scr[...], ws_ref[...], preferred_element_type=F32)

    @pl.when(j == n_seg + 1)
    def _():
        acc = raw_scr[(j + 1) % 2, :, 0:LANES]
        lane = lax.broadcasted_iota(jnp.int32, acc.shape, 1)
        is_ki = lane < WI_OFF
        cos_k = jnp.where(is_ki, cosi_ref[...], 1.0)
        sin_k = jnp.where(is_ki, sini_ref[...], 0.0)
        ki = _rotate(acc, cos_k, sin_k, IDX_ROT_DIM // 2, IDX_DIM)
        wi = acc * idx_scale
        z = acc + bf_ref[...]
        ls = jnp.minimum(z, 0.0) - jnp.log1p(jnp.exp(-jnp.abs(z)))
        is_fb = (lane >= FB_OFF) & (lane < FB_OFF + n_fb)
        ls = jnp.where(is_fb, ls, 0.0)
        row = lax.broadcasted_iota(jnp.int32, (tm, tm), 0)
        col = lax.broadcasted_iota(jnp.int32, (tm, tm), 1)
        tri = (col <= row).astype(BF16)
        p1 = ls.astype(BF16)
        r1 = ls - p1.astype(F32)
        p2 = r1.astype(BF16)
        p3 = (r1 - p2.astype(F32)).astype(BF16)
        c = (jnp.dot(tri, p1, preferred_element_type=F32)
             + jnp.dot(tri, p2, preferred_element_type=F32)
             + jnp.dot(tri, p3, preferred_element_type=F32))

        @pl.when(i == 0)
        def _():
            carry_scr[...] = jnp.zeros_like(carry_scr)

        c = c + carry_scr[...]
        carry_scr[...] = c[tm - 1:tm, :]
        small_ref[...] = jnp.where(is_ki, ki, jnp.where(lane < FB_OFF, wi, c))
        ki_ref[...] = jnp.where(is_ki, ki, 0.0).astype(BF16)
        for hb in range(n_fb):
            cneg_ref[hb] = jnp.broadcast_to(-LOG2_E * c[:, FB_OFF + hb:FB_OFF + hb + 1], (tm, LANES))


def _in_proj(x, g, w_main, w_small, bf_pad, gains, tabs, *, n_seg, seg, tm):
    s_len, d = x.shape
    cosa, sina, cosi, sini = tabs
    grid = (s_len // tm, n_seg + 2)
    row_spec = lambda w: pl.BlockSpec((tm, w), lambda i, j: (i, 0))
    finished = lambda j: jnp.clip(j - 1, 0, n_seg - 1)
    kern = functools.partial(_in_proj_kernel, n_seg=n_seg,
                             idx_scale=float((N_IDX_HEADS * IDX_DIM) ** -0.5), n_fb=N_HEADS_B)
    return pl.pallas_call(
        kern,
        grid=grid,
        in_specs=[
            pl.BlockSpec((tm, d), lambda i, j: (i, 0), pipeline_mode=pl.Buffered(1)),
            pl.BlockSpec((1, d), lambda i, j: (0, 0)),
            pl.BlockSpec((d, seg), lambda i, j: (0, jnp.minimum(j, n_seg - 1))),
            pl.BlockSpec((d, LANES), lambda i, j: (0, 0)),
            pl.BlockSpec((1, LANES), lambda i, j: (0, 0)),
            pl.BlockSpec((4, LANES), lambda i, j: (0, 0)),
            row_spec(LANES), row_spec(LANES), row_spec(LANES), row_spec(LANES),
        ],
        out_specs=[
            pl.BlockSpec((1, tm, seg), lambda i, j: (finished(j), i, 0)),
            pl.BlockSpec((tm, LANES), lambda i, j: (i, 0)),
            pl.BlockSpec((1, 1, seg, tm), lambda i, j: (jnp.where(finished(j) <= 2, 0, 1), i, 0, 0)),
            pl.BlockSpec((N_HEADS_B, tm, LANES), lambda i, j: (0, i, 0)),
            pl.BlockSpec((tm, LANES), lambda i, j: (i, 0)),
        ],
        out_shape=[
            jax.ShapeDtypeStruct((n_seg, s_len, seg), BF16),
            jax.ShapeDtypeStruct((s_len, LANES), F32),
            jax.ShapeDtypeStruct((2, s_len // tm, seg, tm), BF16),
            jax.ShapeDtypeStruct((N_HEADS_B, s_len, LANES), F32),
            jax.ShapeDtypeStruct((s_len, LANES), BF16),
        ],
        scratch_shapes=[
            pltpu.VMEM((tm, d), BF16),
            pltpu.VMEM((2, tm, seg), F32),
            pltpu.VMEM((1, LANES), F32),
        ],
        compiler_params=_cparams(("arbitrary", "arbitrary")),
        name="in_proj",
    )(x, g, w_main, w_small, bf_pad, gains, cosa, sina, cosi, sini)


def _transpose_heads(src, dst_scr):
    for s in range(src.shape[1] // LANES):
        sl = slice(s * LANES, (s + 1) * LANES)
        dst_scr[sl, :] = src[:, sl].astype(F32).T.astype(BF16)


def _logits_pass(h, z_t, z_scr, m_scr, mnew_scr):
    z_scr[h] = z_t
    mnew_scr[h:h + 1, :] = jnp.maximum(m_scr[h:h + 1, :], jnp.max(z_t, axis=0, keepdims=True))


def _softmax_pass(h, v_t, z_scr, m_scr, mnew_scr, l_scr, acc_scr):
    hs = slice(h * HEAD_DIM, (h + 1) * HEAD_DIM)
    m_new = mnew_scr[h:h + 1, :]
    alpha = jnp.exp2(m_scr[h:h + 1, :] - m_new)
    p_t = jnp.exp2(z_scr[h] - m_new)
    l_scr[h:h + 1, :] = alpha * l_scr[h:h + 1, :] + jnp.sum(p_t, axis=0, keepdims=True)
    acc_scr[hs, :] = alpha * acc_scr[hs, :] + jnp.dot(v_t, p_t.astype(BF16),
                                                      preferred_element_type=F32)
    m_scr[h:h + 1, :] = m_new


def _init_attn_state(m_scr, l_scr, acc_scr):
    m_scr[...] = jnp.full_like(m_scr, NEG_BIG)
    l_scr[...] = jnp.zeros_like(l_scr)
    acc_scr[...] = jnp.zeros_like(acc_scr)


def _dsa_t_kernel(qa_ref, qi_ref, wit_ref, ki_ref, ka_ref, vat_ref, o_ref,
                  key_scr, half_scr, qat_scr, qit_scr, z_scr, m_scr, mnew_scr, l_scr, acc_scr,
                  *, tq, tk, top_k, n_heads, n_idx):
    i = pl.program_id(0)
    n_kt = ((i + 1) * tq + tk - 1) // tk
    t_glob = i * tq + lax.broadcasted_iota(jnp.int32, (1, tq), 1)
    adm_len = (t_glob // CHUNK + 1) * CHUNK
    key_pos = lax.broadcasted_iota(jnp.int32, (tk, 1), 0)

    _transpose_heads(qa_ref[0], qat_scr)
    _transpose_heads(qi_ref[0], qit_scr)

    def score_tile(kt, carry):
        k0 = pl.multiple_of(kt * tk, tk)
        k_idx = ki_ref[pl.ds(k0, tk), 0:IDX_DIM]
        acc = jnp.zeros((tk, tq), F32)
        for h in range(n_idx):
            q_t = qit_scr[h * IDX_DIM:(h + 1) * IDX_DIM, :]
            rel = jnp.maximum(jnp.dot(k_idx, q_t, preferred_element_type=F32), 0.0)
            acc = acc + rel * wit_ref[h:h + 1, :]
        bits = lax.bitcast_convert_type(acc, jnp.int32)
        key = jnp.where(bits < 0, bits ^ INT_MAX, bits)
        key = jnp.where(k0 + key_pos < adm_len, key, KEY_NONE)
        key_scr[kt] = key
        half_scr[kt] = (key >> 16).astype(jnp.int16)
        return carry

    lax.fori_loop(0, n_kt, score_tile, 0)

    def count(pred):
        def body(kt, cnt):
            hit = jnp.where(pred(key_scr[kt], kt * tk + key_pos), 1, 0)
            return cnt + jnp.sum(hit.reshape(tk // 8, 8, tq), axis=0)

        cnt = lax.fori_loop(0, n_kt, body, jnp.zeros((8, tq), jnp.int32))
        return jnp.sum(cnt, axis=0, keepdims=True)

    def count_half_ge(mid):
        mid16 = mid.astype(jnp.int16)

        def body(kt, cnt):
            hit = jnp.where(half_scr[kt] >= mid16, jnp.int16(1), jnp.int16(0))
            parts = [hit[r * 16:(r + 1) * 16] for r in range(tk // 16)]
            while len(parts) > 1:
                parts = [parts[a] + parts[a + 1] for a in range(0, len(parts), 2)]
            return cnt + parts[0].astype(jnp.int32)

        cnt = lax.fori_loop(0, n_kt, body, jnp.zeros((16, tq), jnp.int32))
        return jnp.sum(cnt, axis=0, keepdims=True)

    def bisect_half(need_k, c_all):
        def step(_, state):
            lo, hi, c_lo, c_hi = state
            mid = (lo + hi) >> 1
            c_mid = count_half_ge(mid)
            take = c_mid >= need_k
            return (jnp.where(take, mid, lo), jnp.where(take, hi, mid),
                    jnp.where(take, c_mid, c_lo), jnp.where(take, c_hi, c_mid))

        x, _, c_x, c_above = lax.fori_loop(0, 16, step,
                                           (row(HALF_MIN), row(HALF_MAX + 1), c_all, row(0)))
        return x, c_x, c_above

    row = lambda v: jnp.full((1, tq), v, jnp.int32)
    thr_hi, c_ge_hi, c_gt_hi = bisect_half(top_k, n_kt * tk + row(0))
    thr_hi16 = thr_hi.astype(jnp.int16)

    def low_half_tile(kt, carry):
        low = ((key_scr[kt] & 0xFFFF) + HALF_MIN).astype(jnp.int16)
        half_scr[kt] = jnp.where(half_scr[kt] == thr_hi16, low, jnp.int16(HALF_MIN))
        return carry

    lax.fori_loop(0, n_kt, low_half_tile, 0)
    thr_lo, c_ge_lo, c_gt_lo = bisect_half(top_k - c_gt_hi, c_ge_hi - c_gt_hi)
    thr = thr_hi * 65536 + (thr_lo - HALF_MIN)
    c_ge = c_gt_hi + c_ge_lo
    c_gt = c_gt_hi + c_gt_lo
    short = thr <= KEY_NONE
    thr = jnp.maximum(thr, KEY_NONE)
    need = top_k - c_gt

    def tie_cut():
        def step(_, lohi):
            lo, hi = lohi
            mid = (lo + hi) >> 1
            ok = count(lambda k, pos: (k == thr) & (pos <= mid)) >= need
            return jnp.where(ok, lo, mid), jnp.where(ok, mid, hi)

        n_steps = int(np.ceil(np.log2(key_scr.shape[0] * tk))) + 1
        return lax.fori_loop(0, n_steps, step, (row(-1), n_kt * tk - 1 + row(0)))[1]

    has_tie = jnp.max(jnp.where((c_ge > top_k) & ~short, 1.0, 0.0)) > 0.5
    cut = lax.cond(has_tie, tie_cut, lambda: row(INT_MAX))
    cut = jnp.where(short, -1, cut)

    _init_attn_state(m_scr, l_scr, acc_scr)

    def attend(kt, carry):
        k0 = pl.multiple_of(kt * tk, tk)
        keys = key_scr[kt]
        sel = (keys > thr) | ((keys == thr) & (kt * tk + key_pos <= cut))
        bias = jnp.where(sel, 0.0, NEG_BIG).astype(F32)
        for h in range(n_heads):
            hs = slice(h * HEAD_DIM, (h + 1) * HEAD_DIM)
            z_t = jnp.dot(ka_ref[0, pl.ds(k0, tk), hs], qat_scr[hs, :],
                          preferred_element_type=F32) + bias
            _logits_pass(h, z_t, z_scr, m_scr, mnew_scr)
        for h in range(n_heads):
            hs = slice(h * HEAD_DIM, (h + 1) * HEAD_DIM)
            _softmax_pass(h, vat_ref[0, kt, hs, :], z_scr, m_scr, mnew_scr, l_scr, acc_scr)
        return carry

    lax.fori_loop(0, n_kt, attend, 0)
    for h in range(n_heads):
        hs = slice(h * HEAD_DIM, (h + 1) * HEAD_DIM)
        o_ref[:, hs] = (acc_scr[hs, :] / l_scr[h:h + 1, :]).T.astype(BF16)


def _dsa_t(main, vt, ki, wi_t, *, tq, tk, top_k):
    n_seg, s_len, seg = main.shape
    assert vt.shape == (2, s_len // tk, seg, tk)
    kern = functools.partial(_dsa_t_kernel, tq=tq, tk=tk, top_k=top_k, n_heads=N_HEADS_A,
                             n_idx=N_IDX_HEADS)
    resident = functools.partial(pl.BlockSpec, pipeline_mode=pl.Buffered(1))
    return pl.pallas_call(
        kern,
        grid=(s_len // tq,),
        in_specs=[
            pl.BlockSpec((1, tq, seg), lambda i: (0, i, 0)),
            pl.BlockSpec((1, tq, seg), lambda i: (3, i, 0)),
            pl.BlockSpec((N_IDX_HEADS, tq), lambda i: (0, i)),
            resident((s_len, LANES), lambda i: (0, 0)),
            resident((1, s_len, seg), lambda i: (1, 0, 0)),
            resident((1, s_len // tk, seg, tk), lambda i: (0, 0, 0, 0)),
        ],
        out_specs=pl.BlockSpec((tq, seg), lambda i: (i, 0)),
        out_shape=jax.ShapeDtypeStruct((s_len, seg), BF16),
        scratch_shapes=[
            pltpu.VMEM((s_len // tk, tk, tq), jnp.int32),
            pltpu.VMEM((s_len // tk, tk, tq), jnp.int16),
            pltpu.VMEM((seg, tq), BF16),
            pltpu.VMEM((seg, tq), BF16),
            pltpu.VMEM((N_HEADS_A, tk, tq), F32),
            pltpu.VMEM((N_HEADS_A, tq), F32),
            pltpu.VMEM((N_HEADS_A, tq), F32),
            pltpu.VMEM((N_HEADS_A, tq), F32),
            pltpu.VMEM((seg, tq), F32),
        ],
        compiler_params=_cparams(("arbitrary",)),
        name="dsa",
    )(main, main, wi_t, ki, main, vt)


def _fox_t_kernel(qt_ref, kt_ref, q_ref, k_ref, vt_ref, gate_ref, cneg_ref, o_ref,
                  qt_scr, z_scr, m_scr, mnew_scr, l_scr, acc_scr, *, t, n_heads):
    p = pl.program_id(0)
    qi = qt_ref[p]
    ki = kt_ref[p]

    @pl.when(ki == 0)
    def _():
        _transpose_heads(q_ref[0], qt_scr)
        _init_attn_state(m_scr, l_scr, acc_scr)

    def update(causal):
        key_i = lax.broadcasted_iota(jnp.int32, (t, LANES), 0)
        qry_i = lax.broadcasted_iota(jnp.int32, (t, LANES), 1)
        for h in range(n_heads):
            hs = slice(h * HEAD_DIM, (h + 1) * HEAD_DIM)
            z_t = jnp.dot(k_ref[0, :, hs], qt_scr[hs, :], preferred_element_type=F32)
            cols = []
            for c in range(t // LANES):
                zc = z_t[:, c * LANES:(c + 1) * LANES] + cneg_ref[h]
                if causal:
                    zc = jnp.where(key_i <= qry_i + c * LANES, zc, NEG_BIG)
                cols.append(zc)
            _logits_pass(h, jnp.concatenate(cols, axis=1), z_scr, m_scr, mnew_scr)
        for h in range(n_heads):
            hs = slice(h * HEAD_DIM, (h + 1) * HEAD_DIM)
            _softmax_pass(h, vt_ref[0, 0, hs, :], z_scr, m_scr, mnew_scr, l_scr, acc_scr)

    @pl.when(ki < qi)
    def _():
        update(causal=False)

    @pl.when(ki == qi)
    def _():
        update(causal=True)
        for h in range(n_heads):
            hs = slice(h * HEAD_DIM, (h + 1) * HEAD_DIM)
            o = (acc_scr[hs, :] / l_scr[h:h + 1, :]).T
            o_ref[:, hs] = (o * gate_ref[0, :, hs].astype(F32)).astype(BF16)


def _fox_t(main, vt, cneg, *, t):
    n_seg, s_len, seg = main.shape
    assert vt.shape == (2, s_len // t, seg, t)
    nq = s_len // t
    pairs = [(a, b) for a in range(nq) for b in range(a + 1)]
    q_tab = jnp.asarray([a for a, _ in pairs], jnp.int32)
    k_tab = jnp.asarray([b for _, b in pairs], jnp.int32)
    kern = functools.partial(_fox_t_kernel, t=t, n_heads=N_HEADS_B)
    grid_spec = pltpu.PrefetchScalarGridSpec(
        num_scalar_prefetch=2,
        grid=(len(pairs),),
        in_specs=[
            pl.BlockSpec((1, t, seg), lambda p, qt, kt: (4, qt[p], 0)),
            pl.BlockSpec((1, t, seg), lambda p, qt, kt: (5, kt[p], 0)),
            pl.BlockSpec((1, 1, seg, t), lambda p, qt, kt: (1, kt[p], 0, 0)),
            pl.BlockSpec((1, t, seg), lambda p, qt, kt: (7, qt[p], 0)),
            pl.BlockSpec((N_HEADS_B, t, LANES), lambda p, qt, kt: (0, kt[p], 0)),
        ],
        out_specs=pl.BlockSpec((t, seg), lambda p, qt, kt: (qt[p], 0)),
        scratch_shapes=[
            pltpu.VMEM((seg, t), BF16),
            pltpu.VMEM((N_HEADS_B, t, t), F32),
            pltpu.VMEM((N_HEADS_B, t), F32),
            pltpu.VMEM((N_HEADS_B, t), F32),
            pltpu.VMEM((N_HEADS_B, t), F32),
            pltpu.VMEM((seg, t), F32),
        ],
    )
    return pl.pallas_call(
        kern,
        grid_spec=grid_spec,
        out_shape=jax.ShapeDtypeStruct((s_len, seg), BF16),
        compiler_params=_cparams(("arbitrary",)),
        name="fox",
    )(q_tab, k_tab, main, main, vt, main, cneg)


def _out_proj_kernel(x_ref, oa_ref, ob_ref, wa_ref, wb_ref, y_ref):
    y = x_ref[...] + jnp.dot(oa_ref[...], wa_ref[...], preferred_element_type=F32)
    y_ref[...] = y + jnp.dot(ob_ref[...], wb_ref[...], preferred_element_type=F32)


def _out_proj(x, o_a, o_b, w_a, w_b, *, tm):
    s_len, d = x.shape
    seg = o_a.shape[1]
    resident = functools.partial(pl.BlockSpec, pipeline_mode=pl.Buffered(1))
    return pl.pallas_call(
        _out_proj_kernel,
        grid=(s_len // tm,),
        in_specs=[
            pl.BlockSpec((tm, d), lambda i: (i, 0)),
            pl.BlockSpec((tm, seg), lambda i: (i, 0)),
            pl.BlockSpec((tm, seg), lambda i: (i, 0)),
            resident((seg, d), lambda i: (0, 0)),
            resident((seg, d), lambda i: (0, 0)),
        ],
        out_specs=pl.BlockSpec((tm, d), lambda i: (i, 0)),
        out_shape=jax.ShapeDtypeStruct((s_len, d), F32),
        compiler_params=_cparams(("arbitrary",)),
        name="out_proj",
    )(x, o_a, o_b, w_a, w_b)


def _swiglu_step(h, wg, wu, wd):
    g = jnp.dot(h, wg.astype(BF16), preferred_element_type=F32)
    u = jnp.dot(h, wu.astype(BF16), preferred_element_type=F32)
    a = (g * jax.nn.sigmoid(g) * u).astype(BF16)
    return jnp.dot(a, wd.astype(BF16), preferred_element_type=F32)


def _ffn_kernel(x_ref, g_ref, wg_ref, wu_ref, wd_ref, y_ref, h_scr):
    @pl.when(pl.program_id(1) == 0)
    def _():
        x = x_ref[...]
        h_scr[...] = _rms(x, g_ref[...]).astype(BF16)
        y_ref[...] = x

    y_ref[...] += _swiglu_step(h_scr[...], wg_ref[...], wu_ref[...], wd_ref[...])


def _ffn(x, g, wg, wu, wd, *, tm, tf):
    s_len, d = x.shape
    f = wg.shape[1]
    return pl.pallas_call(
        _ffn_kernel,
        grid=(s_len // tm, f // tf),
        in_specs=[
            pl.BlockSpec((tm, d), lambda i, j: (i, 0), pipeline_mode=pl.Buffered(1)),
            pl.BlockSpec((1, d), lambda i, j: (0, 0)),
            pl.BlockSpec((d, tf), lambda i, j: (0, j)),
            pl.BlockSpec((d, tf), lambda i, j: (0, j)),
            pl.BlockSpec((tf, d), lambda i, j: (j, 0)),
        ],
        out_specs=pl.BlockSpec((tm, d), lambda i, j: (i, 0), pipeline_mode=pl.Buffered(1)),
        out_shape=jax.ShapeDtypeStruct((s_len, d), F32),
        scratch_shapes=[pltpu.VMEM((tm, d), BF16)],
        compiler_params=_cparams(("arbitrary", "arbitrary")),
        name="ffn",
    )(x, g, wg, wu, wd)


def _pool_kernel(x_ref, g_ref, w_ref, sc_ref, gm_ref, r_ref, y_ref, hp_ref, route_ref, cnt_ref,
                 ext_scr, carry_scr, *, halo, n_exp):
    i = pl.program_id(0)
    tm, d = x_ref.shape
    n_grp = len(POOL_WINDOWS)
    width = d // n_grp

    @pl.when(i == 0)
    def _():
        ext_scr[0:halo, :] = jnp.zeros((halo, d), F32)

    x = x_ref[...]
    ext_scr[halo:halo + tm, :] = _rms(x, g_ref[...])
    t_glob = i * tm + lax.broadcasted_iota(jnp.int32, (tm, 1), 0)
    for gi, win in enumerate(POOL_WINDOWS):
        cs = slice(gi * width, (gi + 1) * width)
        hn = ext_scr[halo:halo + tm, cs]
        tot = hn
        for dlt in range(1, win):
            tot = tot + ext_scr[halo - dlt:halo - dlt + tm, cs]
        count = jnp.minimum(t_glob + 1, win).astype(F32)
        pooled = (tot / count - hn).astype(BF16)
        y = jnp.dot(pooled, w_ref[gi], preferred_element_type=F32)
        y_ref[:, cs] = x[:, cs] + y * sc_ref[:, cs]
    ext_scr[0:halo, :] = ext_scr[tm:tm + halo, :]

    h = _rms(y_ref[...], gm_ref[...])
    hp_ref[...] = _pack_bf16_pairs(h)
    _route_rows(h, r_ref, route_ref, cnt_ref, carry_scr, n_exp=n_exp)


def _pool_route(x, g, w_pool, scale, g_moe, r_pad, *, tm, n_exp):
    s_len, d = x.shape
    halo = 16
    assert max(POOL_WINDOWS) <= halo
    n_grp, width, _ = w_pool.shape
    row_out = lambda w: pl.BlockSpec((tm, w), lambda i: (i, 0))
    return pl.pallas_call(
        functools.partial(_pool_kernel, halo=halo, n_exp=n_exp),
        grid=(s_len // tm,),
        in_specs=[
            pl.BlockSpec((tm, d), lambda i: (i, 0)),
            pl.BlockSpec((1, d), lambda i: (0, 0)),
            pl.BlockSpec((n_grp, width, width), lambda i: (0, 0, 0)),
            pl.BlockSpec((1, d), lambda i: (0, 0)),
            pl.BlockSpec((1, d), lambda i: (0, 0)),
            pl.BlockSpec((2, d, LANES), lambda i: (0, 0, 0)),
        ],
        out_specs=[row_out(d), row_out(d // 2), row_out(LANES),
                   pl.BlockSpec((1, LANES), lambda i: (0, 0))],
        out_shape=[
            jax.ShapeDtypeStruct((s_len, d), F32),
            jax.ShapeDtypeStruct((s_len, d // 2), jnp.uint32),
            jax.ShapeDtypeStruct((s_len, LANES), F32),
            jax.ShapeDtypeStruct((1, LANES), F32),
        ],
        scratch_shapes=[pltpu.VMEM((tm + halo, d), F32), pltpu.VMEM((1, LANES), F32)],
        compiler_params=_cparams(("arbitrary",)),
        name="pool_route",
    )(x, g, w_pool, scale, g_moe, r_pad)


R_E1, R_E2, R_G1, R_G2, R_K1, R_K2 = range(6)


def _route_rows(h, r_ref, route_ref, cnt_ref, carry_scr, *, n_exp):
    tm = h.shape[0]

    @pl.when(pl.program_id(0) == 0)
    def _():
        carry_scr[...] = jnp.zeros_like(carry_scr)

    h_hi = h.astype(BF16)
    h_lo = (h - h_hi.astype(F32)).astype(BF16)
    logits = (jnp.dot(h_hi, r_ref[0], preferred_element_type=F32)
              + jnp.dot(h_hi, r_ref[1], preferred_element_type=F32)
              + jnp.dot(h_lo, r_ref[0], preferred_element_type=F32))
    lane = lax.broadcasted_iota(jnp.int32, logits.shape, 1)
    logits = jnp.where(lane < n_exp, logits, -jnp.inf)
    lane_f = lane.astype(F32)
    v1 = jnp.max(logits, axis=1, keepdims=True)
    e1 = jnp.min(jnp.where(logits == v1, lane_f, float(LANES)), axis=1, keepdims=True)
    rest = jnp.where(lane_f == e1, -jnp.inf, logits)
    v2 = jnp.max(rest, axis=1, keepdims=True)
    e2 = jnp.min(jnp.where(rest == v2, lane_f, float(LANES)), axis=1, keepdims=True)
    ex = jnp.exp(v2 - v1)
    g1 = 1.0 / (1.0 + ex)
    g2 = ex / (1.0 + ex)
    oh1 = (lane_f == e1).astype(F32)
    oh2 = (lane_f == e2).astype(F32)
    both = oh1 + oh2
    row = lax.broadcasted_iota(jnp.int32, (tm, tm), 0)
    col = lax.broadcasted_iota(jnp.int32, (tm, tm), 1)
    tri = (col < row).astype(BF16)
    before = jnp.dot(tri, both.astype(BF16), preferred_element_type=F32) + carry_scr[...]
    k1 = jnp.sum(before * oh1, axis=1, keepdims=True)
    k2 = jnp.sum(before * oh2, axis=1, keepdims=True)
    carry_scr[...] = carry_scr[...] + jnp.sum(both, axis=0, keepdims=True)
    cnt_ref[...] = carry_scr[...]
    vals = (e1, e2, g1, g2, k1, k2)
    out = jnp.zeros(logits.shape, F32)
    for li, v in enumerate(vals):
        out = jnp.where(lane == li, v, out)
    route_ref[...] = out


def _pack_bf16_pairs(h):
    half = h.shape[1] // 2
    lo = lax.bitcast_convert_type(h[:, :half].astype(BF16).astype(F32), jnp.uint32)
    hi = lax.bitcast_convert_type(h[:, half:].astype(BF16).astype(F32), jnp.uint32)
    return (lo >> 16) | (hi & jnp.uint32(0xFFFF0000))


def _unpack_bf16_pairs(u):
    lo = lax.bitcast_convert_type(u << 16, F32).astype(BF16)
    hi = lax.bitcast_convert_type(u & jnp.uint32(0xFFFF0000), F32).astype(BF16)
    return lo, hi


ROW_COPY_UNROLL = 8


def _run_row_copies(copies, n):
    def start(t, c):
        a, b = copies(t)
        a.start(priority=0)
        b.start(priority=1)
        return c

    def wait(t, c):
        a, b = copies(t)
        a.wait()
        b.wait()
        return c

    lax.fori_loop(0, n, start, 0, unroll=ROW_COPY_UNROLL)
    lax.fori_loop(0, n, wait, 0, unroll=ROW_COPY_UNROLL)


def _scatter_kernel(p1_ref, p2_ref, hp_ref, xs_in_ref, xs_ref, sem):
    del xs_in_ref
    i = pl.program_id(0)
    tm = hp_ref.shape[0]

    def copies(t):
        src = hp_ref.at[pl.ds(t, 1), :]
        return (pltpu.make_async_copy(src, xs_ref.at[pl.ds(p1_ref[i * tm + t], 1), :], sem.at[0]),
                pltpu.make_async_copy(src, xs_ref.at[pl.ds(p2_ref[i * tm + t], 1), :], sem.at[1]))

    _run_row_copies(copies, tm)


def _scatter(pos1, pos2, hp, xs_zero, *, tm):
    s_len, half = hp.shape
    grid_spec = pltpu.PrefetchScalarGridSpec(
        num_scalar_prefetch=2,
        grid=(s_len // tm,),
        in_specs=[
            pl.BlockSpec((tm, half), lambda i, p1, p2: (i, 0)),
            pl.BlockSpec(memory_space=pl.ANY),
        ],
        out_specs=pl.BlockSpec(memory_space=pl.ANY),
        scratch_shapes=[pltpu.SemaphoreType.DMA((2,))],
    )
    return pl.pallas_call(
        _scatter_kernel,
        grid_spec=grid_spec,
        out_shape=jax.ShapeDtypeStruct(xs_zero.shape, jnp.uint32),
        input_output_aliases={3: 0},
        compiler_params=_cparams(("arbitrary",)),
        name="moe_scatter",
    )(pos1, pos2, hp, xs_zero)


def _experts_kernel(e_ref, v_ref, xs_ref, wg_ref, wu_ref, wd_ref, o_ref, xb_scr):
    j = pl.program_id(0)
    f = pl.program_id(1)
    half = xs_ref.shape[1]

    @pl.when(f == 0)
    def _():
        o_ref[...] = jnp.zeros_like(o_ref)

    @pl.when(v_ref[j] > 0)
    def _():
        @pl.when(f == 0)
        def _():
            lo, hi = _unpack_bf16_pairs(xs_ref[...])
            xb_scr[:, :half] = lo
            xb_scr[:, half:] = hi

        o_ref[...] += _swiglu_step(xb_scr[...], wg_ref[0], wu_ref[0], wd_ref[0])


def _experts(item_e, item_rows, xs, wg, wu, wd, *, r, tf):
    rows, half = xs.shape
    n_exp, d, f = wg.shape
    n_items = rows // r
    n_f = f // tf

    def f_idx(fi, j, v):
        return jnp.where(v[j] > 0, fi, n_f - 1)

    grid_spec = pltpu.PrefetchScalarGridSpec(
        num_scalar_prefetch=2,
        grid=(n_items, n_f),
        in_specs=[
            pl.BlockSpec((r, half), lambda j, fi, e, v: (j, 0)),
            pl.BlockSpec((1, d, tf), lambda j, fi, e, v: (e[j], 0, f_idx(fi, j, v))),
            pl.BlockSpec((1, d, tf), lambda j, fi, e, v: (e[j], 0, f_idx(fi, j, v))),
            pl.BlockSpec((1, tf, d), lambda j, fi, e, v: (e[j], f_idx(fi, j, v), 0)),
        ],
        out_specs=pl.BlockSpec((r, d), lambda j, fi, e, v: (j, 0)),
        scratch_shapes=[pltpu.VMEM((r, d), BF16)],
    )
    return pl.pallas_call(
        _experts_kernel,
        grid_spec=grid_spec,
        out_shape=jax.ShapeDtypeStruct((rows, d), F32),
        compiler_params=_cparams(("arbitrary", "arbitrary")),
        name="moe_experts",
    )(item_e, item_rows, xs, wg, wu, wd)


def _combine_kernel(p1_ref, p2_ref, x_ref, route_ref, o_hbm, y_ref, a_scr, b_scr, sem):
    i = pl.program_id(0)
    tm = x_ref.shape[0]

    def copies(t):
        return (pltpu.make_async_copy(o_hbm.at[pl.ds(p1_ref[i * tm + t], 1), :],
                                      a_scr.at[pl.ds(t, 1), :], sem.at[0]),
                pltpu.make_async_copy(o_hbm.at[pl.ds(p2_ref[i * tm + t], 1), :],
                                      b_scr.at[pl.ds(t, 1), :], sem.at[1]))

    _run_row_copies(copies, tm)
    route = route_ref[...]
    g1 = route[:, R_G1:R_G1 + 1]
    g2 = route[:, R_G2:R_G2 + 1]
    y_ref[...] = x_ref[...] + g1 * a_scr[...] + g2 * b_scr[...]


def _combine(pos1, pos2, x, route, o_sorted, *, tm):
    s_len, d = x.shape
    grid_spec = pltpu.PrefetchScalarGridSpec(
        num_scalar_prefetch=2,
        grid=(s_len // tm,),
        in_specs=[
            pl.BlockSpec((tm, d), lambda i, p1, p2: (i, 0)),
            pl.BlockSpec((tm, LANES), lambda i, p1, p2: (i, 0)),
            pl.BlockSpec(memory_space=pl.ANY),
        ],
        out_specs=pl.BlockSpec((tm, d), lambda i, p1, p2: (i, 0)),
        scratch_shapes=[
            pltpu.VMEM((tm, d), F32),
            pltpu.VMEM((tm, d), F32),
            pltpu.SemaphoreType.DMA((2,)),
        ],
    )
    return pl.pallas_call(
        _combine_kernel,
        grid_spec=grid_spec,
        out_shape=jax.ShapeDtypeStruct((s_len, d), F32),
        compiler_params=_cparams(("arbitrary",)),
        name="moe_combine",
    )(pos1, pos2, x, route, o_sorted)


def _rotary_tables(s_len, rot_dim, period):
    half = rot_dim // 2
    inv_freq = 1.0 / (ROPE_THETA ** (jnp.arange(half, dtype=F32) * 2.0 / rot_dim))
    ang = jnp.arange(s_len, dtype=F32)[:, None] * inv_freq[None, :]
    cos, sin = jnp.cos(ang), jnp.sin(ang)
    pad = period - rot_dim
    cos_p = jnp.concatenate([cos, cos, jnp.ones((s_len, pad), F32)], axis=1)
    sin_p = jnp.concatenate([-sin, sin, jnp.zeros((s_len, pad), F32)], axis=1)
    reps = LANES // period
    return jnp.tile(cos_p, (1, reps)), jnp.tile(sin_p, (1, reps))


def _split_w_in(w_in):
    seg_a = N_HEADS_A * HEAD_DIM
    seg_i = N_IDX_HEADS * IDX_DIM
    seg_b = N_HEADS_B * HEAD_DIM
    sizes = (seg_a, seg_a, seg_a, seg_i, IDX_DIM, N_IDX_HEADS, seg_b, seg_b, seg_b, N_HEADS_B, seg_b)
    offs = np.cumsum(sizes)[:-1].tolist()
    return jnp.split(w_in, offs, axis=1)


def _moe_tables(counts, e1, e2, k1, k2, *, r, n_items):
    n_exp = counts.shape[0]
    blocks = (counts + r - 1) // r
    bend = jnp.cumsum(blocks)
    bstart = bend - blocks
    j = jnp.arange(n_items, dtype=jnp.int32)
    item_e = jnp.sum((j[:, None] >= bend[None, :]).astype(jnp.int32), axis=1)
    valid = item_e < n_exp
    e_c = jnp.minimum(item_e, n_exp - 1)
    item_rows = jnp.where(valid, jnp.clip(counts[e_c] - (j - bstart[e_c]) * r, 0, r), 0)
    last_e = jnp.minimum(item_e[jnp.maximum(bend[-1] - 1, 0)], n_exp - 1)
    item_e = jnp.where(valid, item_e, last_e).astype(jnp.int32)
    pos1 = bstart[e1] * r + k1
    pos2 = bstart[e2] * r + k2
    return item_e, item_rows.astype(jnp.int32), pos1.astype(jnp.int32), pos2.astype(jnp.int32)


def _moe_block_rows(n_assign, n_exp):
    share = -(-n_assign // (2 * n_exp))
    return -(-(share + 3 * share // 64) // BF16_SUBLANES) * BF16_SUBLANES


class _Tiles(NamedTuple):
    attn: int
    dsa_q: int
    out_rows: int
    ffn_rows: int
    ffn_cols: int
    pool_rows: int
    moe_rows: int
    expert_cols: int


def _tiles(s_len, d_ff, d_ff_expert):
    return _Tiles(attn=_tile(s_len, 512), dsa_q=_tile(s_len, 256), out_rows=_tile(s_len, 512),
                  ffn_rows=_tile(s_len, 1024), ffn_cols=_tile(d_ff, 512),
                  pool_rows=_tile(s_len, 512), moe_rows=_tile(s_len, 512),
                  expert_cols=_tile(d_ff_expert, 256))


def kernel(x, attn_norm_e, w_in_e, b_forget_e, q_norm_a_e, k_norm_a_e, q_norm_b_e, k_norm_b_e,
           w_out_e, ffn_norm_e, w_gate_e, w_up_e, w_down_e, pool_norm_o, w_pool_o, pool_scale_o,
           moe_norm_o, router_o, w_gate_o, w_up_o, w_down_o):
    b, s_len, d = x.shape
    assert b == 1
    x0 = x[0]
    tiles = _tiles(s_len, w_gate_e.shape[2], w_gate_o.shape[3])
    seg = N_HEADS_A * HEAD_DIM
    assert seg == N_IDX_HEADS * IDX_DIM == N_HEADS_B * HEAD_DIM
    assert N_IDX_HEADS <= FB_OFF - WI_OFF and N_HEADS_B <= LANES - FB_OFF
    row = lambda v: v.reshape(1, -1)

    qa, ka, va, qi, ki, wi, qb, kb, vb, fb, gb = _split_w_in(w_in_e[0])
    w_main = jnp.concatenate([qa, ka, va, qi, qb, kb, vb, gb], axis=1).astype(BF16)
    w_small = jnp.concatenate(
        [ki, wi, jnp.zeros((d, FB_OFF - WI_OFF - N_IDX_HEADS), F32), fb,
         jnp.zeros((d, LANES - FB_OFF - N_HEADS_B), F32)], axis=1).astype(BF16)
    bf_pad = jnp.zeros((1, LANES), F32).at[0, FB_OFF:FB_OFF + N_HEADS_B].set(b_forget_e[0])
    scale = HEAD_DIM ** -0.5 * LOG2_E
    gains = jnp.stack([q_norm_a_e[0] * scale, k_norm_a_e[0], q_norm_b_e[0] * scale, k_norm_b_e[0]])
    tabs = _rotary_tables(s_len, ROT_DIM, HEAD_DIM) + _rotary_tables(s_len, IDX_ROT_DIM, IDX_DIM)
    main, small, v_t, c_neg, ki_rot = _in_proj(x0, row(attn_norm_e[0]), w_main, w_small, bf_pad, gains,
                                               tabs, n_seg=8, seg=seg, tm=tiles.attn)
    wi_t = small[:, WI_OFF:WI_OFF + N_IDX_HEADS].T

    top_k = min(TOPK_MAX, s_len // 4)
    o_a = _dsa_t(main, v_t, ki_rot, wi_t, tq=tiles.dsa_q, tk=tiles.attn, top_k=top_k)
    o_b = _fox_t(main, v_t, c_neg, t=tiles.attn)
    w_out = w_out_e[0].astype(BF16)
    x1 = _out_proj(x0, o_a, o_b, w_out[:seg], w_out[seg:], tm=tiles.out_rows)

    x2 = _ffn(x1, row(ffn_norm_e[0]), w_gate_e[0], w_up_e[0], w_down_e[0],
              tm=tiles.ffn_rows, tf=tiles.ffn_cols)

    n_exp = router_o.shape[2]
    r_f32 = jnp.zeros((d, LANES), F32).at[:, :n_exp].set(router_o[0])
    r_hi = r_f32.astype(BF16)
    r_pad = jnp.stack([r_hi, (r_f32 - r_hi.astype(F32)).astype(BF16)])
    x3, h3_packed, route, counts = _pool_route(
        x2, row(pool_norm_o[0]), w_pool_o[0].astype(BF16), row(pool_scale_o[0]),
        row(moe_norm_o[0]), r_pad, tm=tiles.pool_rows, n_exp=n_exp)

    r_rows = _moe_block_rows(2 * s_len, n_exp)
    n_items = (2 * s_len) // r_rows + n_exp
    as_int = lambda c: route[:, c].astype(jnp.int32)
    item_e, item_rows, pos1, pos2 = _moe_tables(
        counts[0, :n_exp].astype(jnp.int32), as_int(R_E1), as_int(R_E2), as_int(R_K1), as_int(R_K2),
        r=r_rows, n_items=n_items)
    xs_zero = jnp.zeros((n_items * r_rows, d // 2), jnp.uint32)
    xs = _scatter(pos1, pos2, h3_packed, xs_zero, tm=tiles.moe_rows)
    o_sorted = _experts(item_e, item_rows, xs, w_gate_o[0], w_up_o[0], w_down_o[0],
                        r=r_rows, tf=tiles.expert_cols)
    y = _combine(pos1, pos2, x3, route, o_sorted, tm=tiles.moe_rows)
    return y[None]
```

```python
import functools
from typing import NamedTuple

import jax
import jax.numpy as jnp
import numpy as np
from jax import lax
from jax.experimental import pallas as pl
from jax.experimental.pallas import tpu as pltpu

HEAD_DIM = 128
N_HEADS_A = 8
N_HEADS_B = 8
N_IDX_HEADS = 16
IDX_DIM = 64
ROT_DIM = HEAD_DIM // 4
IDX_ROT_DIM = IDX_DIM // 4
ROPE_THETA = 500000.0
CHUNK = 64
TOPK_MAX = 256
POOL_WINDOWS = (2, 4, 8, 16)
RMS_EPS = 1e-6

LANES = 128
BF16_SUBLANES = 16
MXU_COLS = 256
VMEM_LIMIT_BYTES = 56 * 1024 * 1024

WI_OFF = IDX_DIM
FB_OFF = 80

LOG2_E = 1.4426950408889634
NEG_BIG = -1e30
INT_MIN = -(2 ** 31)
INT_MAX = 2 ** 31 - 1
KEY_NONE = INT_MIN + 1
HALF_MIN = -(2 ** 15)
HALF_MAX = 2 ** 15 - 1

F32 = jnp.float32
BF16 = jnp.bfloat16


def _cparams(sem):
    return pltpu.CompilerParams(dimension_semantics=sem, vmem_limit_bytes=VMEM_LIMIT_BYTES)


def _tile(n, pref):
    t = min(n, pref)
    while n % t:
        t //= 2
    return t


def _rms(x, g):
    ms = jnp.mean(x * x, axis=-1, keepdims=True)
    return x * lax.rsqrt(ms + RMS_EPS) * g


def _rotate(y, cos_t, sin_t, half, period):
    lane = lax.broadcasted_iota(jnp.int32, y.shape, 1)
    up = pltpu.roll(y, LANES - half, 1)
    dn = pltpu.roll(y, half, 1)
    rot = jnp.where((lane & (period - 1)) < half, up, dn)
    return y * cos_t + rot * sin_t


def _in_proj_kernel(x_ref, g_ref, wm_ref, ws_ref, bf_ref, gains_ref, cosa_ref, sina_ref,
                    cosi_ref, sini_ref, main_ref, small_ref, vt_ref, cneg_ref, ki_ref, h_scr,
                    raw_scr, carry_scr, *, n_seg, idx_scale, n_fb):
    i = pl.program_id(0)
    j = pl.program_id(1)
    tm = h_scr.shape[0]
    seg = wm_ref.shape[1]

    @pl.when(j == 0)
    def _():
        h_scr[...] = _rms(x_ref[...], g_ref[...]).astype(BF16)

    def head_norm(slab, gain):
        ms = jnp.mean(slab * slab, axis=-1, keepdims=True)
        return slab * lax.rsqrt(ms + RMS_EPS) * gain

    def produce():
        for c0 in range(0, seg, MXU_COLS):
            raw_scr[j % 2, :, c0:c0 + MXU_COLS] = jnp.dot(
                h_scr[...], wm_ref[:, c0:c0 + MXU_COLS], preferred_element_type=F32)

    def finish(epilogue):
        for c0 in range(0, seg, MXU_COLS):
            epilogue(raw_scr[(j + 1) % 2, :, c0:c0 + MXU_COLS], c0)

    def per_slab(fn):
        def epilogue(acc, c0):
            for s0 in range(0, MXU_COLS, LANES):
                y = fn(acc[:, s0:s0 + LANES])
                main_ref[0, :, c0 + s0:c0 + s0 + LANES] = y.astype(BF16)
        return epilogue

    def store_values(acc, c0):
        main_ref[0, :, c0:c0 + MXU_COLS] = acc.astype(BF16)
        vt_ref[0, 0, c0:c0 + MXU_COLS, :] = acc.T.astype(BF16)

    @pl.when(j == 0)
    def _():
        produce()

    @pl.when((j == 1) | (j == 2))
    def _():
        gain = gains_ref[pl.ds(j - 1, 1), :]
        finish(per_slab(lambda y: _rotate(head_norm(y, gain), cosa_ref[...], sina_ref[...],
                                          ROT_DIM // 2, LANES)))
        produce()

    @pl.when((j == 5) | (j == 6))
    def _():
        gain = gains_ref[pl.ds(j - 3, 1), :]
        finish(per_slab(lambda y: head_norm(y, gain)))
        produce()

    @pl.when((j == 3) | (j == 7))
    def _():
        finish(store_values)
        produce()

    @pl.when(j == 4)
    def _():
        finish(per_slab(lambda y: _rotate(y, cosi_ref[...], sini_ref[...], IDX_ROT_DIM // 2, IDX_DIM)))
        produce()

    @pl.when(j == n_seg)
    def _():
        finish(per_slab(jax.nn.sigmoid))
        raw_scr[j % 2, :, 0:LANES] = jnp.dot(h_scr[...], ws_ref[...], preferred_element_type=F32)

    @pl.when(j == n_seg + 1)
    def _():
        acc = raw_scr[(j + 1) % 2, :, 0:LANES]
        lane = lax.broadcasted_iota(jnp.int32, acc.shape, 1)
        is_ki = lane < WI_OFF
        cos_k = jnp.where(is_ki, cosi_ref[...], 1.0)
        sin_k = jnp.where(is_ki, sini_ref[...], 0.0)
        ki = _rotate(acc, cos_k, sin_k, IDX_ROT_DIM // 2, IDX_DIM)
        wi = acc * idx_scale
        z = acc + bf_ref[...]
        ls = jnp.minimum(z, 0.0) - jnp.log1p(jnp.exp(-jnp.abs(z)))
        is_fb = (lane >= FB_OFF) & (lane < FB_OFF + n_fb)
        ls = jnp.where(is_fb, ls, 0.0)
        row = lax.broadcasted_iota(jnp.int32, (tm, tm), 0)
        col = lax.broadcasted_iota(jnp.int32, (tm, tm), 1)
        tri = (col <= row).astype(BF16)
        p1 = ls.astype(BF16)
        r1 = ls - p1.astype(F32)
        p2 = r1.astype(BF16)
        p3 = (r1 - p2.astype(F32)).astype(BF16)
        c = (jnp.dot(tri, p1, preferred_element_type=F32)
             + jnp.dot(tri, p2, preferred_element_type=F32)
             + jnp.dot(tri, p3, preferred_element_type=F32))

        @pl.when(i == 0)
        def _():
            carry_scr[...] = jnp.zeros_like(carry_scr)

        c = c + carry_scr[...]
        carry_scr[...] = c[tm - 1:tm, :]
        small_ref[...] = jnp.where(is_ki, ki, jnp.where(lane < FB_OFF, wi, c))
        ki_ref[...] = jnp.where(is_ki, ki, 0.0).astype(BF16)
        for hb in range(n_fb):
            cneg_ref[hb] = jnp.broadcast_to(-LOG2_E * c[:, FB_OFF + hb:FB_OFF + hb + 1], (tm, LANES))


def _in_proj(x, g, w_main, w_small, bf_pad, gains, tabs, *, n_seg, seg, tm):
    s_len, d = x.shape
    cosa, sina, cosi, sini = tabs
    grid = (s_len // tm, n_seg + 2)
    row_spec = lambda w: pl.BlockSpec((tm, w), lambda i, j: (i, 0))
    finished = lambda j: jnp.clip(j - 1, 0, n_seg - 1)
    kern = functools.partial(_in_proj_kernel, n_seg=n_seg,
                             idx_scale=float((N_IDX_HEADS * IDX_DIM) ** -0.5), n_fb=N_HEADS_B)
    return pl.pallas_call(
        kern,
        grid=grid,
        in_specs=[
            pl.BlockSpec((tm, d), lambda i, j: (i, 0), pipeline_mode=pl.Buffered(1)),
            pl.BlockSpec((1, d), lambda i, j: (0, 0)),
            pl.BlockSpec((d, seg), lambda i, j: (0, jnp.minimum(j, n_seg - 1))),
            pl.BlockSpec((d, LANES), lambda i, j: (0, 0)),
            pl.BlockSpec((1, LANES), lambda i, j: (0, 0)),
            pl.BlockSpec((4, LANES), lambda i, j: (0, 0)),
            row_spec(LANES), row_spec(LANES), row_spec(LANES), row_spec(LANES),
        ],
        out_specs=[
            pl.BlockSpec((1, tm, seg), lambda i, j: (finished(j), i, 0)),
            pl.BlockSpec((tm, LANES), lambda i, j: (i, 0)),
            pl.BlockSpec((1, 1, seg, tm), lambda i, j: (jnp.where(finished(j) <= 2, 0, 1), i, 0, 0)),
            pl.BlockSpec((N_HEADS_B, tm, LANES), lambda i, j: (0, i, 0)),
            pl.BlockSpec((tm, LANES), lambda i, j: (i, 0)),
        ],
        out_shape=[
            jax.ShapeDtypeStruct((n_seg, s_len, seg), BF16),
            jax.ShapeDtypeStruct((s_len, LANES), F32),
            jax.ShapeDtypeStruct((2, s_len // tm, seg, tm), BF16),
            jax.ShapeDtypeStruct((N_HEADS_B, s_len, LANES), F32),
            jax.ShapeDtypeStruct((s_len, LANES), BF16),
        ],
        scratch_shapes=[
            pltpu.VMEM((tm, d), BF16),
            pltpu.VMEM((2, tm, seg), F32),
            pltpu.VMEM((1, LANES), F32),
        ],
        compiler_params=_cparams(("arbitrary", "arbitrary")),
        name="in_proj",
    )(x, g, w_main, w_small, bf_pad, gains, cosa, sina, cosi, sini)


def _transpose_heads(src, dst_scr):
    for s in range(src.shape[1] // LANES):
        sl = slice(s * LANES, (s + 1) * LANES)
        dst_scr[sl, :] = src[:, sl].astype(F32).T.astype(BF16)


def _logits_pass(h, z_t, z_scr, m_scr, mnew_scr):
    z_scr[h] = z_t
    mnew_scr[h:h + 1, :] = jnp.maximum(m_scr[h:h + 1, :], jnp.max(z_t, axis=0, keepdims=True))


def _softmax_pass(h, v_t, z_scr, m_scr, mnew_scr, l_scr, acc_scr):
    hs = slice(h * HEAD_DIM, (h + 1) * HEAD_DIM)
    m_new = mnew_scr[h:h + 1, :]
    alpha = jnp.exp2(m_scr[h:h + 1, :] - m_new)
    p_t = jnp.exp2(z_scr[h] - m_new)
    l_scr[h:h + 1, :] = alpha * l_scr[h:h + 1, :] + jnp.sum(p_t, axis=0, keepdims=True)
    acc_scr[hs, :] = alpha * acc_scr[hs, :] + jnp.dot(v_t, p_t.astype(BF16),
                                                      preferred_element_type=F32)
    m_scr[h:h + 1, :] = m_new


def _init_attn_state(m_scr, l_scr, acc_scr):
    m_scr[...] = jnp.full_like(m_scr, NEG_BIG)
    l_scr[...] = jnp.zeros_like(l_scr)
    acc_scr[...] = jnp.zeros_like(acc_scr)


def _dsa_t_kernel(qa_ref, qi_ref, wit_ref, ki_ref, ka_ref, vat_ref, o_ref,
                  key_scr, half_scr, qat_scr, qit_scr, z_scr, m_scr, mnew_scr, l_scr, acc_scr,
                  *, tq, tk, top_k, n_heads, n_idx):
    i = pl.program_id(0)
    n_kt = ((i + 1) * tq + tk - 1) // tk
    t_glob = i * tq + lax.broadcasted_iota(jnp.int32, (1, tq), 1)
    adm_len = (t_glob // CHUNK + 1) * CHUNK
    key_pos = lax.broadcasted_iota(jnp.int32, (tk, 1), 0)

    _transpose_heads(qa_ref[0], qat_scr)
    _transpose_heads(qi_ref[0], qit_scr)

    def score_tile(kt, carry):
        k0 = pl.multiple_of(kt * tk, tk)
        k_idx = ki_ref[pl.ds(k0, tk), 0:IDX_DIM]
        acc = jnp.zeros((tk, tq), F32)
        for h in range(n_idx):
            q_t = qit_scr[h * IDX_DIM:(h + 1) * IDX_DIM, :]
            rel = jnp.maximum(jnp.dot(k_idx, q_t, preferred_element_type=F32), 0.0)
            acc = acc + rel * wit_ref[h:h + 1, :]
        bits = lax.bitcast_convert_type(acc, jnp.int32)
        key = jnp.where(bits < 0, bits ^ INT_MAX, bits)
        key = jnp.where(k0 + key_pos < adm_len, key, KEY_NONE)
        key_scr[kt] = key
        half_scr[kt] = (key >> 16).astype(jnp.int16)
        return carry

    lax.fori_loop(0, n_kt, score_tile, 0)

    def count(pred):
        def body(kt, cnt):
            hit = jnp.where(pred(key_scr[kt], kt * tk + key_pos), 1, 0)
            return cnt + jnp.sum(hit.reshape(tk // 8, 8, tq), axis=0)

        cnt = lax.fori_loop(0, n_kt, body, jnp.zeros((8, tq), jnp.int32))
        return jnp.sum(cnt, axis=0, keepdims=True)

    def count_half_ge(mid):
        mid16 = mid.astype(jnp.int16)

        def body(kt, cnt):
            hit = jnp.where(half_scr[kt] >= mid16, jnp.int16(1), jnp.int16(0))
            parts = [hit[r * 16:(r + 1) * 16] for r in range(tk // 16)]
            while len(parts) > 1:
                parts = [parts[a] + parts[a + 1] for a in range(0, len(parts), 2)]
            return cnt + parts[0].astype(jnp.int32)

        cnt = lax.fori_loop(0, n_kt, body, jnp.zeros((16, tq), jnp.int32))
        return jnp.sum(cnt, axis=0, keepdims=True)

    def bisect_half(need_k, c_all):
        def step(_, state):
            lo, hi, c_lo, c_hi = state
            mid = (lo + hi) >> 1
            c_mid = count_half_ge(mid)
            take = c_mid >= need_k
            return (jnp.where(take, mid, lo), jnp.where(take, hi, mid),
                    jnp.where(take, c_mid, c_lo), jnp.where(take, c_hi, c_mid))

        x, _, c_x, c_above = lax.fori_loop(0, 16, step,
                                           (row(HALF_MIN), row(HALF_MAX + 1), c_all, row(0)))
        return x, c_x, c_above

    row = lambda v: jnp.full((1, tq), v, jnp.int32)
    thr_hi, c_ge_hi, c_gt_hi = bisect_half(top_k, n_kt * tk + row(0))
    thr_hi16 = thr_hi.astype(jnp.int16)

    def low_half_tile(kt, carry):
        low = ((key_scr[kt] & 0xFFFF) + HALF_MIN).astype(jnp.int16)
        half_scr[kt] = jnp.where(half_scr[kt] == thr_hi16, low, jnp.int16(HALF_MIN))
        return carry

    lax.fori_loop(0, n_kt, low_half_tile, 0)
    thr_lo, c_ge_lo, c_gt_lo = bisect_half(top_k - c_gt_hi, c_ge_hi - c_gt_hi)
    thr = thr_hi * 65536 + (thr_lo - HALF_MIN)
    c_ge = c_gt_hi + c_ge_lo
    c_gt = c_gt_hi + c_gt_lo
    short = thr <= KEY_NONE
    thr = jnp.maximum(thr, KEY_NONE)
    need = top_k - c_gt

    def tie_cut():
        def step(_, lohi):
            lo, hi = lohi
            mid = (lo + hi) >> 1
            ok = count(lambda k, pos: (k == thr) & (pos <= mid)) >= need
            return jnp.where(ok, lo, mid), jnp.where(ok, mid, hi)

        n_steps = int(np.ceil(np.log2(key_scr.shape[0] * tk))) + 1
        return lax.fori_loop(0, n_steps, step, (row(-1), n_kt * tk - 1 + row(0)))[1]

    has_tie = jnp.max(jnp.where((c_ge > top_k) & ~short, 1.0, 0.0)) > 0.5
    cut = lax.cond(has_tie, tie_cut, lambda: row(INT_MAX))
    cut = jnp.where(short, -1, cut)

    _init_attn_state(m_scr, l_scr, acc_scr)

    def attend(kt, carry):
        k0 = pl.multiple_of(kt * tk, tk)
        keys = key_scr[kt]
        sel = (keys > thr) | ((keys == thr) & (kt * tk + key_pos <= cut))
        bias = jnp.where(sel, 0.0, NEG_BIG).astype(F32)
        for h in range(n_heads):
            hs = slice(h * HEAD_DIM, (h + 1) * HEAD_DIM)
            z_t = jnp.dot(ka_ref[0, pl.ds(k0, tk), hs], qat_scr[hs, :],
                          preferred_element_type=F32) + bias
            _logits_pass(h, z_t, z_scr, m_scr, mnew_scr)
        for h in range(n_heads):
            hs = slice(h * HEAD_DIM, (h + 1) * HEAD_DIM)
            _softmax_pass(h, vat_ref[0, kt, hs, :], z_scr, m_scr, mnew_scr, l_scr, acc_scr)
        return carry

    lax.fori_loop(0, n_kt, attend, 0)
    for h in range(n_heads):
        hs = slice(h * HEAD_DIM, (h + 1) * HEAD_DIM)
        o_ref[:, hs] = (acc_scr[hs, :] / l_scr[h:h + 1, :]).T.astype(BF16)


def _dsa_t(main, vt, ki, wi_t, *, tq, tk, top_k):
    n_seg, s_len, seg = main.shape
    assert vt.shape == (2, s_len // tk, seg, tk)
    kern = functools.partial(_dsa_t_kernel, tq=tq, tk=tk, top_k=top_k, n_heads=N_HEADS_A,
                             n_idx=N_IDX_HEADS)
    resident = functools.partial(pl.BlockSpec, pipeline_mode=pl.Buffered(1))
    return pl.pallas_call(
        kern,
        grid=(s_len // tq,),
        in_specs=[
            pl.BlockSpec((1, tq, seg), lambda i: (0, i, 0)),
            pl.BlockSpec((1, tq, seg), lambda i: (3, i, 0)),
            pl.BlockSpec((N_IDX_HEADS, tq), lambda i: (0, i)),
            resident((s_len, LANES), lambda i: (0, 0)),
            resident((1, s_len, seg), lambda i: (1, 0, 0)),
            resident((1, s_len // tk, seg, tk), lambda i: (0, 0, 0, 0)),
        ],
        out_specs=pl.BlockSpec((tq, seg), lambda i: (i, 0)),
        out_shape=jax.ShapeDtypeStruct((s_len, seg), BF16),
        scratch_shapes=[
            pltpu.VMEM((s_len // tk, tk, tq), jnp.int32),
            pltpu.VMEM((s_len // tk, tk, tq), jnp.int16),
            pltpu.VMEM((seg, tq), BF16),
            pltpu.VMEM((seg, tq), BF16),
            pltpu.VMEM((N_HEADS_A, tk, tq), F32),
            pltpu.VMEM((N_HEADS_A, tq), F32),
            pltpu.VMEM((N_HEADS_A, tq), F32),
            pltpu.VMEM((N_HEADS_A, tq), F32),
            pltpu.VMEM((seg, tq), F32),
        ],
        compiler_params=_cparams(("arbitrary",)),
        name="dsa",
    )(main, main, wi_t, ki, main, vt)


def _fox_t_kernel(qt_ref, kt_ref, q_ref, k_ref, vt_ref, gate_ref, cneg_ref, o_ref,
                  qt_scr, z_scr, m_scr, mnew_scr, l_scr, acc_scr, *, t, n_heads):
    p = pl.program_id(0)
    qi = qt_ref[p]
    ki = kt_ref[p]

    @pl.when(ki == 0)
    def _():
        _transpose_heads(q_ref[0], qt_scr)
        _init_attn_state(m_scr, l_scr, acc_scr)

    def update(causal):
        key_i = lax.broadcasted_iota(jnp.int32, (t, LANES), 0)
        qry_i = lax.broadcasted_iota(jnp.int32, (t, LANES), 1)
        for h in range(n_heads):
            hs = slice(h * HEAD_DIM, (h + 1) * HEAD_DIM)
            z_t = jnp.dot(k_ref[0, :, hs], qt_scr[hs, :], preferred_element_type=F32)
            cols = []
            for c in range(t // LANES):
                zc = z_t[:, c * LANES:(c + 1) * LANES] + cneg_ref[h]
                if causal:
                    zc = jnp.where(key_i <= qry_i + c * LANES, zc, NEG_BIG)
                cols.append(zc)
            _logits_pass(h, jnp.concatenate(cols, axis=1), z_scr, m_scr, mnew_scr)
        for h in range(n_heads):
            hs = slice(h * HEAD_DIM, (h + 1) * HEAD_DIM)
            _softmax_pass(h, vt_ref[0, 0, hs, :], z_scr, m_scr, mnew_scr, l_scr, acc_scr)

    @pl.when(ki < qi)
    def _():
        update(causal=False)

    @pl.when(ki == qi)
    def _():
        update(causal=True)
        for h in range(n_heads):
            hs = slice(h * HEAD_DIM, (h + 1) * HEAD_DIM)
            o = (acc_scr[hs, :] / l_scr[h:h + 1, :]).T
            o_ref[:, hs] = (o * gate_ref[0, :, hs].astype(F32)).astype(BF16)


def _fox_t(main, vt, cneg, *, t):
    n_seg, s_len, seg = main.shape
    assert vt.shape == (2, s_len // t, seg, t)
    nq = s_len // t
    pairs = [(a, b) for a in range(nq) for b in range(a + 1)]
    q_tab = jnp.asarray([a for a, _ in pairs], jnp.int32)
    k_tab = jnp.asarray([b for _, b in pairs], jnp.int32)
    kern = functools.partial(_fox_t_kernel, t=t, n_heads=N_HEADS_B)
    grid_spec = pltpu.PrefetchScalarGridSpec(
        num_scalar_prefetch=2,
        grid=(len(pairs),),
        in_specs=[
            pl.BlockSpec((1, t, seg), lambda p, qt, kt: (4, qt[p], 0)),
            pl.BlockSpec((1, t, seg), lambda p, qt, kt: (5, kt[p], 0)),
            pl.BlockSpec((1, 1, seg, t), lambda p, qt, kt: (1, kt[p], 0, 0)),
            pl.BlockSpec((1, t, seg), lambda p, qt, kt: (7, qt[p], 0)),
            pl.BlockSpec((N_HEADS_B, t, LANES), lambda p, qt, kt: (0, kt[p], 0)),
        ],
        out_specs=pl.BlockSpec((t, seg), lambda p, qt, kt: (qt[p], 0)),
        scratch_shapes=[
            pltpu.VMEM((seg, t), BF16),
            pltpu.VMEM((N_HEADS_B, t, t), F32),
            pltpu.VMEM((N_HEADS_B, t), F32),
            pltpu.VMEM((N_HEADS_B, t), F32),
            pltpu.VMEM((N_HEADS_B, t), F32),
            pltpu.VMEM((seg, t), F32),
        ],
    )
    return pl.pallas_call(
        kern,
        grid_spec=grid_spec,
        out_shape=jax.ShapeDtypeStruct((s_len, seg), BF16),
        compiler_params=_cparams(("arbitrary",)),
        name="fox",
    )(q_tab, k_tab, main, main, vt, main, cneg)


def _out_proj_kernel(x_ref, oa_ref, ob_ref, wa_ref, wb_ref, y_ref):
    y = x_ref[...] + jnp.dot(oa_ref[...], wa_ref[...], preferred_element_type=F32)
    y_ref[...] = y + jnp.dot(ob_ref[...], wb_ref[...], preferred_element_type=F32)


def _out_proj(x, o_a, o_b, w_a, w_b, *, tm):
    s_len, d = x.shape
    seg = o_a.shape[1]
    resident = functools.partial(pl.BlockSpec, pipeline_mode=pl.Buffered(1))
    return pl.pallas_call(
        _out_proj_kernel,
        grid=(s_len // tm,),
        in_specs=[
            pl.BlockSpec((tm, d), lambda i: (i, 0)),
            pl.BlockSpec((tm, seg), lambda i: (i, 0)),
            pl.BlockSpec((tm, seg), lambda i: (i, 0)),
            resident((seg, d), lambda i: (0, 0)),
            resident((seg, d), lambda i: (0, 0)),
        ],
        out_specs=pl.BlockSpec((tm, d), lambda i: (i, 0)),
        out_shape=jax.ShapeDtypeStruct((s_len, d), F32),
        compiler_params=_cparams(("arbitrary",)),
        name="out_proj",
    )(x, o_a, o_b, w_a, w_b)


def _swiglu_step(h, wg, wu, wd):
    g = jnp.dot(h, wg.astype(BF16), preferred_element_type=F32)
    u = jnp.dot(h, wu.astype(BF16), preferred_element_type=F32)
    a = (g * jax.nn.sigmoid(g) * u).astype(BF16)
    return jnp.dot(a, wd.astype(BF16), preferred_element_type=F32)


def _ffn_kernel(x_ref, g_ref, wg_ref, wu_ref, wd_ref, y_ref, h_scr):
    @pl.when(pl.program_id(1) == 0)
    def _():
        x = x_ref[...]
        h_scr[...] = _rms(x, g_ref[...]).astype(BF16)
        y_ref[...] = x

    y_ref[...] += _swiglu_step(h_scr[...], wg_ref[...], wu_ref[...], wd_ref[...])


def _ffn(x, g, wg, wu, wd, *, tm, tf):
    s_len, d = x.shape
    f = wg.shape[1]
    return pl.pallas_call(
        _ffn_kernel,
        grid=(s_len // tm, f // tf),
        in_specs=[
            pl.BlockSpec((tm, d), lambda i, j: (i, 0), pipeline_mode=pl.Buffered(1)),
            pl.BlockSpec((1, d), lambda i, j: (0, 0)),
            pl.BlockSpec((d, tf), lambda i, j: (0, j)),
            pl.BlockSpec((d, tf), lambda i, j: (0, j)),
            pl.BlockSpec((tf, d), lambda i, j: (j, 0)),
        ],
        out_specs=pl.BlockSpec((tm, d), lambda i, j: (i, 0), pipeline_mode=pl.Buffered(1)),
        out_shape=jax.ShapeDtypeStruct((s_len, d), F32),
        scratch_shapes=[pltpu.VMEM((tm, d), BF16)],
        compiler_params=_cparams(("arbitrary", "arbitrary")),
        name="ffn",
    )(x, g, wg, wu, wd)


def _pool_kernel(x_ref, g_ref, w_ref, sc_ref, gm_ref, r_ref, y_ref, hp_ref, route_ref, cnt_ref,
                 ext_scr, carry_scr, *, halo, n_exp):
    i = pl.program_id(0)
    tm, d = x_ref.shape
    n_grp = len(POOL_WINDOWS)
    width = d // n_grp

    @pl.when(i == 0)
    def _():
        ext_scr[0:halo, :] = jnp.zeros((halo, d), F32)

    x = x_ref[...]
    ext_scr[halo:halo + tm, :] = _rms(x, g_ref[...])
    t_glob = i * tm + lax.broadcasted_iota(jnp.int32, (tm, 1), 0)
    for gi, win in enumerate(POOL_WINDOWS):
        cs = slice(gi * width, (gi + 1) * width)
        hn = ext_scr[halo:halo + tm, cs]
        tot = hn
        for dlt in range(1, win):
            tot = tot + ext_scr[halo - dlt:halo - dlt + tm, cs]
        count = jnp.minimum(t_glob + 1, win).astype(F32)
        pooled = (tot / count - hn).astype(BF16)
        y = jnp.dot(pooled, w_ref[gi], preferred_element_type=F32)
        y_ref[:, cs] = x[:, cs] + y * sc_ref[:, cs]
    ext_scr[0:halo, :] = ext_scr[tm:tm + halo, :]

    h = _rms(y_ref[...], gm_ref[...])
    hp_ref[...] = _pack_bf16_pairs(h)
    _route_rows(h, r_ref, route_ref, cnt_ref, carry_scr, n_exp=n_exp)


def _pool_route(x, g, w_pool, scale, g_moe, r_pad, *, tm, n_exp):
    s_len, d = x.shape
    halo = 16
    assert max(POOL_WINDOWS) <= halo
    n_grp, width, _ = w_pool.shape
    row_out = lambda w: pl.BlockSpec((tm, w), lambda i: (i, 0))
    return pl.pallas_call(
        functools.partial(_pool_kernel, halo=halo, n_exp=n_exp),
        grid=(s_len // tm,),
        in_specs=[
            pl.BlockSpec((tm, d), lambda i: (i, 0)),
            pl.BlockSpec((1, d), lambda i: (0, 0)),
            pl.BlockSpec((n_grp, width, width), lambda i: (0, 0, 0)),
            pl.BlockSpec((1, d), lambda i: (0, 0)),
            pl.BlockSpec((1, d), lambda i: (0, 0)),
            pl.BlockSpec((2, d, LANES), lambda i: (0, 0, 0)),
        ],
        out_specs=[row_out(d), row_out(d // 2), row_out(LANES),
                   pl.BlockSpec((1, LANES), lambda i: (0, 0))],
        out_shape=[
            jax.ShapeDtypeStruct((s_len, d), F32),
            jax.ShapeDtypeStruct((s_len, d // 2), jnp.uint32),
            jax.ShapeDtypeStruct((s_len, LANES), F32),
            jax.ShapeDtypeStruct((1, LANES), F32),
        ],
        scratch_shapes=[pltpu.VMEM((tm + halo, d), F32), pltpu.VMEM((1, LANES), F32)],
        compiler_params=_cparams(("arbitrary",)),
        name="pool_route",
    )(x, g, w_pool, scale, g_moe, r_pad)


R_E1, R_E2, R_G1, R_G2, R_K1, R_K2 = range(6)


def _route_rows(h, r_ref, route_ref, cnt_ref, carry_scr, *, n_exp):
    tm = h.shape[0]

    @pl.when(pl.program_id(0) == 0)
    def _():
        carry_scr[...] = jnp.zeros_like(carry_scr)

    h_hi = h.astype(BF16)
    h_lo = (h - h_hi.astype(F32)).astype(BF16)
    logits = (jnp.dot(h_hi, r_ref[0], preferred_element_type=F32)
              + jnp.dot(h_hi, r_ref[1], preferred_element_type=F32)
              + jnp.dot(h_lo, r_ref[0], preferred_element_type=F32))
    lane = lax.broadcasted_iota(jnp.int32, logits.shape, 1)
    logits = jnp.where(lane < n_exp, logits, -jnp.inf)
    lane_f = lane.astype(F32)
    v1 = jnp.max(logits, axis=1, keepdims=True)
    e1 = jnp.min(jnp.where(logits == v1, lane_f, float(LANES)), axis=1, keepdims=True)
    rest = jnp.where(lane_f == e1, -jnp.inf, logits)
    v2 = jnp.max(rest, axis=1, keepdims=True)
    e2 = jnp.min(jnp.where(rest == v2, lane_f, float(LANES)), axis=1, keepdims=True)
    ex = jnp.exp(v2 - v1)
    g1 = 1.0 / (1.0 + ex)
    g2 = ex / (1.0 + ex)
    oh1 = (lane_f == e1).astype(F32)
    oh2 = (lane_f == e2).astype(F32)
    both = oh1 + oh2
    row = lax.broadcasted_iota(jnp.int32, (tm, tm), 0)
    col = lax.broadcasted_iota(jnp.int32, (tm, tm), 1)
    tri = (col < row).astype(BF16)
    before = jnp.dot(tri, both.astype(BF16), preferred_element_type=F32) + carry_scr[...]
    k1 = jnp.sum(before * oh1, axis=1, keepdims=True)
    k2 = jnp.sum(before * oh2, axis=1, keepdims=True)
    carry_scr[...] = carry_scr[...] + jnp.sum(both, axis=0, keepdims=True)
    cnt_ref[...] = carry_scr[...]
    vals = (e1, e2, g1, g2, k1, k2)
    out = jnp.zeros(logits.shape, F32)
    for li, v in enumerate(vals):
        out = jnp.where(lane == li, v, out)
    route_ref[...] = out


def _pack_bf16_pairs(h):
    half = h.shape[1] // 2
    lo = lax.bitcast_convert_type(h[:, :half].astype(BF16).astype(F32), jnp.uint32)
    hi = lax.bitcast_convert_type(h[:, half:].astype(BF16).astype(F32), jnp.uint32)
    return (lo >> 16) | (hi & jnp.uint32(0xFFFF0000))


def _unpack_bf16_pairs(u):
    lo = lax.bitcast_convert_type(u << 16, F32).astype(BF16)
    hi = lax.bitcast_convert_type(u & jnp.uint32(0xFFFF0000), F32).astype(BF16)
    return lo, hi


ROW_COPY_UNROLL = 8


def _run_row_copies(copies, n):
    def start(t, c):
        a, b = copies(t)
        a.start(priority=0)
        b.start(priority=1)
        return c

    def wait(t, c):
        a, b = copies(t)
        a.wait()
        b.wait()
        return c

    lax.fori_loop(0, n, start, 0, unroll=ROW_COPY_UNROLL)
    lax.fori_loop(0, n, wait, 0, unroll=ROW_COPY_UNROLL)


def _start_row_copies(copies, n):
    def start(t, c):
        a, b = copies(t)
        a.start(priority=0)
        b.start(priority=1)
        return c

    lax.fori_loop(0, n, start, 0, unroll=ROW_COPY_UNROLL)


def _wait_row_copies(copies, n):
    def wait(t, c):
        a, b = copies(t)
        a.wait()
        b.wait()
        return c

    lax.fori_loop(0, n, wait, 0, unroll=ROW_COPY_UNROLL)


def _scatter_kernel(p1_ref, p2_ref, hp_ref, xs_in_ref, xs_ref, sem):
    del xs_in_ref
    i = pl.program_id(0)
    tm = hp_ref.shape[0]

    def copies(t):
        src = hp_ref.at[pl.ds(t, 1), :]
        return (pltpu.make_async_copy(src, xs_ref.at[pl.ds(p1_ref[i * tm + t], 1), :], sem.at[0]),
                pltpu.make_async_copy(src, xs_ref.at[pl.ds(p2_ref[i * tm + t], 1), :], sem.at[1]))

    _run_row_copies(copies, tm)


def _scatter(pos1, pos2, hp, xs_zero, *, tm):
    s_len, half = hp.shape
    grid_spec = pltpu.PrefetchScalarGridSpec(
        num_scalar_prefetch=2,
        grid=(s_len // tm,),
        in_specs=[
            pl.BlockSpec((tm, half), lambda i, p1, p2: (i, 0)),
            pl.BlockSpec(memory_space=pl.ANY),
        ],
        out_specs=pl.BlockSpec(memory_space=pl.ANY),
        scratch_shapes=[pltpu.SemaphoreType.DMA((2,))],
    )
    return pl.pallas_call(
        _scatter_kernel,
        grid_spec=grid_spec,
        out_shape=jax.ShapeDtypeStruct(xs_zero.shape, jnp.uint32),
        input_output_aliases={3: 0},
        compiler_params=_cparams(("arbitrary",)),
        name="moe_scatter",
    )(pos1, pos2, hp, xs_zero)


def _experts_kernel(e_ref, v_ref, xs_ref, wg_ref, wu_ref, wd_ref, o_ref, xb_scr):
    j = pl.program_id(0)
    f = pl.program_id(1)
    half = xs_ref.shape[1]

    @pl.when(f == 0)
    def _():
        o_ref[...] = jnp.zeros_like(o_ref)

    @pl.when(v_ref[j] > 0)
    def _():
        @pl.when(f == 0)
        def _():
            lo, hi = _unpack_bf16_pairs(xs_ref[...])
            xb_scr[:, :half] = lo
            xb_scr[:, half:] = hi

        o_ref[...] += _swiglu_step(xb_scr[...], wg_ref[0], wu_ref[0], wd_ref[0])


def _experts(item_e, item_rows, xs, wg, wu, wd, *, r, tf):
    rows, half = xs.shape
    n_exp, d, f = wg.shape
    n_items = rows // r
    n_f = f // tf

    def f_idx(fi, j, v):
        return jnp.where(v[j] > 0, fi, n_f - 1)

    grid_spec = pltpu.PrefetchScalarGridSpec(
        num_scalar_prefetch=2,
        grid=(n_items, n_f),
        in_specs=[
            pl.BlockSpec((r, half), lambda j, fi, e, v: (j, 0)),
            pl.BlockSpec((1, d, tf), lambda j, fi, e, v: (e[j], 0, f_idx(fi, j, v))),
            pl.BlockSpec((1, d, tf), lambda j, fi, e, v: (e[j], 0, f_idx(fi, j, v))),
            pl.BlockSpec((1, tf, d), lambda j, fi, e, v: (e[j], f_idx(fi, j, v), 0)),
        ],
        out_specs=pl.BlockSpec((r, d), lambda j, fi, e, v: (j, 0)),
        scratch_shapes=[pltpu.VMEM((r, d), BF16)],
    )
    return pl.pallas_call(
        _experts_kernel,
        grid_spec=grid_spec,
        out_shape=jax.ShapeDtypeStruct((rows, d), F32),
        compiler_params=_cparams(("arbitrary", "arbitrary")),
        name="moe_experts",
    )(item_e, item_rows, xs, wg, wu, wd)


def _combine_kernel(p1_ref, p2_ref, x_ref, route_ref, o_hbm, y_ref, a_scr, b_scr, sem):
    i = pl.program_id(0)
    tm = x_ref.shape[0]

    def copies_of(tile):
        slot = tile % 2

        def copies(t):
            return (pltpu.make_async_copy(o_hbm.at[pl.ds(p1_ref[tile * tm + t], 1), :],
                                          a_scr.at[slot, pl.ds(t, 1), :], sem.at[slot, 0]),
                    pltpu.make_async_copy(o_hbm.at[pl.ds(p2_ref[tile * tm + t], 1), :],
                                          b_scr.at[slot, pl.ds(t, 1), :], sem.at[slot, 1]))
        return copies

    @pl.when(i == 0)
    def _():
        _start_row_copies(copies_of(i), tm)

    @pl.when(i + 1 < pl.num_programs(0))
    def _():
        _start_row_copies(copies_of(i + 1), tm)

    _wait_row_copies(copies_of(i), tm)
    route = route_ref[...]
    g1 = route[:, R_G1:R_G1 + 1]
    g2 = route[:, R_G2:R_G2 + 1]
    slot = i % 2
    y_ref[...] = x_ref[...] + g1 * a_scr[slot] + g2 * b_scr[slot]


def _combine(pos1, pos2, x, route, o_sorted, *, tm):
    s_len, d = x.shape
    grid_spec = pltpu.PrefetchScalarGridSpec(
        num_scalar_prefetch=2,
        grid=(s_len // tm,),
        in_specs=[
            pl.BlockSpec((tm, d), lambda i, p1, p2: (i, 0)),
            pl.BlockSpec((tm, LANES), lambda i, p1, p2: (i, 0)),
            pl.BlockSpec(memory_space=pl.ANY),
        ],
        out_specs=pl.BlockSpec((tm, d), lambda i, p1, p2: (i, 0)),
        scratch_shapes=[
            pltpu.VMEM((2, tm, d), F32),
            pltpu.VMEM((2, tm, d), F32),
            pltpu.SemaphoreType.DMA((2, 2)),
        ],
    )
    return pl.pallas_call(
        _combine_kernel,
        grid_spec=grid_spec,
        out_shape=jax.ShapeDtypeStruct((s_len, d), F32),
        compiler_params=_cparams(("arbitrary",)),
        name="moe_combine",
    )(pos1, pos2, x, route, o_sorted)


def _rotary_tables(s_len, rot_dim, period):
    half = rot_dim // 2
    inv_freq = 1.0 / (ROPE_THETA ** (jnp.arange(half, dtype=F32) * 2.0 / rot_dim))
    ang = jnp.arange(s_len, dtype=F32)[:, None] * inv_freq[None, :]
    cos, sin = jnp.cos(ang), jnp.sin(ang)
    pad = period - rot_dim
    cos_p = jnp.concatenate([cos, cos, jnp.ones((s_len, pad), F32)], axis=1)
    sin_p = jnp.concatenate([-sin, sin, jnp.zeros((s_len, pad), F32)], axis=1)
    reps = LANES // period
    return jnp.tile(cos_p, (1, reps)), jnp.tile(sin_p, (1, reps))


def _split_w_in(w_in):
    seg_a = N_HEADS_A * HEAD_DIM
    seg_i = N_IDX_HEADS * IDX_DIM
    seg_b = N_HEADS_B * HEAD_DIM
    sizes = (seg_a, seg_a, seg_a, seg_i, IDX_DIM, N_IDX_HEADS, seg_b, seg_b, seg_b, N_HEADS_B, seg_b)
    offs = np.cumsum(sizes)[:-1].tolist()
    return jnp.split(w_in, offs, axis=1)


def _moe_tables(counts, e1, e2, k1, k2, *, r, n_items):
    n_exp = counts.shape[0]
    blocks = (counts + r - 1) // r
    bend = jnp.cumsum(blocks)
    bstart = bend - blocks
    j = jnp.arange(n_items, dtype=jnp.int32)
    item_e = jnp.sum((j[:, None] >= bend[None, :]).astype(jnp.int32), axis=1)
    valid = item_e < n_exp
    e_c = jnp.minimum(item_e, n_exp - 1)
    item_rows = jnp.where(valid, jnp.clip(counts[e_c] - (j - bstart[e_c]) * r, 0, r), 0)
    last_e = jnp.minimum(item_e[jnp.maximum(bend[-1] - 1, 0)], n_exp - 1)
    item_e = jnp.where(valid, item_e, last_e).astype(jnp.int32)
    pos1 = bstart[e1] * r + k1
    pos2 = bstart[e2] * r + k2
    return item_e, item_rows.astype(jnp.int32), pos1.astype(jnp.int32), pos2.astype(jnp.int32)


def _moe_block_rows(n_assign, n_exp):
    share = -(-n_assign // (2 * n_exp))
    return -(-(share + 3 * share // 64) // BF16_SUBLANES) * BF16_SUBLANES


class _Tiles(NamedTuple):
    attn: int
    dsa_q: int
    out_rows: int
    ffn_rows: int
    ffn_cols: int
    pool_rows: int
    moe_rows: int
    expert_cols: int


def _tiles(s_len, d_ff, d_ff_expert):
    return _Tiles(attn=_tile(s_len, 512), dsa_q=_tile(s_len, 256), out_rows=_tile(s_len, 512),
                  ffn_rows=_tile(s_len, 1024), ffn_cols=_tile(d_ff, 512),
                  pool_rows=_tile(s_len, 512), moe_rows=_tile(s_len, 512),
                  expert_cols=_tile(d_ff_expert, 256))


def kernel(x, attn_norm_e, w_in_e, b_forget_e, q_norm_a_e, k_norm_a_e, q_norm_b_e, k_norm_b_e,
           w_out_e, ffn_norm_e, w_gate_e, w_up_e, w_down_e, pool_norm_o, w_pool_o, pool_scale_o,
           moe_norm_o, router_o, w_gate_o, w_up_o, w_down_o):
    b, s_len, d = x.shape
    assert b == 1
    x0 = x[0]
    tiles = _tiles(s_len, w_gate_e.shape[2], w_gate_o.shape[3])
    seg = N_HEADS_A * HEAD_DIM
    assert seg == N_IDX_HEADS * IDX_DIM == N_HEADS_B * HEAD_DIM
    assert N_IDX_HEADS <= FB_OFF - WI_OFF and N_HEADS_B <= LANES - FB_OFF
    row = lambda v: v.reshape(1, -1)

    qa, ka, va, qi, ki, wi, qb, kb, vb, fb, gb = _split_w_in(w_in_e[0])
    w_main = jnp.concatenate([qa, ka, va, qi, qb, kb, vb, gb], axis=1).astype(BF16)
    w_small = jnp.concatenate(
        [ki, wi, jnp.zeros((d, FB_OFF - WI_OFF - N_IDX_HEADS), F32), fb,
         jnp.zeros((d, LANES - FB_OFF - N_HEADS_B), F32)], axis=1).astype(BF16)
    bf_pad = jnp.zeros((1, LANES), F32).at[0, FB_OFF:FB_OFF + N_HEADS_B].set(b_forget_e[0])
    scale = HEAD_DIM ** -0.5 * LOG2_E
    gains = jnp.stack([q_norm_a_e[0] * scale, k_norm_a_e[0], q_norm_b_e[0] * scale, k_norm_b_e[0]])
    tabs = _rotary_tables(s_len, ROT_DIM, HEAD_DIM) + _rotary_tables(s_len, IDX_ROT_DIM, IDX_DIM)
    main, small, v_t, c_neg, ki_rot = _in_proj(x0, row(attn_norm_e[0]), w_main, w_small, bf_pad, gains,
                                               tabs, n_seg=8, seg=seg, tm=tiles.attn)
    wi_t = small[:, WI_OFF:WI_OFF + N_IDX_HEADS].T

    top_k = min(TOPK_MAX, s_len // 4)
    o_a = _dsa_t(main, v_t, ki_rot, wi_t, tq=tiles.dsa_q, tk=tiles.attn, top_k=top_k)
    o_b = _fox_t(main, v_t, c_neg, t=tiles.attn)
    w_out = w_out_e[0].astype(BF16)
    x1 = _out_proj(x0, o_a, o_b, w_out[:seg], w_out[seg:], tm=tiles.out_rows)

    x2 = _ffn(x1, row(ffn_norm_e[0]), w_gate_e[0], w_up_e[0], w_down_e[0],
              tm=tiles.ffn_rows, tf=tiles.ffn_cols)

    n_exp = router_o.shape[2]
    r_f32 = jnp.zeros((d, LANES), F32).at[:, :n_exp].set(router_o[0])
    r_hi = r_f32.astype(BF16)
    r_pad = jnp.stack([r_hi, (r_f32 - r_hi.astype(F32)).astype(BF16)])
    x3, h3_packed, route, counts = _pool_route(
        x2, row(pool_norm_o[0]), w_pool_o[0].astype(BF16), row(pool_scale_o[0]),
        row(moe_norm_o[0]), r_pad, tm=tiles.pool_rows, n_exp=n_exp)

    r_rows = _moe_block_rows(2 * s_len, n_exp)
    n_items = (2 * s_len) // r_rows + n_exp
    as_int = lambda c: route[:, c].astype(jnp.int32)
    item_e, item_rows, pos1, pos2 = _moe_tables(
        counts[0, :n_exp].astype(jnp.int32), as_int(R_E1), as_int(R_E2), as_int(R_K1), as_int(R_K2),
        r=r_rows, n_items=n_items)
    xs_zero = jnp.zeros((n_items * r_rows, d // 2), jnp.uint32)
    xs = _scatter(pos1, pos2, h3_packed, xs_zero, tm=tiles.moe_rows)
    o_sorted = _experts(item_e, item_rows, xs, w_gate_o[0], w_up_o[0], w_down_o[0],
                        r=r_rows, tf=tiles.expert_cols)
    y = _combine(pos1, pos2, x3, route, o_sorted, tm=tiles.moe_rows)
    return y[None]
```
